```python
import jax, jax.numpy as jnp
from jax import lax
import numpy as np

D_MODEL = 2048
BATCH = 4
SEQ = 2048
DEPTH = 2

RET_HEADS = 8
RET_HEAD_DIM = D_MODEL // 16
RET_WIDTH = RET_HEADS * RET_HEAD_DIM
RET_CHUNK = 128
ROPE_BASE = 10000.0
SB_HEADS = 8
SB_HEAD_DIM = D_MODEL // 16
SB_WIDTH = SB_HEADS * SB_HEAD_DIM
SB_BLOCK = 128
IN_WIDTH = 4 * RET_WIDTH + 3 * SB_WIDTH
N_BRANCHES = 2
D_FF = 11 * D_MODEL // 4
N_EXPERTS = 8
TOP_K = 2
D_FF_EXPERT = 7 * D_MODEL // 2
MOE_BLOCK = 256
N_DENSE = (DEPTH + 1) // 2
N_MOE = DEPTH // 2
PLE_DIM = 256
DEEPNORM_ALPHA = (2 * DEPTH) ** 0.25
DEEPNORM_BETA = (8 * DEPTH) ** -0.25
LN_EPS = 1e-5

kernel_name = "hybrid_retention_stickbreaking_moe_deepnorm"


def layer_norm(x, g, b):
    xf = x.astype(jnp.float32)
    mu = jnp.mean(xf, axis=-1, keepdims=True)
    xc = xf - mu
    var = jnp.mean(xc * xc, axis=-1, keepdims=True)
    y = xc * lax.rsqrt(var + LN_EPS) * g.astype(jnp.float32) + b.astype(jnp.float32)
    return y.astype(x.dtype)


def head_norm(o):
    mu = jnp.mean(o, axis=-1, keepdims=True)
    oc = o - mu
    var = jnp.mean(oc * oc, axis=-1, keepdims=True)
    return oc * lax.rsqrt(var + LN_EPS)


def rotary(t, pos):
    half = t.shape[-1] // 2
    inv = ROPE_BASE ** (-jnp.arange(half, dtype=jnp.float32) / half)
    ang = pos[:, None] * inv[None, :]
    cos = jnp.cos(ang)[None, :, None, :]
    sin = jnp.sin(ang)[None, :, None, :]
    t1, t2 = t[..., :half], t[..., half:]
    return jnp.concatenate([t1 * cos - t2 * sin, t1 * sin + t2 * cos], axis=-1)


def retention(q, k, v):
    B, S, H, d = q.shape
    dv = v.shape[-1]
    C = RET_CHUNK
    NC = S // C
    gamma = 1.0 - jnp.exp2(-5.0 - jnp.arange(H, dtype=jnp.float32))
    lg = jnp.log(gamma)
    idx = jnp.arange(C, dtype=jnp.float32)
    diff = idx[:, None] - idx[None, :]
    causal = diff >= 0
    decay = jnp.where(causal[None], jnp.exp(jnp.where(causal, diff, 0.0)[None] * lg[:, None, None]), 0.0)
    xi = jnp.exp((idx + 1.0)[None, :] * lg[:, None])
    zeta = jnp.exp((C - 1.0 - idx)[None, :] * lg[:, None])
    chunk_decay = jnp.exp(C * lg)

    def to_chunks(t):
        return t.reshape(B, NC, C, H, t.shape[-1]).transpose(1, 0, 3, 2, 4)

    def step(R, qkv):
        qc, kc, vc = qkv
        inner = jnp.einsum('bhnd,bhmd->bhnm', qc, kc) * decay[None]
        o = (jnp.einsum('bhnm,bhme->bhne', inner, vc)
             + jnp.einsum('bhnd,bhde->bhne', qc, R) * xi[None, :, :, None])
        R = (R * chunk_decay[None, :, None, None]
             + jnp.einsum('bhmd,bhme->bhde', kc * zeta[None, :, :, None], vc))
        return R, o

    R0 = jnp.zeros((B, H, d, dv), jnp.float32)
    _, o = lax.scan(step, R0, (to_chunks(q), to_chunks(k), to_chunks(v)))
    return o.transpose(1, 0, 3, 2, 4).reshape(B, S, H, dv)


def stick_breaking(q, k, v):
    B, S, H, d = q.shape
    scale = d ** -0.5
    outs = []
    for t0 in range(0, S, SB_BLOCK):
        L = t0 + SB_BLOCK
        qb, kp, vp = q[:, t0:L], k[:, :L], v[:, :L]
        z = jnp.einsum('bqhd,bkhd->bhqk', qb, kp) * scale
        mask = jnp.arange(L)[None, :] < (t0 + jnp.arange(SB_BLOCK))[:, None]
        log_fail = jnp.where(mask, jax.nn.log_sigmoid(-z), 0.0)
        later = lax.cumsum(log_fail, axis=3, reverse=True) - log_fail
        w = jnp.where(mask, jnp.exp(jax.nn.log_sigmoid(z) + later), 0.0)
        outs.append(jnp.einsum('bhqk,bkhd->bqhd', w, vp))
    return jnp.concatenate(outs, axis=1)


def token_mixer(x, w_in, w_br_ret, w_br_sb, w_gate, b_gate, w_o):
    B, S, D = x.shape
    h = x @ w_in
    cuts = [RET_WIDTH, 2 * RET_WIDTH, 3 * RET_WIDTH, 4 * RET_WIDTH,
            4 * RET_WIDTH + SB_WIDTH, 4 * RET_WIDTH + 2 * SB_WIDTH]
    rq, rk, rv, rg, sq, sk, sv = jnp.split(h, cuts, axis=-1)
    f32 = jnp.float32
    pos = jnp.arange(S, dtype=f32)

    def ret_heads(t):
        return t.reshape(B, S, RET_HEADS, RET_HEAD_DIM).astype(f32)

    def sb_heads(t):
        return t.reshape(B, S, SB_HEADS, SB_HEAD_DIM).astype(f32)

    q_r = rotary(ret_heads(rq), pos)
    k_r = rotary(ret_heads(rk), pos) * (RET_HEAD_DIM ** -0.5)
    o_ret = head_norm(retention(q_r, k_r, ret_heads(rv))).reshape(B, S, RET_WIDTH)
    o_ret = o_ret.astype(x.dtype) * jax.nn.silu(rg)
    o_sb = stick_breaking(sb_heads(sq), sb_heads(sk), sb_heads(sv)).reshape(B, S, SB_WIDTH).astype(x.dtype)
    gates = jax.nn.sigmoid(x @ w_gate + b_gate)
    g_ret, g_sb = jnp.split(gates, N_BRANCHES, axis=-1)
    merged = g_ret * (o_ret @ w_br_ret) + g_sb * (o_sb @ w_br_sb)
    return merged @ w_o


def swiglu(x, w1, w3, w2):
    return (jax.nn.silu(x @ w1) * (x @ w3)) @ w2


def moe_swiglu(x, w_router, w1, w3, w2):
    B, S, D = x.shape
    N = B * S
    NK = N * TOP_K
    M = MOE_BLOCK
    NB = -(-NK // M) + N_EXPERTS
    P = NB * M
    xf = x.reshape(N, D)
    logits = (xf @ w_router).astype(jnp.float32)
    top_v, top_i = lax.top_k(logits, TOP_K)
    gate = jax.nn.softmax(top_v, axis=-1)
    e_flat = top_i.reshape(NK).astype(jnp.int32)
    tok_flat = jnp.repeat(jnp.arange(N, dtype=jnp.int32), TOP_K)
    g_flat = gate.reshape(NK)
    order = jnp.argsort(e_flat, stable=True)
    se, stok, sg = e_flat[order], tok_flat[order], g_flat[order]
    counts = jax.ops.segment_sum(jnp.ones_like(e_flat), e_flat, num_segments=N_EXPERTS)
    start = jnp.cumsum(counts) - counts
    padded = (counts + M - 1) // M * M
    pad_end = jnp.cumsum(padded)
    pad_off = pad_end - padded
    dest = pad_off[se] + (jnp.arange(NK, dtype=jnp.int32) - start[se])
    buf_tok = jnp.full((P,), N, jnp.int32).at[dest].set(stok)
    buf_gate = jnp.zeros((P,), jnp.float32).at[dest].set(sg)
    block_e = jnp.clip(jnp.searchsorted(pad_end, jnp.arange(NB, dtype=jnp.int32) * M, side='right'),
                       0, N_EXPERTS - 1)
    xpad = jnp.concatenate([xf, jnp.zeros((1, D), x.dtype)], axis=0)
    xb = xpad[buf_tok].reshape(NB, M, D)

    def expert_block(args):
        xblk, e = args
        return (jax.nn.silu(xblk @ w1[e]) * (xblk @ w3[e])) @ w2[e]

    yb = lax.map(expert_block, (xb, block_e)).reshape(P, D)
    out = jnp.zeros((N + 1, D), x.dtype).at[buf_tok].add(yb * buf_gate[:, None].astype(x.dtype))
    return out[:N].reshape(B, S, D)


def setup_inputs(seed: int = 0) -> dict:
    key = jax.random.key(seed)
    ks = jax.random.split(key, 24)
    f32 = jnp.float32

    def nrm(k, shape, scale):
        return jax.random.normal(k, shape, f32) * scale

    D = D_MODEL
    return {
        "x": nrm(ks[0], (BATCH, SEQ, D), 1.0),
        "p": nrm(ks[1], (DEPTH, BATCH, SEQ, PLE_DIM), 1.0),
        "w_in": nrm(ks[2], (DEPTH, D, IN_WIDTH), D ** -0.5),
        "w_br_ret": nrm(ks[3], (DEPTH, RET_WIDTH, D), RET_WIDTH ** -0.5),
        "w_br_sb": nrm(ks[4], (DEPTH, SB_WIDTH, D), SB_WIDTH ** -0.5),
        "w_gate": nrm(ks[5], (DEPTH, D, N_BRANCHES * D), D ** -0.5),
        "b_gate": nrm(ks[6], (DEPTH, N_BRANCHES * D), 0.01),
        "w_o": nrm(ks[7], (DEPTH, D, D), D ** -0.5 * DEEPNORM_BETA),
        "ln1_g": 1.0 + nrm(ks[8], (DEPTH, D), 0.02),
        "ln1_b": nrm(ks[9], (DEPTH, D), 0.02),
        "ffn_w1": nrm(ks[10], (N_DENSE, D, D_FF), D ** -0.5),
        "ffn_w3": nrm(ks[11], (N_DENSE, D, D_FF), D ** -0.5),
        "ffn_w2": nrm(ks[12], (N_DENSE, D_FF, D), D_FF ** -0.5 * DEEPNORM_BETA),
        "moe_router": nrm(ks[13], (N_MOE, D, N_EXPERTS), D ** -0.5),
        "moe_w1": nrm(ks[14], (N_MOE, N_EXPERTS, D, D_FF_EXPERT), D ** -0.5),
        "moe_w3": nrm(ks[15], (N_MOE, N_EXPERTS, D, D_FF_EXPERT), D ** -0.5),
        "moe_w2": nrm(ks[16], (N_MOE, N_EXPERTS, D_FF_EXPERT, D), D_FF_EXPERT ** -0.5 * DEEPNORM_BETA),
        "ple_w": nrm(ks[17], (DEPTH, PLE_DIM, D), PLE_DIM ** -0.5),
        "ple_gate_w": nrm(ks[18], (DEPTH, D, D), D ** -0.5),
        "ln2_g": 1.0 + nrm(ks[19], (DEPTH, D), 0.02),
        "ln2_b": nrm(ks[20], (DEPTH, D), 0.02),
    }


def reference(x, p, w_in, w_br_ret, w_br_sb, w_gate, b_gate, w_o, ln1_g, ln1_b,
              ffn_w1, ffn_w3, ffn_w2, moe_router, moe_w1, moe_w3, moe_w2,
              ple_w, ple_gate_w, ln2_g, ln2_b):
    for i in range(DEPTH):
        mix = token_mixer(x, w_in[i], w_br_ret[i], w_br_sb[i], w_gate[i], b_gate[i], w_o[i])
        x = layer_norm(DEEPNORM_ALPHA * x + mix, ln1_g[i], ln1_b[i])
        if i % 2 == 0:
            f = swiglu(x, ffn_w1[i // 2], ffn_w3[i // 2], ffn_w2[i // 2])
        else:
            f = moe_swiglu(x, moe_router[i // 2], moe_w1[i // 2], moe_w3[i // 2], moe_w2[i // 2])
        ple = jax.nn.sigmoid(x @ ple_gate_w[i]) * (p[i] @ ple_w[i])
        x = layer_norm(DEEPNORM_ALPHA * x + f + ple, ln2_g[i], ln2_b[i])
    return x
```

```python
import functools
import math

import jax
import jax.numpy as jnp
from jax import lax
from jax.experimental import pallas as pl
from jax.experimental.pallas import tpu as pltpu

F32 = jnp.float32
BF16 = jnp.bfloat16

RET_HEADS = 8
SB_HEADS = 8
HEAD_DIM = 128
RET_CHUNK = 128
ROPE_BASE = 10000.0
TOP_K = 2
DEPTH = 2
DEEPNORM_ALPHA = (2 * DEPTH) ** 0.25
LN_EPS = 1e-5

V7X_VMEM_BYTES = 64 * 1024 * 1024
VMEM_LIMIT = V7X_VMEM_BYTES - 8 * 1024 * 1024
LANES = 128

SB_BLOCK = 128
FFN_ROWS = 1024
FFN_SUB = 256
FFN_TF = 256
LN_ROWS = 256
ROUTE_ROWS = 512
GATHER_ROWS = FFN_SUB


def _params(sem):
    return pltpu.CompilerParams(dimension_semantics=sem, vmem_limit_bytes=VMEM_LIMIT)


def _mm_kernel(*refs, n_prod, n_extra, epilogue):
    a_refs = refs[:n_prod]
    b_refs = refs[n_prod:2 * n_prod]
    e_refs = refs[2 * n_prod:2 * n_prod + n_extra]
    o_ref = refs[2 * n_prod + n_extra]
    b_scr = refs[2 * n_prod + n_extra + 1:]

    @pl.when(pl.program_id(1) == 0)
    def _():
        for b_ref, s in zip(b_refs, b_scr):
            s[...] = b_ref[...].astype(BF16)

    accs = [jnp.dot(a[...].astype(BF16), s[...], preferred_element_type=F32)
            for a, s in zip(a_refs, b_scr)]
    o_ref[...] = epilogue(accs, [e[...] for e in e_refs]).astype(o_ref.dtype)


def _mm(products, extras, layer, epilogue, n_out, out_dtype, tm, tn, name):
    m = products[0][0].shape[1]
    tm = min(tm, m)
    tn = min(tn, n_out)
    assert m % tm == 0 and n_out % tn == 0
    in_specs, args, scratch = [], [], []
    for a, _, _ in products:
        la = layer if a.shape[0] > 1 else 0
        in_specs.append(pl.BlockSpec((None, tm, a.shape[2]), lambda j, i, la=la: (la, i, 0)))
        args.append(a)
    for _, b, off in products:
        in_specs.append(pl.BlockSpec((None, b.shape[1], tn), lambda j, i, off=off: (layer, 0, j + off)))
        args.append(b)
        scratch.append(pltpu.VMEM((b.shape[1], tn), BF16))
    for e, off in extras:
        in_specs.append(pl.BlockSpec((None, 1, tn), lambda j, i, off=off: (layer, 0, j + off)))
        args.append(e)
    kern = functools.partial(_mm_kernel, n_prod=len(products), n_extra=len(extras), epilogue=epilogue)
    return pl.pallas_call(
        kern,
        out_shape=jax.ShapeDtypeStruct((m, n_out), out_dtype),
        grid=(n_out // tn, m // tm),
        in_specs=in_specs,
        out_specs=pl.BlockSpec((tm, tn), lambda j, i: (i, j)),
        scratch_shapes=scratch,
        compiler_params=_params(("arbitrary", "arbitrary")),
        name=name,
    )(*args)


def _ep_identity(accs, extras):
    return accs[0]


def _ep_merge(accs, extras):
    o_r, o_s, z_r, z_s = accs
    b_r, b_s = extras
    return jax.nn.sigmoid(z_r + b_r) * o_r + jax.nn.sigmoid(z_s + b_s) * o_s


def _ep_ple(accs, extras):
    return jax.nn.sigmoid(accs[0]) * accs[1]


def _ln_body(y, g, b):
    mu = jnp.mean(y, axis=-1, keepdims=True)
    yc = y - mu
    var = jnp.mean(yc * yc, axis=-1, keepdims=True)
    return yc * lax.rsqrt(var + LN_EPS) * g + b


def _ln_kernel(*refs, n_add):
    x_ref = refs[0]
    add_refs = refs[1:1 + n_add]
    g_ref, b_ref, o_ref, ob_ref = refs[1 + n_add:]
    y = DEEPNORM_ALPHA * x_ref[...]
    for a in add_refs:
        y = y + a[...].astype(F32)
    out = _ln_body(y, g_ref[...], b_ref[...])
    o_ref[...] = out
    ob_ref[...] = out.astype(BF16)


def _ln(x, adds, g, b, name):
    n, d = x.shape
    tm = min(LN_ROWS, n)
    tile = pl.BlockSpec((tm, d), lambda i: (i, 0))
    row = pl.BlockSpec((1, d), lambda i: (0, 0))
    return pl.pallas_call(
        functools.partial(_ln_kernel, n_add=len(adds)),
        out_shape=(jax.ShapeDtypeStruct((n, d), F32), jax.ShapeDtypeStruct((n, d), BF16)),
        grid=(n // tm,),
        in_specs=[tile] * (1 + len(adds)) + [row, row],
        out_specs=(tile, tile),
        compiler_params=_params(("arbitrary",)),
        name=name,
    )(x, *adds, g.reshape(1, d), b.reshape(1, d))


def _dot_nt(a, b):
    return lax.dot_general(a, b, (((1,), (1,)), ((), ())), preferred_element_type=F32)


def _dot_tn(a, b):
    return lax.dot_general(a, b, (((0,), (0,)), ((), ())), preferred_element_type=F32)


def _ret_kernel(q_ref, k_ref, v_ref, g_ref, cos_ref, sin_ref, decay_ref, xi_ref, zeta_ref,
                o_ref, r_scr, *, chunk_decay):
    d = HEAD_DIM

    @pl.when(pl.program_id(1) == 0)
    def _():
        r_scr[...] = jnp.zeros_like(r_scr)

    cos = cos_ref[...]
    sin = sin_ref[...]

    def rot(t):
        return t * cos + pltpu.roll(t, d // 2, 1) * sin

    for h in range(RET_HEADS):
        sl = slice(h * d, (h + 1) * d)
        qr = rot(q_ref[:, sl].astype(F32))
        kr = rot(k_ref[:, sl].astype(F32)) * (d ** -0.5)
        v = v_ref[:, sl]
        qb = qr.astype(BF16)
        inner = _dot_nt(qb, kr.astype(BF16)) * decay_ref[h]
        r = r_scr[h]
        o = (jnp.dot(inner.astype(BF16), v, preferred_element_type=F32)
             + jnp.dot(qb, r.astype(BF16), preferred_element_type=F32) * xi_ref[h])
        kz = (kr * zeta_ref[h]).astype(BF16)
        r_scr[h] = r * chunk_decay[h] + _dot_tn(kz, v)
        mu = jnp.mean(o, axis=-1, keepdims=True)
        oc = o - mu
        var = jnp.mean(oc * oc, axis=-1, keepdims=True)
        g = g_ref[:, sl].astype(F32)
        o_ref[:, sl] = (oc * lax.rsqrt(var + LN_EPS) * (g * jax.nn.sigmoid(g))).astype(o_ref.dtype)


def _retention(h, batch, seq):
    n = h.shape[0]
    c = RET_CHUNK
    d = HEAD_DIM
    w = RET_HEADS * d
    nc = seq // c
    half = d // 2
    pos = jnp.arange(seq, dtype=F32)
    inv = ROPE_BASE ** (-jnp.arange(half, dtype=F32) / half)
    ang = pos[:, None] * inv[None, :]
    cos = jnp.concatenate([jnp.cos(ang), jnp.cos(ang)], axis=1)
    sin = jnp.concatenate([-jnp.sin(ang), jnp.sin(ang)], axis=1)
    gamma = 1.0 - jnp.exp2(-5.0 - jnp.arange(RET_HEADS, dtype=F32))
    lg = jnp.log(gamma)
    idx = jnp.arange(c, dtype=F32)
    diff = idx[:, None] - idx[None, :]
    causal = diff >= 0
    decay = jnp.where(causal[None], jnp.exp(jnp.where(causal, diff, 0.0)[None] * lg[:, None, None]), 0.0)
    xi = jnp.broadcast_to(jnp.exp((idx + 1.0)[None, :] * lg[:, None])[:, :, None], (RET_HEADS, c, d))
    zeta = jnp.broadcast_to(jnp.exp((c - 1.0 - idx)[None, :] * lg[:, None])[:, :, None], (RET_HEADS, c, d))
    chunk_decay = tuple(math.exp(c * math.log(1.0 - 2.0 ** (-5.0 - hh))) for hh in range(RET_HEADS))

    def col(j):
        return pl.BlockSpec((c, w), lambda b, t, j=j: (b * nc + t, j))

    tab = pl.BlockSpec((c, d), lambda b, t: (t, 0))
    hconst = pl.BlockSpec((RET_HEADS, c, d), lambda b, t: (0, 0, 0))
    return pl.pallas_call(
        functools.partial(_ret_kernel, chunk_decay=chunk_decay),
        out_shape=jax.ShapeDtypeStruct((n, w), BF16),
        grid=(batch, nc),
        in_specs=[col(0), col(1), col(2), col(3), tab, tab,
                  pl.BlockSpec((RET_HEADS, c, c), lambda b, t: (0, 0, 0)), hconst, hconst],
        out_specs=pl.BlockSpec((c, w), lambda b, t: (b * nc + t, 0)),
        scratch_shapes=[pltpu.VMEM((RET_HEADS, d, d), F32)],
        compiler_params=_params(("arbitrary", "arbitrary")),
        name="retention",
    )(h, h, h, h, cos, sin, decay, xi, zeta)


def _sb_kernel(q_ref, k_ref, v_ref, u_ref, o_ref):
    blk = SB_BLOCK
    d = HEAD_DIM
    scale = d ** -0.5
    qb = pl.program_id(2)
    q = q_ref[...]
    u = u_ref[...]

    def block(kb, carry, acc, mask):
        start = pl.multiple_of(kb * blk, blk)
        z = _dot_nt(q, k_ref[pl.ds(start, blk), :]) * scale
        sp = jnp.log1p(jnp.exp(-jnp.abs(z)))
        log_beta = jnp.minimum(z, 0.0) - sp
        log_fail = -jnp.maximum(z, 0.0) - sp
        if mask is not None:
            log_fail = jnp.where(mask, log_fail, 0.0)
        hi = log_fail.astype(BF16)
        lo = (log_fail - hi.astype(F32)).astype(BF16)
        later = (carry + jnp.dot(hi, u, preferred_element_type=F32)
                 + jnp.dot(lo, u, preferred_element_type=F32))
        w = jnp.exp(log_beta + later)
        if mask is not None:
            w = jnp.where(mask, w, 0.0)
        acc = acc + jnp.dot(w.astype(BF16), v_ref[pl.ds(start, blk), :], preferred_element_type=F32)
        carry = carry + jnp.sum(log_fail, axis=1, keepdims=True)
        return carry, acc

    row = lax.broadcasted_iota(jnp.int32, (blk, blk), 0)
    col = lax.broadcasted_iota(jnp.int32, (blk, blk), 1)
    carry, acc = block(qb, jnp.zeros((blk, 1), F32), jnp.zeros((blk, d), F32), col < row)
    carry, acc = lax.fori_loop(0, qb, lambda i, ca: block(qb - 1 - i, ca[0], ca[1], None), (carry, acc))
    o_ref[...] = acc.astype(o_ref.dtype)


def _stick_breaking(h, batch, seq, col0):
    n = h.shape[0]
    blk = SB_BLOCK
    d = HEAD_DIM
    nq = seq // blk
    idx = jnp.arange(blk)
    u = (idx[:, None] > idx[None, :]).astype(BF16)
    return pl.pallas_call(
        _sb_kernel,
        out_shape=jax.ShapeDtypeStruct((n, SB_HEADS * d), BF16),
        grid=(batch, SB_HEADS, nq),
        in_specs=[pl.BlockSpec((blk, d), lambda b, hh, t: (b * nq + t, col0 + hh)),
                  pl.BlockSpec((seq, d), lambda b, hh, t: (b, col0 + SB_HEADS + hh)),
                  pl.BlockSpec((seq, d), lambda b, hh, t: (b, col0 + 2 * SB_HEADS + hh)),
                  pl.BlockSpec((blk, blk), lambda b, hh, t: (0, 0))],
        out_specs=pl.BlockSpec((blk, d), lambda b, hh, t: (b * nq + t, hh)),
        compiler_params=_params(("arbitrary", "arbitrary", "arbitrary")),
        name="stick_breaking",
    )(h, h, h, u)


def _ffn_kernel(be_ref, ns_ref, x_ref, w1_ref, w3_ref, w2_ref, o_ref, acc, w1b, w3b, w2b):
    i = pl.program_id(0)
    j = pl.program_id(1)
    nsub = ns_ref[i]

    @pl.when(j == 0)
    def _():
        acc[...] = jnp.zeros_like(acc)

    @pl.when(nsub > 0)
    def _():
        w1b[...] = w1_ref[...].astype(BF16)
        w3b[...] = w3_ref[...].astype(BF16)
        w2b[...] = w2_ref[...].astype(BF16)

        def sub(s, carry):
            rows = pl.ds(pl.multiple_of(s * FFN_SUB, FFN_SUB), FFN_SUB)
            x = x_ref[rows, :]
            a = jnp.dot(x, w1b[...], preferred_element_type=F32)
            b = jnp.dot(x, w3b[...], preferred_element_type=F32)
            hmid = (a * jax.nn.sigmoid(a) * b).astype(BF16)
            acc[rows, :] += jnp.dot(hmid, w2b[...], preferred_element_type=F32)
            return carry

        lax.fori_loop(0, nsub, sub, 0)

    @pl.when(j == pl.num_programs(1) - 1)
    def _():
        o_ref[...] = acc[...].astype(o_ref.dtype)


def _ffn(x, w1, w3, w2, block_e, block_nsub, out_dtype, name):
    p, d = x.shape
    f = w1.shape[2]
    tm = min(FFN_ROWS, p)
    tf = min(FFN_TF, f)
    assert p % tm == 0 and f % tf == 0 and tm % FFN_SUB == 0
    nj = f // tf

    def jj(i, j, ns):
        return jnp.where(ns[i] > 0, j, nj - 1)

    return pl.pallas_call(
        _ffn_kernel,
        out_shape=jax.ShapeDtypeStruct((p, d), out_dtype),
        grid_spec=pltpu.PrefetchScalarGridSpec(
            num_scalar_prefetch=2,
            grid=(p // tm, nj),
            in_specs=[pl.BlockSpec((tm, d), lambda i, j, be, ns: (i, 0)),
                      pl.BlockSpec((None, d, tf), lambda i, j, be, ns: (be[i], 0, jj(i, j, ns))),
                      pl.BlockSpec((None, d, tf), lambda i, j, be, ns: (be[i], 0, jj(i, j, ns))),
                      pl.BlockSpec((None, tf, d), lambda i, j, be, ns: (be[i], jj(i, j, ns), 0))],
            out_specs=pl.BlockSpec((tm, d), lambda i, j, be, ns: (i, 0)),
            scratch_shapes=[pltpu.VMEM((tm, d), F32), pltpu.VMEM((d, tf), BF16),
                            pltpu.VMEM((d, tf), BF16), pltpu.VMEM((tf, d), BF16)]),
        compiler_params=_params(("arbitrary", "arbitrary")),
        name=name,
    )(block_e, block_nsub, x, w1, w3, w2)


def _route_kernel(x_ref, w_ref, idx_ref, gate_ref, *, n_experts):
    logits = jnp.dot(x_ref[...], w_ref[...], preferred_element_type=F32, precision=lax.Precision.HIGHEST)
    lane_i = lax.broadcasted_iota(jnp.int32, logits.shape, 1)
    lane = lane_i.astype(F32)
    neg = jnp.float32(-jnp.inf)
    logits = jnp.where(lane_i < n_experts, logits, neg)
    m1 = jnp.max(logits, axis=-1, keepdims=True)
    i1 = jnp.min(jnp.where(logits == m1, lane, float(LANES)), axis=-1, keepdims=True)
    rest = jnp.where(lane == i1, neg, logits)
    m2 = jnp.max(rest, axis=-1, keepdims=True)
    i2 = jnp.min(jnp.where(rest == m2, lane, float(LANES)), axis=-1, keepdims=True)
    e2 = jnp.exp(m2 - m1)
    den = 1.0 + e2
    idx_ref[...] = jnp.where(lane_i == 0, i1, jnp.where(lane_i == 1, i2, 0.0)).astype(jnp.int32)
    gate_ref[...] = jnp.where(lane_i == 0, 1.0 / den, jnp.where(lane_i == 1, e2 / den, 0.0))


def _route(x, w_router):
    n, d = x.shape
    e = w_router.shape[1]
    tm = min(ROUTE_ROWS, n)
    wpad = jnp.zeros((d, LANES), F32).at[:, :e].set(w_router)
    idx, gate = pl.pallas_call(
        functools.partial(_route_kernel, n_experts=e),
        out_shape=(jax.ShapeDtypeStruct((n, LANES), jnp.int32), jax.ShapeDtypeStruct((n, LANES), F32)),
        grid=(n // tm,),
        in_specs=[pl.BlockSpec((tm, d), lambda i: (i, 0)), pl.BlockSpec((d, LANES), lambda i: (0, 0))],
        out_specs=(pl.BlockSpec((tm, LANES), lambda i: (i, 0)), pl.BlockSpec((tm, LANES), lambda i: (i, 0))),
        compiler_params=_params(("arbitrary",)),
        name="route_top2",
    )(x, wpad)
    return idx[:, :TOP_K], gate[:, :TOP_K]


def _gather_kernel(tok_ref, ns_ref, x_hbm, o_ref, buf, sem):
    s = pl.program_id(0)
    per = FFN_ROWS // GATHER_ROWS
    nonempty = (s % per) < ns_ref[s // per]

    @pl.when(nonempty)
    def _():
        base = s * GATHER_ROWS

        def issue(r, carry):
            t = tok_ref[base + r]
            pltpu.make_async_copy(x_hbm.at[pl.ds(t, 1), :], buf.at[pl.ds(r, 1), :], sem).start()
            return carry

        lax.fori_loop(0, GATHER_ROWS, issue, 0)

        def drain(r, carry):
            pltpu.make_async_copy(x_hbm.at[pl.ds(0, 1), :], buf.at[pl.ds(r, 1), :], sem).wait()
            return carry

        lax.fori_loop(0, GATHER_ROWS, drain, 0)
        o_ref[...] = buf[...].astype(o_ref.dtype)

    @pl.when(jnp.logical_not(nonempty))
    def _():
        o_ref[...] = jnp.zeros_like(o_ref)


def _gather_rows(x, slot_tok, block_nsub, p):
    n, d = x.shape
    return pl.pallas_call(
        _gather_kernel,
        out_shape=jax.ShapeDtypeStruct((p, d), BF16),
        grid_spec=pltpu.PrefetchScalarGridSpec(
            num_scalar_prefetch=2,
            grid=(p // GATHER_ROWS,),
            in_specs=[pl.BlockSpec(memory_space=pl.ANY)],
            out_specs=pl.BlockSpec((GATHER_ROWS, d), lambda s, tok, ns: (s, 0)),
            scratch_shapes=[pltpu.VMEM((GATHER_ROWS, d), F32), pltpu.SemaphoreType.DMA(())]),
        compiler_params=_params(("arbitrary",)),
        name="moe_gather",
    )(slot_tok, block_nsub, x)


def _combine_kernel(p0_ref, p1_ref, y_hbm, x_ref, ple_ref, g0_ref, g1_ref, lg_ref, lb_ref,
                    o_ref, ob_ref, buf, sem):
    i = pl.program_id(0)
    tm = x_ref.shape[0]
    base = i * tm

    def issue(r, carry):
        pltpu.make_async_copy(y_hbm.at[pl.ds(p0_ref[base + r], 1), :], buf.at[0, pl.ds(r, 1), :], sem).start()
        pltpu.make_async_copy(y_hbm.at[pl.ds(p1_ref[base + r], 1), :], buf.at[1, pl.ds(r, 1), :], sem).start()
        return carry

    lax.fori_loop(0, tm, issue, 0)

    def drain(r, carry):
        pltpu.make_async_copy(y_hbm.at[pl.ds(0, 1), :], buf.at[0, pl.ds(r, 1), :], sem).wait()
        pltpu.make_async_copy(y_hbm.at[pl.ds(0, 1), :], buf.at[1, pl.ds(r, 1), :], sem).wait()
        return carry

    lax.fori_loop(0, tm, drain, 0)
    f = g0_ref[...] * buf[0] + g1_ref[...] * buf[1]
    out = _ln_body(DEEPNORM_ALPHA * x_ref[...] + f + ple_ref[...].astype(F32), lg_ref[...], lb_ref[...])
    o_ref[...] = out
    ob_ref[...] = out.astype(BF16)


def _combine_ln(y, pos0, pos1, g0, g1, x, ple, ln_g, ln_b):
    n, d = x.shape
    tm = min(LN_ROWS, n)
    tile = lambda: pl.BlockSpec((tm, d), lambda i, a, b: (i, 0))
    colv = lambda: pl.BlockSpec((tm, 1), lambda i, a, b: (i, 0))
    row = lambda: pl.BlockSpec((1, d), lambda i, a, b: (0, 0))
    return pl.pallas_call(
        _combine_kernel,
        out_shape=(jax.ShapeDtypeStruct((n, d), F32), jax.ShapeDtypeStruct((n, d), BF16)),
        grid_spec=pltpu.PrefetchScalarGridSpec(
            num_scalar_prefetch=2,
            grid=(n // tm,),
            in_specs=[pl.BlockSpec(memory_space=pl.ANY), tile(), tile(), colv(), colv(), row(), row()],
            out_specs=(tile(), tile()),
            scratch_shapes=[pltpu.VMEM((2, tm, d), F32), pltpu.SemaphoreType.DMA(())]),
        compiler_params=_params(("arbitrary",)),
        name="moe_combine_ln",
    )(pos0, pos1, y, x, ple, g0.reshape(n, 1), g1.reshape(n, 1), ln_g.reshape(1, d), ln_b.reshape(1, d))


def _moe_plan(top_i, n_experts):
    n = top_i.shape[0]
    nk = n * TOP_K
    nblk = nk // FFN_ROWS + n_experts
    e_flat = top_i.reshape(nk)
    onehot = (e_flat[:, None] == jnp.arange(n_experts, dtype=jnp.int32)[None, :]).astype(jnp.int32)
    csum = jnp.cumsum(onehot, axis=0)
    rank = jnp.sum((csum - onehot) * onehot, axis=1)
    counts = csum[-1]
    padded = (counts + FFN_ROWS - 1) // FFN_ROWS * FFN_ROWS
    pad_end = jnp.cumsum(padded)
    pad_off = pad_end - padded
    pos = pad_off[e_flat] + rank
    blk_start = jnp.arange(nblk, dtype=jnp.int32) * FFN_ROWS
    block_e = jnp.clip(jnp.searchsorted(pad_end, blk_start, side="right"), 0, n_experts - 1).astype(jnp.int32)
    rows = jnp.clip(counts[block_e] - (blk_start - pad_off[block_e]), 0, FFN_ROWS)
    rows = jnp.where(blk_start < pad_end[-1], rows, 0)
    block_nsub = ((rows + FFN_SUB - 1) // FFN_SUB).astype(jnp.int32)
    tok_flat = jnp.arange(nk, dtype=jnp.int32) // TOP_K
    slot_tok = jnp.zeros((nblk * FFN_ROWS,), jnp.int32).at[pos].set(tok_flat)
    return pos.reshape(n, TOP_K).astype(jnp.int32), slot_tok, block_e, block_nsub, nblk * FFN_ROWS


def _token_mixer(xb, layer, w_in, w_br_ret, w_br_sb, w_gate, b_gate, w_o, batch, seq):
    d = xb.shape[1]
    in_width = w_in.shape[2]
    h = _mm([(xb[None], w_in, 0)], [], layer, _ep_identity, in_width, BF16, 1024, 512, "in_proj")
    o_ret = _retention(h, batch, seq)
    o_sb = _stick_breaking(h, batch, seq, 4 * RET_HEADS)
    tn = 256
    bg = b_gate.reshape(b_gate.shape[0], 1, 2 * d)
    merged = _mm([(o_ret[None], w_br_ret, 0), (o_sb[None], w_br_sb, 0), (xb[None], w_gate, 0),
                  (xb[None], w_gate, d // tn)],
                 [(bg, 0), (bg, d // tn)], layer, _ep_merge, d, BF16, 1024, tn, "branch_merge")
    return _mm([(merged[None], w_o, 0)], [], layer, _ep_identity, d, F32, 1024, 512, "out_proj")


def kernel(x, p, w_in, w_br_ret, w_br_sb, w_gate, b_gate, w_o, ln1_g, ln1_b,
           ffn_w1, ffn_w3, ffn_w2, moe_router, moe_w1, moe_w3, moe_w2,
           ple_w, ple_gate_w, ln2_g, ln2_b):
    batch, seq, d = x.shape
    n = batch * seq
    depth = w_in.shape[0]
    n_experts = moe_router.shape[2]
    xf = x.reshape(n, d)
    xb = xf.astype(BF16)
    pf = p.reshape(depth, n, p.shape[3])
    ew1 = moe_w1.reshape((-1,) + moe_w1.shape[2:])
    ew3 = moe_w3.reshape((-1,) + moe_w3.shape[2:])
    ew2 = moe_w2.reshape((-1,) + moe_w2.shape[2:])
    for i in range(depth):
        mix = _token_mixer(xb, i, w_in, w_br_ret, w_br_sb, w_gate, b_gate, w_o, batch, seq)
        xf, xb = _ln(xf, [mix], ln1_g[i], ln1_b[i], "ln_mixer")
        ple = _mm([(xb[None], ple_gate_w, 0), (pf, ple_w, 0)], [], i, _ep_ple, d, BF16, 1024, 512, "ple")
        if i % 2 == 0:
            nblk = n // FFN_ROWS
            f = _ffn(xb, ffn_w1, ffn_w3, ffn_w2, jnp.full((nblk,), i // 2, jnp.int32),
                     jnp.full((nblk,), FFN_ROWS // FFN_SUB, jnp.int32), BF16, "dense_swiglu")
            xf, xb = _ln(xf, [f, ple], ln2_g[i], ln2_b[i], "ln_ffn")
        else:
            top_i, gate = _route(xf, moe_router[i // 2])
            pos, slot_tok, block_e, block_nsub, slots = _moe_plan(top_i, n_experts)
            xs = _gather_rows(xf, slot_tok, block_nsub, slots)
            y = _ffn(xs, ew1, ew3, ew2, block_e + (i // 2) * n_experts, block_nsub, F32, "expert_swiglu")
            xf, xb = _combine_ln(y, pos[:, 0], pos[:, 1], gate[:, 0], gate[:, 1], xf, ple, ln2_g[i], ln2_b[i])
    return xf.reshape(batch, seq, d)
```

```python
import functools
import math

import jax
import jax.numpy as jnp
from jax import lax
from jax.experimental import pallas as pl
from jax.experimental.pallas import tpu as pltpu

F32 = jnp.float32
BF16 = jnp.bfloat16

RET_HEADS = 8
SB_HEADS = 8
HEAD_DIM = 128
RET_CHUNK = 128
ROPE_BASE = 10000.0
TOP_K = 2
DEPTH = 2
DEEPNORM_ALPHA = (2 * DEPTH) ** 0.25
LN_EPS = 1e-5

V7X_VMEM_BYTES = 64 * 1024 * 1024
VMEM_LIMIT = V7X_VMEM_BYTES - 8 * 1024 * 1024
LANES = 128

SB_BLOCK = 128
FFN_ROWS = 1024
FFN_SUB = 256
FFN_TF = 256
LN_ROWS = 256
ROUTE_ROWS = 512
GATHER_ROWS = FFN_SUB


def _params(sem):
    return pltpu.CompilerParams(dimension_semantics=sem, vmem_limit_bytes=VMEM_LIMIT)


def _mm_kernel(*refs, n_prod, n_extra, epilogue):
    a_refs = refs[:n_prod]
    b_refs = refs[n_prod:2 * n_prod]
    e_refs = refs[2 * n_prod:2 * n_prod + n_extra]
    o_ref = refs[2 * n_prod + n_extra]
    b_scr = refs[2 * n_prod + n_extra + 1:]

    @pl.when(pl.program_id(1) == 0)
    def _():
        for b_ref, s in zip(b_refs, b_scr):
            s[...] = b_ref[...].astype(BF16)

    accs = [jnp.dot(a[...].astype(BF16), s[...], preferred_element_type=F32)
            for a, s in zip(a_refs, b_scr)]
    o_ref[...] = epilogue(accs, [e[...] for e in e_refs]).astype(o_ref.dtype)


def _mm(products, extras, layer, epilogue, n_out, out_dtype, tm, tn, name):
    m = products[0][0].shape[1]
    tm = min(tm, m)
    tn = min(tn, n_out)
    assert m % tm == 0 and n_out % tn == 0
    in_specs, args, scratch = [], [], []
    for a, _, _ in products:
        la = layer if a.shape[0] > 1 else 0
        in_specs.append(pl.BlockSpec((None, tm, a.shape[2]), lambda j, i, la=la: (la, i, 0)))
        args.append(a)
    for _, b, off in products:
        in_specs.append(pl.BlockSpec((None, b.shape[1], tn), lambda j, i, off=off: (layer, 0, j + off)))
        args.append(b)
        scratch.append(pltpu.VMEM((b.shape[1], tn), BF16))
    for e, off in extras:
        in_specs.append(pl.BlockSpec((None, 1, tn), lambda j, i, off=off: (layer, 0, j + off)))
        args.append(e)
    kern = functools.partial(_mm_kernel, n_prod=len(products), n_extra=len(extras), epilogue=epilogue)
    return pl.pallas_call(
        kern,
        out_shape=jax.ShapeDtypeStruct((m, n_out), out_dtype),
        grid=(n_out // tn, m // tm),
        in_specs=in_specs,
        out_specs=pl.BlockSpec((tm, tn), lambda j, i: (i, j)),
        scratch_shapes=scratch,
        compiler_params=_params(("arbitrary", "arbitrary")),
        name=name,
    )(*args)


def _ep_identity(accs, extras):
    return accs[0]


def _ep_merge(accs, extras):
    o_r, o_s, z_r, z_s = accs
    b_r, b_s = extras
    return jax.nn.sigmoid(z_r + b_r) * o_r + jax.nn.sigmoid(z_s + b_s) * o_s


def _ep_ple(accs, extras):
    return jax.nn.sigmoid(accs[0]) * accs[1]


def _ln_body(y, g, b):
    mu = jnp.mean(y, axis=-1, keepdims=True)
    yc = y - mu
    var = jnp.mean(yc * yc, axis=-1, keepdims=True)
    return yc * lax.rsqrt(var + LN_EPS) * g + b


def _ln_kernel(*refs, n_add):
    x_ref = refs[0]
    add_refs = refs[1:1 + n_add]
    g_ref, b_ref, o_ref, ob_ref = refs[1 + n_add:]
    y = DEEPNORM_ALPHA * x_ref[...]
    for a in add_refs:
        y = y + a[...].astype(F32)
    out = _ln_body(y, g_ref[...], b_ref[...])
    o_ref[...] = out
    ob_ref[...] = out.astype(BF16)


def _ln(x, adds, g, b, name):
    n, d = x.shape
    tm = min(LN_ROWS, n)
    tile = pl.BlockSpec((tm, d), lambda i: (i, 0))
    row = pl.BlockSpec((1, d), lambda i: (0, 0))
    return pl.pallas_call(
        functools.partial(_ln_kernel, n_add=len(adds)),
        out_shape=(jax.ShapeDtypeStruct((n, d), F32), jax.ShapeDtypeStruct((n, d), BF16)),
        grid=(n // tm,),
        in_specs=[tile] * (1 + len(adds)) + [row, row],
        out_specs=(tile, tile),
        compiler_params=_params(("arbitrary",)),
        name=name,
    )(x, *adds, g.reshape(1, d), b.reshape(1, d))


def _dot_nt(a, b):
    return lax.dot_general(a, b, (((1,), (1,)), ((), ())), preferred_element_type=F32)


def _dot_tn(a, b):
    return lax.dot_general(a, b, (((0,), (0,)), ((), ())), preferred_element_type=F32)


def _ret_kernel(q_ref, k_ref, v_ref, g_ref, cos_ref, sin_ref, decay_ref, xi_ref, zeta_ref,
                o_ref, r_scr, *, chunk_decay):
    d = HEAD_DIM

    @pl.when(pl.program_id(1) == 0)
    def _():
        r_scr[...] = jnp.zeros_like(r_scr)

    cos = cos_ref[...]
    sin = sin_ref[...]

    def rot(t):
        return t * cos + pltpu.roll(t, d // 2, 1) * sin

    for h in range(RET_HEADS):
        sl = slice(h * d, (h + 1) * d)
        qr = rot(q_ref[:, sl].astype(F32))
        kr = rot(k_ref[:, sl].astype(F32)) * (d ** -0.5)
        v = v_ref[:, sl]
        qb = qr.astype(BF16)
        inner = _dot_nt(qb, kr.astype(BF16)) * decay_ref[h]
        r = r_scr[h]
        o = (jnp.dot(inner.astype(BF16), v, preferred_element_type=F32)
             + jnp.dot(qb, r.astype(BF16), preferred_element_type=F32) * xi_ref[h])
        kz = (kr * zeta_ref[h]).astype(BF16)
        r_scr[h] = r * chunk_decay[h] + _dot_tn(kz, v)
        mu = jnp.mean(o, axis=-1, keepdims=True)
        oc = o - mu
        var = jnp.mean(oc * oc, axis=-1, keepdims=True)
        g = g_ref[:, sl].astype(F32)
        o_ref[:, sl] = (oc * lax.rsqrt(var + LN_EPS) * (g * jax.nn.sigmoid(g))).astype(o_ref.dtype)


def _retention(h, batch, seq):
    n = h.shape[0]
    c = RET_CHUNK
    d = HEAD_DIM
    w = RET_HEADS * d
    nc = seq // c
    half = d // 2
    pos = jnp.arange(seq, dtype=F32)
    inv = ROPE_BASE ** (-jnp.arange(half, dtype=F32) / half)
    ang = pos[:, None] * inv[None, :]
    cos = jnp.concatenate([jnp.cos(ang), jnp.cos(ang)], axis=1)
    sin = jnp.concatenate([-jnp.sin(ang), jnp.sin(ang)], axis=1)
    gamma = 1.0 - jnp.exp2(-5.0 - jnp.arange(RET_HEADS, dtype=F32))
    lg = jnp.log(gamma)
    idx = jnp.arange(c, dtype=F32)
    diff = idx[:, None] - idx[None, :]
    causal = diff >= 0
    decay = jnp.where(causal[None], jnp.exp(jnp.where(causal, diff, 0.0)[None] * lg[:, None, None]), 0.0)
    xi = jnp.broadcast_to(jnp.exp((idx + 1.0)[None, :] * lg[:, None])[:, :, None], (RET_HEADS, c, d))
    zeta = jnp.broadcast_to(jnp.exp((c - 1.0 - idx)[None, :] * lg[:, None])[:, :, None], (RET_HEADS, c, d))
    chunk_decay = tuple(math.exp(c * math.log(1.0 - 2.0 ** (-5.0 - hh))) for hh in range(RET_HEADS))

    def col(j):
        return pl.BlockSpec((c, w), lambda b, t, j=j: (b * nc + t, j))

    tab = pl.BlockSpec((c, d), lambda b, t: (t, 0))
    hconst = pl.BlockSpec((RET_HEADS, c, d), lambda b, t: (0, 0, 0))
    return pl.pallas_call(
        functools.partial(_ret_kernel, chunk_decay=chunk_decay),
        out_shape=jax.ShapeDtypeStruct((n, w), BF16),
        grid=(batch, nc),
        in_specs=[col(0), col(1), col(2), col(3), tab, tab,
                  pl.BlockSpec((RET_HEADS, c, c), lambda b, t: (0, 0, 0)), hconst, hconst],
        out_specs=pl.BlockSpec((c, w), lambda b, t: (b * nc + t, 0)),
        scratch_shapes=[pltpu.VMEM((RET_HEADS, d, d), F32)],
        compiler_params=_params(("arbitrary", "arbitrary")),
        name="retention",
    )(h, h, h, h, cos, sin, decay, xi, zeta)


def _sb_kernel(q_ref, k_ref, v_ref, u_ref, o_ref, acc, carry):
    blk = SB_BLOCK
    d = HEAD_DIM
    scale = d ** -0.5
    qb = pl.program_id(1)
    acc[...] = jnp.zeros_like(acc)
    carry[...] = jnp.zeros_like(carry)
    row = lax.broadcasted_iota(jnp.int32, (blk, blk), 0)
    col = lax.broadcasted_iota(jnp.int32, (blk, blk), 1)
    dif = col - row

    def body(i, c):
        start = pl.multiple_of((qb - i) * blk, blk)
        mask = dif < i * blk
        heads = [slice(hh * d, (hh + 1) * d) for hh in range(SB_HEADS)]
        zs = [_dot_nt(q_ref[:, sl], k_ref[pl.ds(start, blk), sl]) * scale for sl in heads]
        log_betas, hilos = [], []
        for z in zs:
            sp = jnp.log(1.0 + jnp.exp(-jnp.abs(z)))
            log_beta = jnp.minimum(z, 0.0) - sp
            log_fail = jnp.where(mask, log_beta - z, 0.0)
            hi = log_fail.astype(BF16)
            lo = (log_fail - hi.astype(F32)).astype(BF16)
            log_betas.append(log_beta)
            hilos.append(jnp.concatenate([hi, lo], axis=1))
        s_all = jnp.dot(jnp.concatenate(hilos, axis=0), u_ref[...], preferred_element_type=F32)
        ws = []
        for hh, sl in enumerate(heads):
            s = s_all[hh * blk:(hh + 1) * blk]
            ws.append(jnp.where(mask, jnp.exp(log_betas[hh] + carry[:, sl] + s[:, :blk]), 0.0).astype(BF16))
            carry[:, sl] += s[:, blk:]
        for w, sl in zip(ws, heads):
            acc[:, sl] += jnp.dot(w, v_ref[pl.ds(start, blk), sl], preferred_element_type=F32)
        return c

    lax.fori_loop(0, qb + 1, body, 0)
    o_ref[...] = acc[...].astype(o_ref.dtype)


def _stick_breaking(h, batch, seq, col0):
    n = h.shape[0]
    blk = SB_BLOCK
    w = SB_HEADS * HEAD_DIM
    nq = seq // blk
    idx = jnp.arange(blk)
    tri = (idx[:, None] > idx[None, :]).astype(BF16)
    half = jnp.concatenate([tri, jnp.ones((blk, blk), BF16)], axis=1)
    u = jnp.concatenate([half, half], axis=0)
    return pl.pallas_call(
        _sb_kernel,
        out_shape=jax.ShapeDtypeStruct((n, w), BF16),
        grid=(batch, nq),
        in_specs=[pl.BlockSpec((blk, w), lambda b, t: (b * nq + t, col0)),
                  pl.BlockSpec((seq, w), lambda b, t: (b, col0 + 1)),
                  pl.BlockSpec((seq, w), lambda b, t: (b, col0 + 2)),
                  pl.BlockSpec((2 * blk, 2 * blk), lambda b, t: (0, 0))],
        out_specs=pl.BlockSpec((blk, w), lambda b, t: (b * nq + t, 0)),
        scratch_shapes=[pltpu.VMEM((blk, w), F32), pltpu.VMEM((blk, w), F32)],
        compiler_params=_params(("arbitrary", "arbitrary")),
        name="stick_breaking",
    )(h, h, h, u)


def _ffn_kernel(be_ref, ns_ref, x_ref, w1_ref, w3_ref, w2_ref, o_ref, acc, *, all_rows):
    i = pl.program_id(0)
    j = pl.program_id(1)

    @pl.when(j == 0)
    def _():
        acc[...] = jnp.zeros_like(acc)

    def chunk(start, size):
        rows = pl.ds(start, size)
        x = x_ref[rows, :]
        a = jnp.dot(x, w1_ref[...].astype(BF16), preferred_element_type=F32)
        b = jnp.dot(x, w3_ref[...].astype(BF16), preferred_element_type=F32)
        hmid = (a * jax.nn.sigmoid(a) * b).astype(BF16)
        acc[rows, :] += jnp.dot(hmid, w2_ref[...].astype(BF16), preferred_element_type=F32)

    if all_rows:
        chunk(0, x_ref.shape[0])
    else:
        nsub = ns_ref[i]
        pair = 2 * FFN_SUB

        def pair_body(s, carry):
            chunk(pl.multiple_of(s * pair, pair), pair)
            return carry

        lax.fori_loop(0, nsub // 2, pair_body, 0)

        @pl.when(nsub % 2 == 1)
        def _():
            chunk(pl.multiple_of((nsub - 1) * FFN_SUB, FFN_SUB), FFN_SUB)

    @pl.when(j == pl.num_programs(1) - 1)
    def _():
        o_ref[...] = acc[...].astype(o_ref.dtype)


def _ffn(x, w1, w3, w2, block_e, block_nsub, out_dtype, name, all_rows=False):
    p, d = x.shape
    f = w1.shape[2]
    tm = min(FFN_ROWS, p)
    tf = min(FFN_TF, f)
    assert p % tm == 0 and f % tf == 0 and tm % FFN_SUB == 0
    nj = f // tf

    def jj(i, j, ns):
        return jnp.where(ns[i] > 0, j, nj - 1)

    return pl.pallas_call(
        functools.partial(_ffn_kernel, all_rows=all_rows),
        out_shape=jax.ShapeDtypeStruct((p, d), out_dtype),
        grid_spec=pltpu.PrefetchScalarGridSpec(
            num_scalar_prefetch=2,
            grid=(p // tm, nj),
            in_specs=[pl.BlockSpec((tm, d), lambda i, j, be, ns: (i, 0)),
                      pl.BlockSpec((None, d, tf), lambda i, j, be, ns: (be[i], 0, jj(i, j, ns))),
                      pl.BlockSpec((None, d, tf), lambda i, j, be, ns: (be[i], 0, jj(i, j, ns))),
                      pl.BlockSpec((None, tf, d), lambda i, j, be, ns: (be[i], jj(i, j, ns), 0))],
            out_specs=pl.BlockSpec((tm, d), lambda i, j, be, ns: (i, 0)),
            scratch_shapes=[pltpu.VMEM((tm, d), F32)]),
        compiler_params=_params(("arbitrary", "arbitrary")),
        name=name,
    )(block_e, block_nsub, x, w1, w3, w2)


def _route_kernel(x_ref, w_ref, idx_ref, gate_ref, *, n_experts):
    logits = jnp.dot(x_ref[...], w_ref[...], preferred_element_type=F32, precision=lax.Precision.HIGHEST)
    lane_i = lax.broadcasted_iota(jnp.int32, logits.shape, 1)
    lane = lane_i.astype(F32)
    neg = jnp.float32(-jnp.inf)
    logits = jnp.where(lane_i < n_experts, logits, neg)
    m1 = jnp.max(logits, axis=-1, keepdims=True)
    i1 = jnp.min(jnp.where(logits == m1, lane, float(LANES)), axis=-1, keepdims=True)
    rest = jnp.where(lane == i1, neg, logits)
    m2 = jnp.max(rest, axis=-1, keepdims=True)
    i2 = jnp.min(jnp.where(rest == m2, lane, float(LANES)), axis=-1, keepdims=True)
    e2 = jnp.exp(m2 - m1)
    den = 1.0 + e2
    idx_ref[...] = jnp.where(lane_i == 0, i1, jnp.where(lane_i == 1, i2, 0.0)).astype(jnp.int32)
    gate_ref[...] = jnp.where(lane_i == 0, 1.0 / den, jnp.where(lane_i == 1, e2 / den, 0.0))


def _route(x, w_router):
    n, d = x.shape
    e = w_router.shape[1]
    tm = min(ROUTE_ROWS, n)
    wpad = jnp.zeros((d, LANES), F32).at[:, :e].set(w_router)
    idx, gate = pl.pallas_call(
        functools.partial(_route_kernel, n_experts=e),
        out_shape=(jax.ShapeDtypeStruct((n, LANES), jnp.int32), jax.ShapeDtypeStruct((n, LANES), F32)),
        grid=(n // tm,),
        in_specs=[pl.BlockSpec((tm, d), lambda i: (i, 0)), pl.BlockSpec((d, LANES), lambda i: (0, 0))],
        out_specs=(pl.BlockSpec((tm, LANES), lambda i: (i, 0)), pl.BlockSpec((tm, LANES), lambda i: (i, 0))),
        compiler_params=_params(("arbitrary",)),
        name="route_top2",
    )(x, wpad)
    return idx[:, :TOP_K], gate[:, :TOP_K]


def _gather_kernel(tok_ref, ns_ref, x_hbm, o_ref, buf, sem):
    s = pl.program_id(0)
    per = FFN_ROWS // GATHER_ROWS
    nonempty = (s % per) < ns_ref[s // per]

    @pl.when(nonempty)
    def _():
        base = s * GATHER_ROWS

        def issue(r, carry):
            t = tok_ref[base + r]
            pltpu.make_async_copy(x_hbm.at[pl.ds(t, 1), :], buf.at[pl.ds(r, 1), :], sem).start()
            return carry

        lax.fori_loop(0, GATHER_ROWS, issue, 0, unroll=8)
        pltpu.make_async_copy(x_hbm.at[pl.ds(0, GATHER_ROWS), :], buf, sem).wait()
        o_ref[...] = buf[...].astype(o_ref.dtype)

    @pl.when(jnp.logical_not(nonempty))
    def _():
        o_ref[...] = jnp.zeros_like(o_ref)


def _gather_rows(x, slot_tok, block_nsub, p):
    n, d = x.shape
    return pl.pallas_call(
        _gather_kernel,
        out_shape=jax.ShapeDtypeStruct((p, d), BF16),
        grid_spec=pltpu.PrefetchScalarGridSpec(
            num_scalar_prefetch=2,
            grid=(p // GATHER_ROWS,),
            in_specs=[pl.BlockSpec(memory_space=pl.ANY)],
            out_specs=pl.BlockSpec((GATHER_ROWS, d), lambda s, tok, ns: (s, 0)),
            scratch_shapes=[pltpu.VMEM((GATHER_ROWS, d), F32), pltpu.SemaphoreType.DMA(())]),
        compiler_params=_params(("arbitrary",)),
        name="moe_gather",
    )(slot_tok, block_nsub, x)


def _combine_kernel(p0_ref, p1_ref, y_hbm, x_ref, ple_ref, g0_ref, g1_ref, lg_ref, lb_ref,
                    o_ref, ob_ref, buf, sem):
    i = pl.program_id(0)
    tm = x_ref.shape[0]
    base = i * tm

    def issue(r, carry):
        pltpu.make_async_copy(y_hbm.at[pl.ds(p0_ref[base + r], 1), :], buf.at[0, pl.ds(r, 1), :], sem).start()
        pltpu.make_async_copy(y_hbm.at[pl.ds(p1_ref[base + r], 1), :], buf.at[1, pl.ds(r, 1), :], sem).start()
        return carry

    lax.fori_loop(0, tm, issue, 0, unroll=4)
    pltpu.make_async_copy(y_hbm.at[pl.ds(0, tm), :], buf.at[0], sem).wait()
    pltpu.make_async_copy(y_hbm.at[pl.ds(0, tm), :], buf.at[1], sem).wait()
    f = g0_ref[...] * buf[0] + g1_ref[...] * buf[1]
    out = _ln_body(DEEPNORM_ALPHA * x_ref[...] + f + ple_ref[...].astype(F32), lg_ref[...], lb_ref[...])
    o_ref[...] = out
    ob_ref[...] = out.astype(BF16)


def _combine_ln(y, pos0, pos1, g0, g1, x, ple, ln_g, ln_b):
    n, d = x.shape
    tm = min(LN_ROWS, n)
    tile = lambda: pl.BlockSpec((tm, d), lambda i, a, b: (i, 0))
    colv = lambda: pl.BlockSpec((tm, 1), lambda i, a, b: (i, 0))
    row = lambda: pl.BlockSpec((1, d), lambda i, a, b: (0, 0))
    return pl.pallas_call(
        _combine_kernel,
        out_shape=(jax.ShapeDtypeStruct((n, d), F32), jax.ShapeDtypeStruct((n, d), BF16)),
        grid_spec=pltpu.PrefetchScalarGridSpec(
            num_scalar_prefetch=2,
            grid=(n // tm,),
            in_specs=[pl.BlockSpec(memory_space=pl.ANY), tile(), tile(), colv(), colv(), row(), row()],
            out_specs=(tile(), tile()),
            scratch_shapes=[pltpu.VMEM((2, tm, d), F32), pltpu.SemaphoreType.DMA(())]),
        compiler_params=_params(("arbitrary",)),
        name="moe_combine_ln",
    )(pos0, pos1, y, x, ple, g0.reshape(n, 1), g1.reshape(n, 1), ln_g.reshape(1, d), ln_b.reshape(1, d))


def _moe_plan(top_i, n_experts):
    n = top_i.shape[0]
    nk = n * TOP_K
    nblk = nk // FFN_ROWS + n_experts
    sub_per_blk = FFN_ROWS // FFN_SUB
    e_flat = top_i.reshape(nk)
    onehot = (e_flat[:, None] == jnp.arange(n_experts, dtype=jnp.int32)[None, :]).astype(jnp.int32)
    csum = jnp.cumsum(onehot, axis=0)
    rank = jnp.sum((csum - onehot) * onehot, axis=1)
    counts = csum[-1]
    nsub_e = (counts + FFN_SUB - 1) // FFN_SUB
    nblk_e = (nsub_e + sub_per_blk - 1) // sub_per_blk
    rows_e = jnp.maximum((nsub_e + nblk_e - 1) // jnp.maximum(nblk_e, 1), 1) * FFN_SUB
    blk_end = jnp.cumsum(nblk_e)
    blk_off = blk_end - nblk_e
    rpb = rows_e[e_flat]
    pos = (blk_off[e_flat] + rank // rpb) * FFN_ROWS + rank % rpb
    blk = jnp.arange(nblk, dtype=jnp.int32)
    used = blk < blk_end[-1]
    block_e = jnp.clip(jnp.searchsorted(blk_end, blk, side="right"), 0, n_experts - 1).astype(jnp.int32)
    block_e = jnp.where(used, block_e, block_e[jnp.maximum(blk_end[-1] - 1, 0)])
    rows = jnp.clip(counts[block_e] - (blk - blk_off[block_e]) * rows_e[block_e], 0, rows_e[block_e])
    rows = jnp.where(used, rows, 0)
    block_nsub = ((rows + FFN_SUB - 1) // FFN_SUB).astype(jnp.int32)
    tok_flat = jnp.arange(nk, dtype=jnp.int32) // TOP_K
    slot_tok = jnp.zeros((nblk * FFN_ROWS,), jnp.int32).at[pos].set(tok_flat)
    return pos.reshape(n, TOP_K).astype(jnp.int32), slot_tok, block_e, block_nsub, nblk * FFN_ROWS


def _token_mixer(xb, layer, w_in, w_br_ret, w_br_sb, w_gate, b_gate, w_o, batch, seq):
    d = xb.shape[1]
    in_width = w_in.shape[2]
    h = _mm([(xb[None], w_in, 0)], [], layer, _ep_identity, in_width, BF16, 1024, 512, "in_proj")
    o_ret = _retention(h, batch, seq)
    o_sb = _stick_breaking(h, batch, seq, (4 * RET_HEADS) // SB_HEADS)
    tn = 256
    bg = b_gate.reshape(b_gate.shape[0], 1, 2 * d)
    merged = _mm([(o_ret[None], w_br_ret, 0), (o_sb[None], w_br_sb, 0), (xb[None], w_gate, 0),
                  (xb[None], w_gate, d // tn)],
                 [(bg, 0), (bg, d // tn)], layer, _ep_merge, d, BF16, 1024, tn, "branch_merge")
    return _mm([(merged[None], w_o, 0)], [], layer, _ep_identity, d, F32, 1024, 512, "out_proj")


def kernel(x, p, w_in, w_br_ret, w_br_sb, w_gate, b_gate, w_o, ln1_g, ln1_b,
           ffn_w1, ffn_w3, ffn_w2, moe_router, moe_w1, moe_w3, moe_w2,
           ple_w, ple_gate_w, ln2_g, ln2_b):
    batch, seq, d = x.shape
    n = batch * seq
    depth = w_in.shape[0]
    n_experts = moe_router.shape[2]
    xf = x.reshape(n, d)
    xb = xf.astype(BF16)
    pf = p.reshape(depth, n, p.shape[3])
    ew1 = moe_w1.reshape((-1,) + moe_w1.shape[2:])
    ew3 = moe_w3.reshape((-1,) + moe_w3.shape[2:])
    ew2 = moe_w2.reshape((-1,) + moe_w2.shape[2:])
    for i in range(depth):
        mix = _token_mixer(xb, i, w_in, w_br_ret, w_br_sb, w_gate, b_gate, w_o, batch, seq)
        xf, xb = _ln(xf, [mix], ln1_g[i], ln1_b[i], "ln_mixer")
        ple = _mm([(xb[None], ple_gate_w, 0), (pf, ple_w, 0)], [], i, _ep_ple, d, BF16, 1024, 512, "ple")
        if i % 2 == 0:
            nblk = n // FFN_ROWS
            f = _ffn(xb, ffn_w1, ffn_w3, ffn_w2, jnp.full((nblk,), i // 2, jnp.int32),
                     jnp.full((nblk,), FFN_ROWS // FFN_SUB, jnp.int32), BF16, "dense_swiglu", all_rows=True)
            xf, xb = _ln(xf, [f, ple], ln2_g[i], ln2_b[i], "ln_ffn")
        else:
            top_i, gate = _route(xf, moe_router[i // 2])
            pos, slot_tok, block_e, block_nsub, slots = _moe_plan(top_i, n_experts)
            xs = _gather_rows(xf, slot_tok, block_nsub, slots)
            y = _ffn(xs, ew1, ew3, ew2, block_e + (i // 2) * n_experts, block_nsub, F32, "expert_swiglu")
            xf, xb = _combine_ln(y, pos[:, 0], pos[:, 1], gate[:, 0], gate[:, 1], xf, ple, ln2_g[i], ln2_b[i])
    return xf.reshape(batch, seq, d)
```

```python
import functools
import math

import jax
import jax.numpy as jnp
from jax import lax
from jax.experimental import pallas as pl
from jax.experimental.pallas import tpu as pltpu

F32 = jnp.float32
BF16 = jnp.bfloat16

RET_HEADS = 8
SB_HEADS = 8
HEAD_DIM = 128
RET_CHUNK = 128
ROPE_BASE = 10000.0
TOP_K = 2
DEPTH = 2
DEEPNORM_ALPHA = (2 * DEPTH) ** 0.25
LN_EPS = 1e-5

V7X_VMEM_BYTES = 64 * 1024 * 1024
VMEM_LIMIT = V7X_VMEM_BYTES - 8 * 1024 * 1024
LANES = 128

SB_BLOCK = 128
PROJ_LN_ROWS = 512
FFN_ROWS = 2048
FFN_SUB = 256
FFN_TF = 256
LN_ROWS = 256
ROUTE_ROWS = 512
GATHER_ROWS = FFN_SUB


def _params(sem):
    return pltpu.CompilerParams(dimension_semantics=sem, vmem_limit_bytes=VMEM_LIMIT)


def _mm_kernel(*refs, a_of, n_a, n_extra, epilogue):
    n_prod = len(a_of)
    a_refs = refs[:n_a]
    b_refs = refs[n_a:n_a + n_prod]
    e_refs = refs[n_a + n_prod:n_a + n_prod + n_extra]
    o_ref = refs[n_a + n_prod + n_extra]
    b_scr = refs[n_a + n_prod + n_extra + 1:]

    @pl.when(pl.program_id(1) == 0)
    def _():
        for b_ref, s in zip(b_refs, b_scr):
            s[...] = b_ref[...].astype(BF16)

    a_vals = [a[...].astype(BF16) for a in a_refs]
    accs = [jnp.dot(a_vals[ai], s[...], preferred_element_type=F32) for ai, s in zip(a_of, b_scr)]
    o_ref[...] = epilogue(accs, [e[...] for e in e_refs]).astype(o_ref.dtype)


def _mm(a_ops, products, extras, layer, epilogue, n_out, out_dtype, tm, tn, name):
    m = a_ops[0].shape[1]
    tm = min(tm, m)
    tn = min(tn, n_out)
    assert m % tm == 0 and n_out % tn == 0
    in_specs, args, scratch = [], [], []
    for a in a_ops:
        la = layer if a.shape[0] > 1 else 0
        in_specs.append(pl.BlockSpec((None, tm, a.shape[2]), lambda j, i, la=la: (la, i, 0)))
        args.append(a)
    for _, b, off in products:
        in_specs.append(pl.BlockSpec((None, b.shape[1], tn), lambda j, i, off=off: (layer, 0, j + off)))
        args.append(b)
        scratch.append(pltpu.VMEM((b.shape[1], tn), BF16))
    for e, off in extras:
        in_specs.append(pl.BlockSpec((None, 1, tn), lambda j, i, off=off: (layer, 0, j + off)))
        args.append(e)
    kern = functools.partial(_mm_kernel, a_of=tuple(ai for ai, _, _ in products), n_a=len(a_ops),
                             n_extra=len(extras), epilogue=epilogue)
    return pl.pallas_call(
        kern,
        out_shape=jax.ShapeDtypeStruct((m, n_out), out_dtype),
        grid=(n_out // tn, m // tm),
        in_specs=in_specs,
        out_specs=pl.BlockSpec((tm, tn), lambda j, i: (i, j)),
        scratch_shapes=scratch,
        compiler_params=_params(("arbitrary", "arbitrary")),
        name=name,
    )(*args)


def _ep_identity(accs, extras):
    return accs[0]


def _ep_merge(accs, extras):
    o_r, o_s, z_r, z_s = accs
    b_r, b_s = extras
    return jax.nn.sigmoid(z_r + b_r) * o_r + jax.nn.sigmoid(z_s + b_s) * o_s


def _ep_ple(accs, extras):
    return jax.nn.sigmoid(accs[0]) * accs[1]


def _ln_body(y, g, b):
    mu = jnp.mean(y, axis=-1, keepdims=True)
    yc = y - mu
    var = jnp.mean(yc * yc, axis=-1, keepdims=True)
    return yc * lax.rsqrt(var + LN_EPS) * g + b


def _ln_kernel(*refs, n_add):
    x_ref = refs[0]
    add_refs = refs[1:1 + n_add]
    g_ref, b_ref, o_ref, ob_ref = refs[1 + n_add:]
    y = DEEPNORM_ALPHA * x_ref[...]
    for a in add_refs:
        y = y + a[...].astype(F32)
    out = _ln_body(y, g_ref[...], b_ref[...])
    o_ref[...] = out
    ob_ref[...] = out.astype(BF16)


def _ln(x, adds, g, b, name):
    n, d = x.shape
    tm = min(LN_ROWS, n)
    tile = pl.BlockSpec((tm, d), lambda i: (i, 0))
    row = pl.BlockSpec((1, d), lambda i: (0, 0))
    return pl.pallas_call(
        functools.partial(_ln_kernel, n_add=len(adds)),
        out_shape=(jax.ShapeDtypeStruct((n, d), F32), jax.ShapeDtypeStruct((n, d), BF16)),
        grid=(n // tm,),
        in_specs=[tile] * (1 + len(adds)) + [row, row],
        out_specs=(tile, tile),
        compiler_params=_params(("arbitrary",)),
        name=name,
    )(x, *adds, g.reshape(1, d), b.reshape(1, d))


def _proj_ln_kernel(a_ref, w_ref, x_ref, g_ref, b_ref, o_ref, ob_ref, w_scr):
    @pl.when(pl.program_id(0) == 0)
    def _():
        w_scr[...] = w_ref[...].astype(BF16)

    mix = jnp.dot(a_ref[...], w_scr[...], preferred_element_type=F32)
    out = _ln_body(DEEPNORM_ALPHA * x_ref[...] + mix, g_ref[...], b_ref[...])
    o_ref[...] = out
    ob_ref[...] = out.astype(BF16)


def _proj_ln(a, w, layer, x, g, b, name):
    n, d = x.shape
    k = a.shape[1]
    tm = min(PROJ_LN_ROWS, n)
    tile = pl.BlockSpec((tm, d), lambda i: (i, 0))
    row = pl.BlockSpec((1, d), lambda i: (0, 0))
    return pl.pallas_call(
        _proj_ln_kernel,
        out_shape=(jax.ShapeDtypeStruct((n, d), F32), jax.ShapeDtypeStruct((n, d), BF16)),
        grid=(n // tm,),
        in_specs=[pl.BlockSpec((tm, k), lambda i: (i, 0)),
                  pl.BlockSpec((None, k, d), lambda i: (layer, 0, 0), pipeline_mode=pl.Buffered(1)),
                  tile, row, row],
        out_specs=(tile, tile),
        scratch_shapes=[pltpu.VMEM((k, d), BF16)],
        compiler_params=_params(("arbitrary",)),
        name=name,
    )(a, w, x, g.reshape(1, d), b.reshape(1, d))


def _dot_nt(a, b):
    return lax.dot_general(a, b, (((1,), (1,)), ((), ())), preferred_element_type=F32)


def _dot_tn(a, b):
    return lax.dot_general(a, b, (((0,), (0,)), ((), ())), preferred_element_type=F32)


def _ret_kernel(q_ref, k_ref, v_ref, g_ref, cos_ref, sin_ref, decay_ref, xi_ref, zeta_ref,
                o_ref, r_scr, *, chunk_decay):
    d = HEAD_DIM

    @pl.when(pl.program_id(1) == 0)
    def _():
        r_scr[...] = jnp.zeros_like(r_scr)

    cos = cos_ref[...]
    sin = sin_ref[...]

    def rot(t):
        return t * cos + pltpu.roll(t, d // 2, 1) * sin

    heads = [slice(h * d, (h + 1) * d) for h in range(RET_HEADS)]
    qbs, kbs, kzs = [], [], []
    for h, sl in enumerate(heads):
        kr = rot(k_ref[:, sl].astype(F32)) * (d ** -0.5)
        qbs.append(rot(q_ref[:, sl].astype(F32)).astype(BF16))
        kbs.append(kr.astype(BF16))
        kzs.append((kr * zeta_ref[h]).astype(BF16))
    inners = [_dot_nt(qb, kb) for qb, kb in zip(qbs, kbs)]
    crosses = [jnp.dot(qb, r_scr[h].astype(BF16), preferred_element_type=F32) for h, qb in enumerate(qbs)]
    kvs = [_dot_tn(kz, v_ref[:, sl]) for kz, sl in zip(kzs, heads)]
    for h in range(RET_HEADS):
        r_scr[h] = r_scr[h] * chunk_decay[h] + kvs[h]
    pbs = [(inner * decay_ref[h]).astype(BF16) for h, inner in enumerate(inners)]
    outs = [jnp.dot(pb, v_ref[:, sl], preferred_element_type=F32) for pb, sl in zip(pbs, heads)]
    for h, sl in enumerate(heads):
        o = outs[h] + crosses[h] * xi_ref[h]
        mu = jnp.mean(o, axis=-1, keepdims=True)
        oc = o - mu
        var = jnp.mean(oc * oc, axis=-1, keepdims=True)
        g = g_ref[:, sl].astype(F32)
        o_ref[:, sl] = (oc * lax.rsqrt(var + LN_EPS) * (g * jax.nn.sigmoid(g))).astype(o_ref.dtype)


def _retention(h, batch, seq):
    n = h.shape[0]
    c = RET_CHUNK
    d = HEAD_DIM
    w = RET_HEADS * d
    nc = seq // c
    half = d // 2
    pos = jnp.arange(seq, dtype=F32)
    inv = ROPE_BASE ** (-jnp.arange(half, dtype=F32) / half)
    ang = pos[:, None] * inv[None, :]
    cos = jnp.concatenate([jnp.cos(ang), jnp.cos(ang)], axis=1)
    sin = jnp.concatenate([-jnp.sin(ang), jnp.sin(ang)], axis=1)
    gamma = 1.0 - jnp.exp2(-5.0 - jnp.arange(RET_HEADS, dtype=F32))
    lg = jnp.log(gamma)
    idx = jnp.arange(c, dtype=F32)
    diff = idx[:, None] - idx[None, :]
    causal = diff >= 0
    decay = jnp.where(causal[None], jnp.exp(jnp.where(causal, diff, 0.0)[None] * lg[:, None, None]), 0.0)
    xi = jnp.broadcast_to(jnp.exp((idx + 1.0)[None, :] * lg[:, None])[:, :, None], (RET_HEADS, c, d))
    zeta = jnp.broadcast_to(jnp.exp((c - 1.0 - idx)[None, :] * lg[:, None])[:, :, None], (RET_HEADS, c, d))
    chunk_decay = tuple(math.exp(c * math.log(1.0 - 2.0 ** (-5.0 - hh))) for hh in range(RET_HEADS))

    def col(j):
        return pl.BlockSpec((c, w), lambda b, t, j=j: (b * nc + t, j))

    tab = pl.BlockSpec((c, d), lambda b, t: (t, 0))
    hconst = pl.BlockSpec((RET_HEADS, c, d), lambda b, t: (0, 0, 0))
    return pl.pallas_call(
        functools.partial(_ret_kernel, chunk_decay=chunk_decay),
        out_shape=jax.ShapeDtypeStruct((n, w), BF16),
        grid=(batch, nc),
        in_specs=[col(0), col(1), col(2), col(3), tab, tab,
                  pl.BlockSpec((RET_HEADS, c, c), lambda b, t: (0, 0, 0)), hconst, hconst],
        out_specs=pl.BlockSpec((c, w), lambda b, t: (b * nc + t, 0)),
        scratch_shapes=[pltpu.VMEM((RET_HEADS, d, d), F32)],
        compiler_params=_params(("arbitrary", "arbitrary")),
        name="retention",
    )(h, h, h, h, cos, sin, decay, xi, zeta)


def _sb_kernel(q_ref, k_ref, v_ref, u_ref, o_ref, acc, carry):
    blk = SB_BLOCK
    d = HEAD_DIM
    scale = d ** -0.5
    qb = pl.program_id(1)
    acc[...] = jnp.zeros_like(acc)
    carry[...] = jnp.zeros_like(carry)
    row = lax.broadcasted_iota(jnp.int32, (blk, blk), 0)
    col = lax.broadcasted_iota(jnp.int32, (blk, blk), 1)
    dif = col - row

    def body(i, c):
        start = pl.multiple_of((qb - i) * blk, blk)
        mask = dif < i * blk
        heads = [slice(hh * d, (hh + 1) * d) for hh in range(SB_HEADS)]
        zs = [_dot_nt(q_ref[:, sl], k_ref[pl.ds(start, blk), sl]) * scale for sl in heads]
        log_betas, hilos = [], []
        for z in zs:
            sp = jnp.log(1.0 + jnp.exp(-jnp.abs(z)))
            log_beta = jnp.minimum(z, 0.0) - sp
            log_fail = jnp.where(mask, log_beta - z, 0.0)
            hi = log_fail.astype(BF16)
            lo = (log_fail - hi.astype(F32)).astype(BF16)
            log_betas.append(log_beta)
            hilos.append(jnp.concatenate([hi, lo], axis=1))
        s_all = jnp.dot(jnp.concatenate(hilos, axis=0), u_ref[...], preferred_element_type=F32)
        ws = []
        for hh, sl in enumerate(heads):
            s = s_all[hh * blk:(hh + 1) * blk]
            ws.append(jnp.where(mask, jnp.exp(log_betas[hh] + carry[:, sl] + s[:, :blk]), 0.0).astype(BF16))
            carry[:, sl] += s[:, blk:]
        for w, sl in zip(ws, heads):
            acc[:, sl] += jnp.dot(w, v_ref[pl.ds(start, blk), sl], preferred_element_type=F32)
        return c

    lax.fori_loop(0, qb + 1, body, 0)
    o_ref[...] = acc[...].astype(o_ref.dtype)


def _stick_breaking(h, batch, seq, col0):
    n = h.shape[0]
    blk = SB_BLOCK
    w = SB_HEADS * HEAD_DIM
    nq = seq // blk
    idx = jnp.arange(blk)
    tri = (idx[:, None] > idx[None, :]).astype(BF16)
    half = jnp.concatenate([tri, jnp.ones((blk, blk), BF16)], axis=1)
    u = jnp.concatenate([half, half], axis=0)
    return pl.pallas_call(
        _sb_kernel,
        out_shape=jax.ShapeDtypeStruct((n, w), BF16),
        grid=(batch, nq),
        in_specs=[pl.BlockSpec((blk, w), lambda b, t: (b * nq + t, col0)),
                  pl.BlockSpec((seq, w), lambda b, t: (b, col0 + 1)),
                  pl.BlockSpec((seq, w), lambda b, t: (b, col0 + 2)),
                  pl.BlockSpec((2 * blk, 2 * blk), lambda b, t: (0, 0))],
        out_specs=pl.BlockSpec((blk, w), lambda b, t: (b * nq + t, 0)),
        scratch_shapes=[pltpu.VMEM((blk, w), F32), pltpu.VMEM((blk, w), F32)],
        compiler_params=_params(("arbitrary", "arbitrary")),
        name="stick_breaking",
    )(h, h, h, u)


def _ffn_kernel(be_ref, ns_ref, x_ref, w1_ref, w3_ref, w2_ref, y_hbm, acc, sem, *, all_rows):
    i = pl.program_id(0)
    j = pl.program_id(1)
    tm = x_ref.shape[0]
    pair = 2 * FFN_SUB
    nsub = ns_ref[i]

    @pl.when(j == 0)
    def _():
        acc[...] = jnp.zeros_like(acc)

    def chunk(start, size):
        rows = pl.ds(start, size)
        x = x_ref[rows, :]
        a = jnp.dot(x, w1_ref[...].astype(BF16), preferred_element_type=F32)
        b = jnp.dot(x, w3_ref[...].astype(BF16), preferred_element_type=F32)
        hmid = (a * jax.nn.sigmoid(a) * b).astype(BF16)
        acc[rows, :] += jnp.dot(hmid, w2_ref[...].astype(BF16), preferred_element_type=F32)

    if all_rows:
        for s in range(tm // pair):
            chunk(s * pair, pair)
    else:
        def pair_body(s, carry):
            chunk(pl.multiple_of(s * pair, pair), pair)
            return carry

        lax.fori_loop(0, nsub // 2, pair_body, 0)

        @pl.when(nsub % 2 == 1)
        def _():
            chunk(pl.multiple_of((nsub - 1) * FFN_SUB, FFN_SUB), FFN_SUB)

    @pl.when(j == pl.num_programs(1) - 1)
    def _():
        out = pltpu.make_async_copy(acc, y_hbm.at[pl.ds(pl.multiple_of(i * tm, tm), tm), :], sem)
        out.start()
        out.wait()


def _ffn(x, w1, w3, w2, block_e, block_nsub, name, all_rows=False):
    p, d = x.shape
    f = w1.shape[2]
    tm = min(FFN_ROWS, p)
    tf = min(FFN_TF, f)
    assert p % tm == 0 and f % tf == 0 and tm % FFN_SUB == 0
    nj = f // tf

    def jj(i, j, ns):
        return jnp.where(ns[i] > 0, j, nj - 1)

    return pl.pallas_call(
        functools.partial(_ffn_kernel, all_rows=all_rows),
        out_shape=jax.ShapeDtypeStruct((p, d), F32),
        grid_spec=pltpu.PrefetchScalarGridSpec(
            num_scalar_prefetch=2,
            grid=(p // tm, nj),
            in_specs=[pl.BlockSpec((tm, d), lambda i, j, be, ns: (i, 0)),
                      pl.BlockSpec((None, d, tf), lambda i, j, be, ns: (be[i], 0, jj(i, j, ns))),
                      pl.BlockSpec((None, d, tf), lambda i, j, be, ns: (be[i], 0, jj(i, j, ns))),
                      pl.BlockSpec((None, tf, d), lambda i, j, be, ns: (be[i], jj(i, j, ns), 0))],
            out_specs=pl.BlockSpec(memory_space=pl.ANY),
            scratch_shapes=[pltpu.VMEM((tm, d), F32), pltpu.SemaphoreType.DMA(())]),
        compiler_params=_params(("arbitrary", "arbitrary")),
        name=name,
    )(block_e, block_nsub, x, w1, w3, w2)


def _route_kernel(x_ref, w_ref, idx_ref, gate_ref, *, n_experts):
    logits = jnp.dot(x_ref[...], w_ref[...], preferred_element_type=F32, precision=lax.Precision.HIGHEST)
    lane_i = lax.broadcasted_iota(jnp.int32, logits.shape, 1)
    lane = lane_i.astype(F32)
    neg = jnp.float32(-jnp.inf)
    logits = jnp.where(lane_i < n_experts, logits, neg)
    m1 = jnp.max(logits, axis=-1, keepdims=True)
    i1 = jnp.min(jnp.where(logits == m1, lane, float(LANES)), axis=-1, keepdims=True)
    rest = jnp.where(lane == i1, neg, logits)
    m2 = jnp.max(rest, axis=-1, keepdims=True)
    i2 = jnp.min(jnp.where(rest == m2, lane, float(LANES)), axis=-1, keepdims=True)
    e2 = jnp.exp(m2 - m1)
    den = 1.0 + e2
    idx_ref[...] = jnp.where(lane_i == 0, i1, jnp.where(lane_i == 1, i2, 0.0)).astype(jnp.int32)
    gate_ref[...] = jnp.where(lane_i == 0, 1.0 / den, jnp.where(lane_i == 1, e2 / den, 0.0))


def _route(x, w_router):
    n, d = x.shape
    e = w_router.shape[1]
    tm = min(ROUTE_ROWS, n)
    wpad = jnp.zeros((d, LANES), F32).at[:, :e].set(w_router)
    idx, gate = pl.pallas_call(
        functools.partial(_route_kernel, n_experts=e),
        out_shape=(jax.ShapeDtypeStruct((n, LANES), jnp.int32), jax.ShapeDtypeStruct((n, LANES), F32)),
        grid=(n // tm,),
        in_specs=[pl.BlockSpec((tm, d), lambda i: (i, 0)), pl.BlockSpec((d, LANES), lambda i: (0, 0))],
        out_specs=(pl.BlockSpec((tm, LANES), lambda i: (i, 0)), pl.BlockSpec((tm, LANES), lambda i: (i, 0))),
        compiler_params=_params(("arbitrary",)),
        name="route_top2",
    )(x, wpad)
    return idx[:, :TOP_K], gate[:, :TOP_K]


def _gather_kernel(tok_ref, ns_ref, x_hbm, o_ref, buf, sem):
    s = pl.program_id(0)
    per = FFN_ROWS // GATHER_ROWS
    nonempty = (s % per) < ns_ref[s // per]

    @pl.when(nonempty)
    def _():
        base = s * GATHER_ROWS

        def issue(r, carry):
            t = tok_ref[base + r]
            pltpu.make_async_copy(x_hbm.at[pl.ds(t, 1), :], buf.at[pl.ds(r, 1), :], sem).start()
            return carry

        lax.fori_loop(0, GATHER_ROWS, issue, 0, unroll=8)
        pltpu.make_async_copy(x_hbm.at[pl.ds(0, GATHER_ROWS), :], buf, sem).wait()
        o_ref[...] = buf[...].astype(o_ref.dtype)

    @pl.when(jnp.logical_not(nonempty))
    def _():
        o_ref[...] = jnp.zeros_like(o_ref)


def _gather_rows(x, slot_tok, block_nsub, p):
    n, d = x.shape
    return pl.pallas_call(
        _gather_kernel,
        out_shape=jax.ShapeDtypeStruct((p, d), BF16),
        grid_spec=pltpu.PrefetchScalarGridSpec(
            num_scalar_prefetch=2,
            grid=(p // GATHER_ROWS,),
            in_specs=[pl.BlockSpec(memory_space=pl.ANY)],
            out_specs=pl.BlockSpec((GATHER_ROWS, d), lambda s, tok, ns: (s, 0)),
            scratch_shapes=[pltpu.VMEM((GATHER_ROWS, d), F32), pltpu.SemaphoreType.DMA(())]),
        compiler_params=_params(("arbitrary",)),
        name="moe_gather",
    )(slot_tok, block_nsub, x)


def _combine_kernel(p0_ref, p1_ref, y_hbm, x_ref, ple_ref, g0_ref, g1_ref, lg_ref, lb_ref,
                    o_ref, ob_ref, buf, sem):
    i = pl.program_id(0)
    tm = x_ref.shape[0]
    base = i * tm

    def issue(r, carry):
        pltpu.make_async_copy(y_hbm.at[pl.ds(p0_ref[base + r], 1), :], buf.at[0, pl.ds(r, 1), :], sem).start()
        pltpu.make_async_copy(y_hbm.at[pl.ds(p1_ref[base + r], 1), :], buf.at[1, pl.ds(r, 1), :], sem).start()
        return carry

    lax.fori_loop(0, tm, issue, 0, unroll=4)
    pltpu.make_async_copy(y_hbm.at[pl.ds(0, tm), :], buf.at[0], sem).wait()
    pltpu.make_async_copy(y_hbm.at[pl.ds(0, tm), :], buf.at[1], sem).wait()
    f = g0_ref[...] * buf[0] + g1_ref[...] * buf[1]
    out = _ln_body(DEEPNORM_ALPHA * x_ref[...] + f + ple_ref[...].astype(F32), lg_ref[...], lb_ref[...])
    o_ref[...] = out
    ob_ref[...] = out.astype(BF16)


def _combine_ln(y, pos0, pos1, g0, g1, x, ple, ln_g, ln_b):
    n, d = x.shape
    tm = min(LN_ROWS, n)
    tile = lambda: pl.BlockSpec((tm, d), lambda i, a, b: (i, 0))
    colv = lambda: pl.BlockSpec((tm, 1), lambda i, a, b: (i, 0))
    row = lambda: pl.BlockSpec((1, d), lambda i, a, b: (0, 0))
    return pl.pallas_call(
        _combine_kernel,
        out_shape=(jax.ShapeDtypeStruct((n, d), F32), jax.ShapeDtypeStruct((n, d), BF16)),
        grid_spec=pltpu.PrefetchScalarGridSpec(
            num_scalar_prefetch=2,
            grid=(n // tm,),
            in_specs=[pl.BlockSpec(memory_space=pl.ANY), tile(), tile(), colv(), colv(), row(), row()],
            out_specs=(tile(), tile()),
            scratch_shapes=[pltpu.VMEM((2, tm, d), F32), pltpu.SemaphoreType.DMA(())]),
        compiler_params=_params(("arbitrary",)),
        name="moe_combine_ln",
    )(pos0, pos1, y, x, ple, g0.reshape(n, 1), g1.reshape(n, 1), ln_g.reshape(1, d), ln_b.reshape(1, d))


def _moe_plan(top_i, n_experts):
    n = top_i.shape[0]
    nk = n * TOP_K
    nblk = nk // FFN_ROWS + n_experts
    sub_per_blk = FFN_ROWS // FFN_SUB
    e_flat = top_i.reshape(nk)
    onehot = (e_flat[:, None] == jnp.arange(n_experts, dtype=jnp.int32)[None, :]).astype(jnp.int32)
    csum = jnp.cumsum(onehot, axis=0)
    rank = jnp.sum((csum - onehot) * onehot, axis=1)
    counts = csum[-1]
    nsub_e = (counts + FFN_SUB - 1) // FFN_SUB
    nblk_e = (nsub_e + sub_per_blk - 1) // sub_per_blk
    rows_e = jnp.maximum((nsub_e + nblk_e - 1) // jnp.maximum(nblk_e, 1), 1) * FFN_SUB
    blk_end = jnp.cumsum(nblk_e)
    blk_off = blk_end - nblk_e
    rpb = rows_e[e_flat]
    pos = (blk_off[e_flat] + rank // rpb) * FFN_ROWS + rank % rpb
    blk = jnp.arange(nblk, dtype=jnp.int32)
    used = blk < blk_end[-1]
    block_e = jnp.clip(jnp.searchsorted(blk_end, blk, side="right"), 0, n_experts - 1).astype(jnp.int32)
    block_e = jnp.where(used, block_e, block_e[jnp.maximum(blk_end[-1] - 1, 0)])
    rows = jnp.clip(counts[block_e] - (blk - blk_off[block_e]) * rows_e[block_e], 0, rows_e[block_e])
    rows = jnp.where(used, rows, 0)
    block_nsub = ((rows + FFN_SUB - 1) // FFN_SUB).astype(jnp.int32)
    tok_flat = jnp.arange(nk, dtype=jnp.int32) // TOP_K
    slot_tok = jnp.zeros((nblk * FFN_ROWS,), jnp.int32).at[pos].set(tok_flat)
    return pos.reshape(n, TOP_K).astype(jnp.int32), slot_tok, block_e, block_nsub, nblk * FFN_ROWS


def _token_mixer(xb, layer, w_in, w_br_ret, w_br_sb, w_gate, b_gate, batch, seq):
    d = xb.shape[1]
    in_width = w_in.shape[2]
    h = _mm([xb[None]], [(0, w_in, 0)], [], layer, _ep_identity, in_width, BF16, 1024, 1024, "in_proj")
    o_ret = _retention(h, batch, seq)
    o_sb = _stick_breaking(h, batch, seq, (4 * RET_HEADS) // SB_HEADS)
    tn = min(512, d)
    bg = b_gate.reshape(b_gate.shape[0], 1, 2 * d)
    return _mm([o_ret[None], o_sb[None], xb[None]],
               [(0, w_br_ret, 0), (1, w_br_sb, 0), (2, w_gate, 0), (2, w_gate, d // tn)],
               [(bg, 0), (bg, d // tn)], layer, _ep_merge, d, BF16, 512, tn, "branch_merge")


def kernel(x, p, w_in, w_br_ret, w_br_sb, w_gate, b_gate, w_o, ln1_g, ln1_b,
           ffn_w1, ffn_w3, ffn_w2, moe_router, moe_w1, moe_w3, moe_w2,
           ple_w, ple_gate_w, ln2_g, ln2_b):
    batch, seq, d = x.shape
    n = batch * seq
    depth = w_in.shape[0]
    n_experts = moe_router.shape[2]
    xf = x.reshape(n, d)
    xb = xf.astype(BF16)
    pf = p.reshape(depth, n, p.shape[3])
    ew1 = moe_w1.reshape((-1,) + moe_w1.shape[2:])
    ew3 = moe_w3.reshape((-1,) + moe_w3.shape[2:])
    ew2 = moe_w2.reshape((-1,) + moe_w2.shape[2:])
    for i in range(depth):
        merged = _token_mixer(xb, i, w_in, w_br_ret, w_br_sb, w_gate, b_gate, batch, seq)
        xf, xb = _proj_ln(merged, w_o, i, xf, ln1_g[i], ln1_b[i], "out_proj_ln")
        ple = _mm([xb[None], pf], [(0, ple_gate_w, 0), (1, ple_w, 0)], [], i, _ep_ple, d, BF16, 1024, 1024, "ple")
        if i % 2 == 0:
            nblk = n // min(FFN_ROWS, n)
            f = _ffn(xb, ffn_w1, ffn_w3, ffn_w2, jnp.full((nblk,), i // 2, jnp.int32),
                     jnp.full((nblk,), FFN_ROWS // FFN_SUB, jnp.int32), "dense_swiglu", all_rows=True)
            xf, xb = _ln(xf, [f, ple], ln2_g[i], ln2_b[i], "ln_ffn")
        else:
            top_i, gate = _route(xf, moe_router[i // 2])
            pos, slot_tok, block_e, block_nsub, slots = _moe_plan(top_i, n_experts)
            xs = _gather_rows(xf, slot_tok, block_nsub, slots)
            y = _ffn(xs, ew1, ew3, ew2, block_e + (i // 2) * n_experts, block_nsub, "expert_swiglu")
            xf, xb = _combine_ln(y, pos[:, 0], pos[:, 1], gate[:, 0], gate[:, 1], xf, ple, ln2_g[i], ln2_b[i])
    return xf.reshape(batch, seq, d)
```

```python
import functools
import math

import jax
import jax.numpy as jnp
from jax import lax
from jax.experimental import pallas as pl
from jax.experimental.pallas import tpu as pltpu

F32 = jnp.float32
BF16 = jnp.bfloat16

RET_HEADS = 8
SB_HEADS = 8
HEAD_DIM = 128
RET_CHUNK = 128
ROPE_BASE = 10000.0
TOP_K = 2
DEPTH = 2
DEEPNORM_ALPHA = (2 * DEPTH) ** 0.25
LN_EPS = 1e-5

V7X_VMEM_BYTES = 64 * 1024 * 1024
VMEM_LIMIT = V7X_VMEM_BYTES - 8 * 1024 * 1024
LANES = 128

SB_BLOCK = 128
PROJ_LN_ROWS = 512
FFN_ROWS = 2048
FFN_SUB = 256
FFN_TF = 256
LN_ROWS = 256
ROUTE_ROWS = 512
GATHER_ROWS = FFN_SUB


def _params(sem):
    return pltpu.CompilerParams(dimension_semantics=sem, vmem_limit_bytes=VMEM_LIMIT)


def _mm_kernel(*refs, a_of, n_a, n_extra, epilogue):
    n_prod = len(a_of)
    a_refs = refs[:n_a]
    b_refs = refs[n_a:n_a + n_prod]
    e_refs = refs[n_a + n_prod:n_a + n_prod + n_extra]
    o_ref = refs[n_a + n_prod + n_extra]
    b_scr = refs[n_a + n_prod + n_extra + 1:]

    @pl.when(pl.program_id(1) == 0)
    def _():
        for b_ref, s in zip(b_refs, b_scr):
            s[...] = b_ref[...].astype(BF16)

    a_vals = [a[...].astype(BF16) for a in a_refs]
    accs = [jnp.dot(a_vals[ai], s[...], preferred_element_type=F32) for ai, s in zip(a_of, b_scr)]
    o_ref[...] = epilogue(accs, [e[...] for e in e_refs]).astype(o_ref.dtype)


def _mm(a_ops, products, extras, layer, epilogue, n_out, out_dtype, tm, tn, name):
    m = a_ops[0].shape[1]
    tm = min(tm, m)
    tn = min(tn, n_out)
    assert m % tm == 0 and n_out % tn == 0
    in_specs, args, scratch = [], [], []
    for a in a_ops:
        la = layer if a.shape[0] > 1 else 0
        in_specs.append(pl.BlockSpec((None, tm, a.shape[2]), lambda j, i, la=la: (la, i, 0)))
        args.append(a)
    for _, b, off in products:
        in_specs.append(pl.BlockSpec((None, b.shape[1], tn), lambda j, i, off=off: (layer, 0, j + off)))
        args.append(b)
        scratch.append(pltpu.VMEM((b.shape[1], tn), BF16))
    for e, off in extras:
        le = layer if e.shape[0] > 1 else 0
        in_specs.append(pl.BlockSpec((None, 1, tn), lambda j, i, off=off, le=le: (le, 0, j + off)))
        args.append(e)
    kern = functools.partial(_mm_kernel, a_of=tuple(ai for ai, _, _ in products), n_a=len(a_ops),
                             n_extra=len(extras), epilogue=epilogue)
    return pl.pallas_call(
        kern,
        out_shape=jax.ShapeDtypeStruct((m, n_out), out_dtype),
        grid=(n_out // tn, m // tm),
        in_specs=in_specs,
        out_specs=pl.BlockSpec((tm, tn), lambda j, i: (i, j)),
        scratch_shapes=scratch,
        compiler_params=_params(("arbitrary", "arbitrary")),
        name=name,
    )(*args)


def _ep_colscale(accs, extras):
    return accs[0] * extras[0]


def _ep_merge(accs, extras):
    o_r, o_s, z_r, z_s = accs
    b_r, b_s = extras
    return jax.nn.sigmoid(z_r + b_r) * o_r + jax.nn.sigmoid(z_s + b_s) * o_s


def _ep_ple(accs, extras):
    return jax.nn.sigmoid(accs[0]) * accs[1]


def _ln_body(y, g, b):
    mu = jnp.mean(y, axis=-1, keepdims=True)
    yc = y - mu
    var = jnp.mean(yc * yc, axis=-1, keepdims=True)
    return yc * lax.rsqrt(var + LN_EPS) * g + b


def _ln_kernel(*refs, n_add):
    x_ref = refs[0]
    add_refs = refs[1:1 + n_add]
    g_ref, b_ref, o_ref, ob_ref = refs[1 + n_add:]
    y = DEEPNORM_ALPHA * x_ref[...]
    for a in add_refs:
        y = y + a[...].astype(F32)
    out = _ln_body(y, g_ref[...], b_ref[...])
    o_ref[...] = out
    ob_ref[...] = out.astype(BF16)


def _ln(x, adds, g, b, name):
    n, d = x.shape
    tm = min(LN_ROWS, n)
    tile = pl.BlockSpec((tm, d), lambda i: (i, 0))
    row = pl.BlockSpec((1, d), lambda i: (0, 0))
    return pl.pallas_call(
        functools.partial(_ln_kernel, n_add=len(adds)),
        out_shape=(jax.ShapeDtypeStruct((n, d), F32), jax.ShapeDtypeStruct((n, d), BF16)),
        grid=(n // tm,),
        in_specs=[tile] * (1 + len(adds)) + [row, row],
        out_specs=(tile, tile),
        compiler_params=_params(("arbitrary",)),
        name=name,
    )(x, *adds, g.reshape(1, d), b.reshape(1, d))


def _proj_ln_kernel(a_ref, w_ref, x_ref, g_ref, b_ref, o_ref, ob_ref, w_scr):
    @pl.when(pl.program_id(0) == 0)
    def _():
        w_scr[...] = w_ref[...].astype(BF16)

    mix = jnp.dot(a_ref[...], w_scr[...], preferred_element_type=F32)
    out = _ln_body(DEEPNORM_ALPHA * x_ref[...] + mix, g_ref[...], b_ref[...])
    o_ref[...] = out
    ob_ref[...] = out.astype(BF16)


def _proj_ln(a, w, layer, x, g, b, name):
    n, d = x.shape
    k = a.shape[1]
    tm = min(PROJ_LN_ROWS, n)
    tile = pl.BlockSpec((tm, d), lambda i: (i, 0))
    row = pl.BlockSpec((1, d), lambda i: (0, 0))
    return pl.pallas_call(
        _proj_ln_kernel,
        out_shape=(jax.ShapeDtypeStruct((n, d), F32), jax.ShapeDtypeStruct((n, d), BF16)),
        grid=(n // tm,),
        in_specs=[pl.BlockSpec((tm, k), lambda i: (i, 0)),
                  pl.BlockSpec((None, k, d), lambda i: (layer, 0, 0), pipeline_mode=pl.Buffered(1)),
                  tile, row, row],
        out_specs=(tile, tile),
        scratch_shapes=[pltpu.VMEM((k, d), BF16)],
        compiler_params=_params(("arbitrary",)),
        name=name,
    )(a, w, x, g.reshape(1, d), b.reshape(1, d))


def _dot_nt(a, b):
    return lax.dot_general(a, b, (((1,), (1,)), ((), ())), preferred_element_type=F32)


def _dot_tn(a, b):
    return lax.dot_general(a, b, (((0,), (0,)), ((), ())), preferred_element_type=F32)


def _ret_kernel(q_ref, k_ref, v_ref, g_ref, cos_ref, sin_ref, decay_ref, xi_ref, zeta_ref,
                o_ref, r_scr, *, chunk_decay):
    d = HEAD_DIM

    @pl.when(pl.program_id(1) == 0)
    def _():
        r_scr[...] = jnp.zeros_like(r_scr)

    cos = cos_ref[...]
    sin = sin_ref[...]

    def rot(t):
        return t * cos + pltpu.roll(t, d // 2, 1) * sin

    heads = [slice(h * d, (h + 1) * d) for h in range(RET_HEADS)]
    qbs, kbs, kzs = [], [], []
    for h, sl in enumerate(heads):
        kr = rot(k_ref[:, sl].astype(F32))
        qbs.append(rot(q_ref[:, sl].astype(F32)).astype(BF16))
        kbs.append(kr.astype(BF16))
        kzs.append((kr * zeta_ref[h]).astype(BF16))
    inners = [_dot_nt(qb, kb) for qb, kb in zip(qbs, kbs)]
    crosses = [jnp.dot(qb, r_scr[h].astype(BF16), preferred_element_type=F32) for h, qb in enumerate(qbs)]
    kvs = [_dot_tn(kz, v_ref[:, sl]) for kz, sl in zip(kzs, heads)]
    for h in range(RET_HEADS):
        r_scr[h] = r_scr[h] * chunk_decay[h] + kvs[h]
    pbs = [(inner * decay_ref[h]).astype(BF16) for h, inner in enumerate(inners)]
    outs = [jnp.dot(pb, v_ref[:, sl], preferred_element_type=F32) for pb, sl in zip(pbs, heads)]
    for h, sl in enumerate(heads):
        o = outs[h] + crosses[h] * xi_ref[h]
        mu = jnp.mean(o, axis=-1, keepdims=True)
        oc = o - mu
        var = jnp.mean(oc * oc, axis=-1, keepdims=True)
        g = g_ref[:, sl].astype(F32)
        o_ref[:, sl] = (oc * lax.rsqrt(var + LN_EPS) * (g * jax.nn.sigmoid(g))).astype(o_ref.dtype)


def _retention(h, batch, seq):
    n = h.shape[0]
    c = RET_CHUNK
    d = HEAD_DIM
    w = RET_HEADS * d
    nc = seq // c
    half = d // 2
    pos = jnp.arange(seq, dtype=F32)
    inv = ROPE_BASE ** (-jnp.arange(half, dtype=F32) / half)
    ang = pos[:, None] * inv[None, :]
    cos = jnp.concatenate([jnp.cos(ang), jnp.cos(ang)], axis=1)
    sin = jnp.concatenate([-jnp.sin(ang), jnp.sin(ang)], axis=1)
    gamma = 1.0 - jnp.exp2(-5.0 - jnp.arange(RET_HEADS, dtype=F32))
    lg = jnp.log(gamma)
    idx = jnp.arange(c, dtype=F32)
    diff = idx[:, None] - idx[None, :]
    causal = diff >= 0
    decay = jnp.where(causal[None], jnp.exp(jnp.where(causal, diff, 0.0)[None] * lg[:, None, None]), 0.0)
    xi = jnp.broadcast_to(jnp.exp((idx + 1.0)[None, :] * lg[:, None])[:, :, None], (RET_HEADS, c, d))
    zeta = jnp.broadcast_to(jnp.exp((c - 1.0 - idx)[None, :] * lg[:, None])[:, :, None], (RET_HEADS, c, d))
    chunk_decay = tuple(math.exp(c * math.log(1.0 - 2.0 ** (-5.0 - hh))) for hh in range(RET_HEADS))

    def col(j):
        return pl.BlockSpec((c, w), lambda b, t, j=j: (b * nc + t, j))

    tab = pl.BlockSpec((c, d), lambda b, t: (t, 0))
    hconst = pl.BlockSpec((RET_HEADS, c, d), lambda b, t: (0, 0, 0))
    return pl.pallas_call(
        functools.partial(_ret_kernel, chunk_decay=chunk_decay),
        out_shape=jax.ShapeDtypeStruct((n, w), BF16),
        grid=(batch, nc),
        in_specs=[col(0), col(1), col(2), col(3), tab, tab,
                  pl.BlockSpec((RET_HEADS, c, c), lambda b, t: (0, 0, 0)), hconst, hconst],
        out_specs=pl.BlockSpec((c, w), lambda b, t: (b * nc + t, 0)),
        scratch_shapes=[pltpu.VMEM((RET_HEADS, d, d), F32)],
        compiler_params=_params(("arbitrary", "arbitrary")),
        name="retention",
    )(h, h, h, h, cos, sin, decay, xi, zeta)


def _sb_kernel(q_ref, k_ref, v_ref, u_ref, o_ref, acc, carry):
    blk = SB_BLOCK
    d = HEAD_DIM
    qb = pl.program_id(1)
    acc[...] = jnp.zeros_like(acc)
    carry[...] = jnp.zeros_like(carry)
    row = lax.broadcasted_iota(jnp.int32, (blk, blk), 0)
    col = lax.broadcasted_iota(jnp.int32, (blk, blk), 1)
    heads = [slice(hh * d, (hh + 1) * d) for hh in range(SB_HEADS)]
    sign = jnp.uint32(0x80000000)

    def sweep(kb, nblk, mask):
        keys = pl.ds(pl.multiple_of(kb * blk, blk), nblk * blk)
        zs = [_dot_nt(q_ref[:, sl], k_ref[keys, sl]) for sl in heads]
        log_betas, hilos = [], []
        for z in zs:
            neg_abs = lax.bitcast_convert_type(lax.bitcast_convert_type(z, jnp.uint32) | sign, F32)
            log_beta = jnp.minimum(z, 0.0) - jnp.log2(1.0 + jnp.exp2(neg_abs))
            log_fail = log_beta - z
            if mask is not None:
                log_fail = jnp.where(mask, log_fail, 0.0)
            hi = log_fail.astype(BF16)
            lo = (log_fail - hi.astype(F32)).astype(BF16)
            log_betas.append(log_beta)
            for c in range(nblk):
                cols = slice(c * blk, (c + 1) * blk)
                hilos.append(jnp.concatenate([hi[:, cols], lo[:, cols]], axis=1))
        s_all = jnp.dot(jnp.concatenate(hilos, axis=0), u_ref[...], preferred_element_type=F32)
        ws = []
        for hh, sl in enumerate(heads):
            run = carry[:, sl]
            parts = [None] * nblk
            for c in reversed(range(nblk)):
                s = s_all[(hh * nblk + c) * blk:(hh * nblk + c + 1) * blk]
                parts[c] = log_betas[hh][:, c * blk:(c + 1) * blk] + run + s[:, :blk]
                run = run + s[:, blk:]
            carry[:, sl] = run
            w = jnp.exp2(parts[0] if nblk == 1 else jnp.concatenate(parts, axis=1))
            if mask is not None:
                w = jnp.where(mask, w, 0.0)
            ws.append(w.astype(BF16))
        for w, sl in zip(ws, heads):
            acc[:, sl] += jnp.dot(w, v_ref[keys, sl], preferred_element_type=F32)

    sweep(qb, 1, col < row)

    @pl.when(qb % 2 == 1)
    def _():
        sweep(qb - 1, 1, None)

    def body(i, c):
        sweep((qb // 2 - 1 - i) * 2, 2, None)
        return c

    lax.fori_loop(0, qb // 2, body, 0)
    o_ref[...] = acc[...].astype(o_ref.dtype)


def _stick_breaking(h, batch, seq, col0):
    n = h.shape[0]
    blk = SB_BLOCK
    w = SB_HEADS * HEAD_DIM
    nq = seq // blk
    idx = jnp.arange(blk)
    tri = (idx[:, None] > idx[None, :]).astype(BF16)
    half = jnp.concatenate([tri, jnp.ones((blk, blk), BF16)], axis=1)
    u = jnp.concatenate([half, half], axis=0)
    return pl.pallas_call(
        _sb_kernel,
        out_shape=jax.ShapeDtypeStruct((n, w), BF16),
        grid=(batch, nq),
        in_specs=[pl.BlockSpec((blk, w), lambda b, t: (b * nq + t, col0)),
                  pl.BlockSpec((seq, w), lambda b, t: (b, col0 + 1)),
                  pl.BlockSpec((seq, w), lambda b, t: (b, col0 + 2)),
                  pl.BlockSpec((2 * blk, 2 * blk), lambda b, t: (0, 0))],
        out_specs=pl.BlockSpec((blk, w), lambda b, t: (b * nq + t, 0)),
        scratch_shapes=[pltpu.VMEM((blk, w), F32), pltpu.VMEM((blk, w), F32)],
        compiler_params=_params(("arbitrary", "arbitrary")),
        name="stick_breaking",
    )(h, h, h, u)


def _ffn_kernel(be_ref, ns_ref, x_ref, w1_ref, w3_ref, w2_ref, y_hbm, acc, sem, *, all_rows):
    i = pl.program_id(0)
    j = pl.program_id(1)
    tm = x_ref.shape[0]
    pair = 2 * FFN_SUB
    nsub = ns_ref[i]

    @pl.when(j == 0)
    def _():
        acc[...] = jnp.zeros_like(acc)

    def chunk(start, size):
        rows = pl.ds(start, size)
        x = x_ref[rows, :]
        a = jnp.dot(x, w1_ref[...].astype(BF16), preferred_element_type=F32)
        b = jnp.dot(x, w3_ref[...].astype(BF16), preferred_element_type=F32)
        hmid = (a * jax.nn.sigmoid(a) * b).astype(BF16)
        acc[rows, :] += jnp.dot(hmid, w2_ref[...].astype(BF16), preferred_element_type=F32)

    if all_rows:
        for s in range(tm // pair):
            chunk(s * pair, pair)
    else:
        def pair_body(s, carry):
            chunk(pl.multiple_of(s * pair, pair), pair)
            return carry

        lax.fori_loop(0, nsub // 2, pair_body, 0)

        @pl.when(nsub % 2 == 1)
        def _():
            chunk(pl.multiple_of((nsub - 1) * FFN_SUB, FFN_SUB), FFN_SUB)

    @pl.when(j == pl.num_programs(1) - 1)
    def _():
        out = pltpu.make_async_copy(acc, y_hbm.at[pl.ds(pl.multiple_of(i * tm, tm), tm), :], sem)
        out.start()
        out.wait()


def _ffn(x, w1, w3, w2, block_e, block_nsub, name, all_rows=False):
    p, d = x.shape
    f = w1.shape[2]
    tm = min(FFN_ROWS, p)
    tf = min(FFN_TF, f)
    assert p % tm == 0 and f % tf == 0 and tm % FFN_SUB == 0
    nj = f // tf

    def jj(i, j, ns):
        return jnp.where(ns[i] > 0, j, nj - 1)

    return pl.pallas_call(
        functools.partial(_ffn_kernel, all_rows=all_rows),
        out_shape=jax.ShapeDtypeStruct((p, d), F32),
        grid_spec=pltpu.PrefetchScalarGridSpec(
            num_scalar_prefetch=2,
            grid=(p // tm, nj),
            in_specs=[pl.BlockSpec((tm, d), lambda i, j, be, ns: (i, 0)),
                      pl.BlockSpec((None, d, tf), lambda i, j, be, ns: (be[i], 0, jj(i, j, ns))),
                      pl.BlockSpec((None, d, tf), lambda i, j, be, ns: (be[i], 0, jj(i, j, ns))),
                      pl.BlockSpec((None, tf, d), lambda i, j, be, ns: (be[i], jj(i, j, ns), 0))],
            out_specs=pl.BlockSpec(memory_space=pl.ANY),
            scratch_shapes=[pltpu.VMEM((tm, d), F32), pltpu.SemaphoreType.DMA(())]),
        compiler_params=_params(("arbitrary", "arbitrary")),
        name=name,
    )(block_e, block_nsub, x, w1, w3, w2)


def _route_kernel(x_ref, w_ref, tri_ref, idx_ref, gate_ref, cnt_ref, run, *, n_experts):
    @pl.when(pl.program_id(0) == 0)
    def _():
        run[...] = jnp.zeros_like(run)

    logits = jnp.dot(x_ref[...], w_ref[...], preferred_element_type=F32, precision=lax.Precision.HIGHEST)
    lane_i = lax.broadcasted_iota(jnp.int32, logits.shape, 1)
    lane = lane_i.astype(F32)
    neg = jnp.float32(-jnp.inf)
    logits = jnp.where(lane_i < n_experts, logits, neg)
    m1 = jnp.max(logits, axis=-1, keepdims=True)
    i1 = jnp.min(jnp.where(logits == m1, lane, float(LANES)), axis=-1, keepdims=True)
    rest = jnp.where(lane == i1, neg, logits)
    m2 = jnp.max(rest, axis=-1, keepdims=True)
    i2 = jnp.min(jnp.where(rest == m2, lane, float(LANES)), axis=-1, keepdims=True)
    e2 = jnp.exp(m2 - m1)
    den = 1.0 + e2
    pick1 = jnp.where(lane == i1, 1.0, 0.0)
    pick2 = jnp.where(lane == i2, 1.0, 0.0)
    picks = pick1 + pick2
    before = run[...] + jnp.dot(tri_ref[...], picks.astype(BF16), preferred_element_type=F32)
    r1 = jnp.sum(before * pick1, axis=-1, keepdims=True)
    r2 = jnp.sum(before * pick2, axis=-1, keepdims=True)
    run[...] += jnp.sum(picks, axis=0, keepdims=True)
    cnt_ref[...] = run[...].astype(jnp.int32)
    idx_ref[...] = jnp.where(lane_i == 0, i1, jnp.where(lane_i == 1, i2, jnp.where(
        lane_i == 2, r1, jnp.where(lane_i == 3, r2, 0.0)))).astype(jnp.int32)
    gate_ref[...] = jnp.where(lane_i == 0, 1.0 / den, jnp.where(lane_i == 1, e2 / den, 0.0))


def _route(x, w_router):
    n, d = x.shape
    e = w_router.shape[1]
    tm = min(ROUTE_ROWS, n)
    wpad = jnp.zeros((d, LANES), F32).at[:, :e].set(w_router)
    t = jnp.arange(tm)
    tri = (t[:, None] > t[None, :]).astype(BF16)
    tile = pl.BlockSpec((tm, LANES), lambda i: (i, 0))
    idx, gate, cnt = pl.pallas_call(
        functools.partial(_route_kernel, n_experts=e),
        out_shape=(jax.ShapeDtypeStruct((n, LANES), jnp.int32), jax.ShapeDtypeStruct((n, LANES), F32),
                   jax.ShapeDtypeStruct((1, LANES), jnp.int32)),
        grid=(n // tm,),
        in_specs=[pl.BlockSpec((tm, d), lambda i: (i, 0)), pl.BlockSpec((d, LANES), lambda i: (0, 0)),
                  pl.BlockSpec((tm, tm), lambda i: (0, 0))],
        out_specs=(tile, tile, pl.BlockSpec((1, LANES), lambda i: (0, 0))),
        scratch_shapes=[pltpu.VMEM((1, LANES), F32)],
        compiler_params=_params(("arbitrary",)),
        name="route_top2",
    )(x, wpad, tri)
    return idx[:, :TOP_K], gate[:, :TOP_K], idx[:, TOP_K:2 * TOP_K], cnt[0, :e]


def _gather_kernel(tok_ref, ns_ref, x_hbm, o_ref, buf, sem):
    s = pl.program_id(0)
    per = FFN_ROWS // GATHER_ROWS
    nonempty = (s % per) < ns_ref[s // per]

    @pl.when(nonempty)
    def _():
        base = s * GATHER_ROWS

        def issue(r, carry):
            t = tok_ref[base + r]
            pltpu.make_async_copy(x_hbm.at[pl.ds(t, 1), :], buf.at[pl.ds(r, 1), :], sem).start()
            return carry

        lax.fori_loop(0, GATHER_ROWS, issue, 0, unroll=8)
        pltpu.make_async_copy(x_hbm.at[pl.ds(0, GATHER_ROWS), :], buf, sem).wait()
        o_ref[...] = buf[...].astype(o_ref.dtype)

    @pl.when(jnp.logical_not(nonempty))
    def _():
        o_ref[...] = jnp.zeros_like(o_ref)


def _gather_rows(x, slot_tok, block_nsub, p):
    n, d = x.shape
    return pl.pallas_call(
        _gather_kernel,
        out_shape=jax.ShapeDtypeStruct((p, d), BF16),
        grid_spec=pltpu.PrefetchScalarGridSpec(
            num_scalar_prefetch=2,
            grid=(p // GATHER_ROWS,),
            in_specs=[pl.BlockSpec(memory_space=pl.ANY)],
            out_specs=pl.BlockSpec((GATHER_ROWS, d), lambda s, tok, ns: (s, 0)),
            scratch_shapes=[pltpu.VMEM((GATHER_ROWS, d), F32), pltpu.SemaphoreType.DMA(())]),
        compiler_params=_params(("arbitrary",)),
        name="moe_gather",
    )(slot_tok, block_nsub, x)


def _combine_kernel(p0_ref, p1_ref, y_hbm, x_ref, ple_ref, g0_ref, g1_ref, lg_ref, lb_ref,
                    o_ref, ob_ref, buf, sem):
    i = pl.program_id(0)
    tm = x_ref.shape[0]
    base = i * tm

    def issue(r, carry):
        pltpu.make_async_copy(y_hbm.at[pl.ds(p0_ref[base + r], 1), :], buf.at[0, pl.ds(r, 1), :], sem).start()
        pltpu.make_async_copy(y_hbm.at[pl.ds(p1_ref[base + r], 1), :], buf.at[1, pl.ds(r, 1), :], sem).start()
        return carry

    lax.fori_loop(0, tm, issue, 0, unroll=4)
    pltpu.make_async_copy(y_hbm.at[pl.ds(0, tm), :], buf.at[0], sem).wait()
    pltpu.make_async_copy(y_hbm.at[pl.ds(0, tm), :], buf.at[1], sem).wait()
    f = g0_ref[...] * buf[0] + g1_ref[...] * buf[1]
    out = _ln_body(DEEPNORM_ALPHA * x_ref[...] + f + ple_ref[...].astype(F32), lg_ref[...], lb_ref[...])
    o_ref[...] = out
    ob_ref[...] = out.astype(BF16)


def _combine_ln(y, pos0, pos1, g0, g1, x, ple, ln_g, ln_b):
    n, d = x.shape
    tm = min(LN_ROWS, n)
    tile = lambda: pl.BlockSpec((tm, d), lambda i, a, b: (i, 0))
    colv = lambda: pl.BlockSpec((tm, 1), lambda i, a, b: (i, 0))
    row = lambda: pl.BlockSpec((1, d), lambda i, a, b: (0, 0))
    return pl.pallas_call(
        _combine_kernel,
        out_shape=(jax.ShapeDtypeStruct((n, d), F32), jax.ShapeDtypeStruct((n, d), BF16)),
        grid_spec=pltpu.PrefetchScalarGridSpec(
            num_scalar_prefetch=2,
            grid=(n // tm,),
            in_specs=[pl.BlockSpec(memory_space=pl.ANY), tile(), tile(), colv(), colv(), row(), row()],
            out_specs=(tile(), tile()),
            scratch_shapes=[pltpu.VMEM((2, tm, d), F32), pltpu.SemaphoreType.DMA(())]),
        compiler_params=_params(("arbitrary",)),
        name="moe_combine_ln",
    )(pos0, pos1, y, x, ple, g0.reshape(n, 1), g1.reshape(n, 1), ln_g.reshape(1, d), ln_b.reshape(1, d))


def _moe_plan(top_i, rank, counts, n_experts):
    n = top_i.shape[0]
    nk = n * TOP_K
    nblk = nk // FFN_ROWS + n_experts
    sub_per_blk = FFN_ROWS // FFN_SUB
    e_flat = top_i.reshape(nk)
    rank = rank.reshape(nk)
    nsub_e = (counts + FFN_SUB - 1) // FFN_SUB
    nblk_e = (nsub_e + sub_per_blk - 1) // sub_per_blk
    rows_e = jnp.maximum((nsub_e + nblk_e - 1) // jnp.maximum(nblk_e, 1), 1) * FFN_SUB
    blk_end = jnp.cumsum(nblk_e)
    blk_off = blk_end - nblk_e
    rpb = rows_e[e_flat]
    pos = (blk_off[e_flat] + rank // rpb) * FFN_ROWS + rank % rpb
    blk = jnp.arange(nblk, dtype=jnp.int32)
    used = blk < blk_end[-1]
    block_e = jnp.clip(jnp.searchsorted(blk_end, blk, side="right"), 0, n_experts - 1).astype(jnp.int32)
    block_e = jnp.where(used, block_e, block_e[jnp.maximum(blk_end[-1] - 1, 0)])
    rows = jnp.clip(counts[block_e] - (blk - blk_off[block_e]) * rows_e[block_e], 0, rows_e[block_e])
    rows = jnp.where(used, rows, 0)
    block_nsub = ((rows + FFN_SUB - 1) // FFN_SUB).astype(jnp.int32)
    tok_flat = jnp.arange(nk, dtype=jnp.int32) // TOP_K
    slot_tok = jnp.zeros((nblk * FFN_ROWS,), jnp.int32).at[pos].set(tok_flat)
    return pos.reshape(n, TOP_K).astype(jnp.int32), slot_tok, block_e, block_nsub, nblk * FFN_ROWS


def _token_mixer(xb, layer, w_in, w_br_ret, w_br_sb, w_gate, b_gate, batch, seq):
    d = xb.shape[1]
    in_width = w_in.shape[2]
    rw = RET_HEADS * HEAD_DIM
    sw = SB_HEADS * HEAD_DIM
    col_scale = jnp.ones((in_width,), F32)
    col_scale = col_scale.at[rw:2 * rw].set(HEAD_DIM ** -0.5)
    col_scale = col_scale.at[4 * rw:4 * rw + sw].set(HEAD_DIM ** -0.5 * math.log2(math.e))
    h = _mm([xb[None]], [(0, w_in, 0)], [(col_scale.reshape(1, 1, in_width), 0)], layer, _ep_colscale,
            in_width, BF16, 1024, 1024, "in_proj")
    o_ret = _retention(h, batch, seq)
    o_sb = _stick_breaking(h, batch, seq, (4 * RET_HEADS) // SB_HEADS)
    tn = min(512, d)
    bg = b_gate.reshape(b_gate.shape[0], 1, 2 * d)
    return _mm([o_ret[None], o_sb[None], xb[None]],
               [(0, w_br_ret, 0), (1, w_br_sb, 0), (2, w_gate, 0), (2, w_gate, d // tn)],
               [(bg, 0), (bg, d // tn)], layer, _ep_merge, d, BF16, 512, tn, "branch_merge")


def kernel(x, p, w_in, w_br_ret, w_br_sb, w_gate, b_gate, w_o, ln1_g, ln1_b,
           ffn_w1, ffn_w3, ffn_w2, moe_router, moe_w1, moe_w3, moe_w2,
           ple_w, ple_gate_w, ln2_g, ln2_b):
    batch, seq, d = x.shape
    n = batch * seq
    depth = w_in.shape[0]
    n_experts = moe_router.shape[2]
    xf = x.reshape(n, d)
    xb = xf.astype(BF16)
    pf = p.reshape(depth, n, p.shape[3])
    ew1 = moe_w1.reshape((-1,) + moe_w1.shape[2:])
    ew3 = moe_w3.reshape((-1,) + moe_w3.shape[2:])
    ew2 = moe_w2.reshape((-1,) + moe_w2.shape[2:])
    for i in range(depth):
        merged = _token_mixer(xb, i, w_in, w_br_ret, w_br_sb, w_gate, b_gate, batch, seq)
        xf, xb = _proj_ln(merged, w_o, i, xf, ln1_g[i], ln1_b[i], "out_proj_ln")
        ple = _mm([xb[None], pf], [(0, ple_gate_w, 0), (1, ple_w, 0)], [], i, _ep_ple, d, BF16, 1024, 1024, "ple")
        if i % 2 == 0:
            nblk = n // min(FFN_ROWS, n)
            f = _ffn(xb, ffn_w1, ffn_w3, ffn_w2, jnp.full((nblk,), i // 2, jnp.int32),
                     jnp.full((nblk,), FFN_ROWS // FFN_SUB, jnp.int32), "dense_swiglu", all_rows=True)
            xf, xb = _ln(xf, [f, ple], ln2_g[i], ln2_b[i], "ln_ffn")
        else:
            top_i, gate, rank, counts = _route(xf, moe_router[i // 2])
            pos, slot_tok, block_e, block_nsub, slots = _moe_plan(top_i, rank, counts, n_experts)
            xs = _gather_rows(xf, slot_tok, block_nsub, slots)
            y = _ffn(xs, ew1, ew3, ew2, block_e + (i // 2) * n_experts, block_nsub, "expert_swiglu")
            xf, xb = _combine_ln(y, pos[:, 0], pos[:, 1], gate[:, 0], gate[:, 1], xf, ple, ln2_g[i], ln2_b[i])
    return xf.reshape(batch, seq, d)
```

```python
import functools
import math

import jax
import jax.numpy as jnp
from jax import lax
from jax.experimental import pallas as pl
from jax.experimental.pallas import tpu as pltpu

F32 = jnp.float32
BF16 = jnp.bfloat16

RET_HEADS = 8
SB_HEADS = 8
HEAD_DIM = 128
RET_CHUNK = 128
ROPE_BASE = 10000.0
TOP_K = 2
DEPTH = 2
DEEPNORM_ALPHA = (2 * DEPTH) ** 0.25
LN_EPS = 1e-5

V7X_VMEM_BYTES = 64 * 1024 * 1024
VMEM_LIMIT = V7X_VMEM_BYTES - 8 * 1024 * 1024
LANES = 128

SB_BLOCK = 128
PROJ_LN_ROWS = 512
FFN_ROWS = 2048
FFN_SUB = 256
FFN_TF = 256
LN_ROWS = 256
ROUTE_ROWS = 512
META_ROWS = 8
GATHER_ROWS = FFN_SUB


def _params(sem):
    return pltpu.CompilerParams(dimension_semantics=sem, vmem_limit_bytes=VMEM_LIMIT)


def _mm_kernel(*refs, a_of, n_a, n_extra, epilogue):
    n_prod = len(a_of)
    a_refs = refs[:n_a]
    b_refs = refs[n_a:n_a + n_prod]
    e_refs = refs[n_a + n_prod:n_a + n_prod + n_extra]
    o_ref = refs[n_a + n_prod + n_extra]
    b_scr = refs[n_a + n_prod + n_extra + 1:]

    @pl.when(pl.program_id(1) == 0)
    def _():
        for b_ref, s in zip(b_refs, b_scr):
            s[...] = b_ref[...].astype(BF16)

    a_vals = [a[...].astype(BF16) for a in a_refs]
    accs = [jnp.dot(a_vals[ai], s[...], preferred_element_type=F32) for ai, s in zip(a_of, b_scr)]
    o_ref[...] = epilogue(accs, [e[...] for e in e_refs]).astype(o_ref.dtype)


def _mm(a_ops, products, extras, layer, epilogue, n_out, out_dtype, tm, tn, name):
    m = a_ops[0].shape[1]
    tm = min(tm, m)
    tn = min(tn, n_out)
    assert m % tm == 0 and n_out % tn == 0
    in_specs, args, scratch = [], [], []
    for a in a_ops:
        la = layer if a.shape[0] > 1 else 0
        in_specs.append(pl.BlockSpec((None, tm, a.shape[2]), lambda j, i, la=la: (la, i, 0)))
        args.append(a)
    for _, b, off in products:
        in_specs.append(pl.BlockSpec((None, b.shape[1], tn), lambda j, i, off=off: (layer, 0, j + off)))
        args.append(b)
        scratch.append(pltpu.VMEM((b.shape[1], tn), BF16))
    for e, off in extras:
        le = layer if e.shape[0] > 1 else 0
        in_specs.append(pl.BlockSpec((None, 1, tn), lambda j, i, off=off, le=le: (le, 0, j + off)))
        args.append(e)
    kern = functools.partial(_mm_kernel, a_of=tuple(ai for ai, _, _ in products), n_a=len(a_ops),
                             n_extra=len(extras), epilogue=epilogue)
    return pl.pallas_call(
        kern,
        out_shape=jax.ShapeDtypeStruct((m, n_out), out_dtype),
        grid=(n_out // tn, m // tm),
        in_specs=in_specs,
        out_specs=pl.BlockSpec((tm, tn), lambda j, i: (i, j)),
        scratch_shapes=scratch,
        compiler_params=_params(("arbitrary", "arbitrary")),
        name=name,
    )(*args)


def _ep_colscale(accs, extras):
    return accs[0] * extras[0]


def _ep_merge(accs, extras):
    o_r, o_s, z_r, z_s = accs
    b_r, b_s = extras
    return jax.nn.sigmoid(z_r + b_r) * o_r + jax.nn.sigmoid(z_s + b_s) * o_s


def _ep_ple(accs, extras):
    return jax.nn.sigmoid(accs[0]) * accs[1]


def _ln_body(y, g, b):
    mu = jnp.mean(y, axis=-1, keepdims=True)
    yc = y - mu
    var = jnp.mean(yc * yc, axis=-1, keepdims=True)
    return yc * lax.rsqrt(var + LN_EPS) * g + b


def _ln_kernel(*refs, n_add):
    x_ref = refs[0]
    add_refs = refs[1:1 + n_add]
    g_ref, b_ref, o_ref, ob_ref = refs[1 + n_add:]
    y = DEEPNORM_ALPHA * x_ref[...]
    for a in add_refs:
        y = y + a[...].astype(F32)
    out = _ln_body(y, g_ref[...], b_ref[...])
    o_ref[...] = out
    ob_ref[...] = out.astype(BF16)


def _ln(x, adds, g, b, name):
    n, d = x.shape
    tm = min(LN_ROWS, n)
    tile = pl.BlockSpec((tm, d), lambda i: (i, 0))
    row = pl.BlockSpec((1, d), lambda i: (0, 0))
    return pl.pallas_call(
        functools.partial(_ln_kernel, n_add=len(adds)),
        out_shape=(jax.ShapeDtypeStruct((n, d), F32), jax.ShapeDtypeStruct((n, d), BF16)),
        grid=(n // tm,),
        in_specs=[tile] * (1 + len(adds)) + [row, row],
        out_specs=(tile, tile),
        compiler_params=_params(("arbitrary",)),
        name=name,
    )(x, *adds, g.reshape(1, d), b.reshape(1, d))


def _proj_ln_kernel(a_ref, w_ref, x_ref, g_ref, b_ref, o_ref, ob_ref, w_scr):
    @pl.when(pl.program_id(0) == 0)
    def _():
        w_scr[...] = w_ref[...].astype(BF16)

    mix = jnp.dot(a_ref[...], w_scr[...], preferred_element_type=F32)
    out = _ln_body(DEEPNORM_ALPHA * x_ref[...] + mix, g_ref[...], b_ref[...])
    o_ref[...] = out
    ob_ref[...] = out.astype(BF16)


def _proj_ln(a, w, layer, x, g, b, name):
    n, d = x.shape
    k = a.shape[1]
    tm = min(PROJ_LN_ROWS, n)
    tile = pl.BlockSpec((tm, d), lambda i: (i, 0))
    row = pl.BlockSpec((1, d), lambda i: (0, 0))
    return pl.pallas_call(
        _proj_ln_kernel,
        out_shape=(jax.ShapeDtypeStruct((n, d), F32), jax.ShapeDtypeStruct((n, d), BF16)),
        grid=(n // tm,),
        in_specs=[pl.BlockSpec((tm, k), lambda i: (i, 0)),
                  pl.BlockSpec((None, k, d), lambda i: (layer, 0, 0), pipeline_mode=pl.Buffered(1)),
                  tile, row, row],
        out_specs=(tile, tile),
        scratch_shapes=[pltpu.VMEM((k, d), BF16)],
        compiler_params=_params(("arbitrary",)),
        name=name,
    )(a, w, x, g.reshape(1, d), b.reshape(1, d))


def _dot_nt(a, b):
    return lax.dot_general(a, b, (((1,), (1,)), ((), ())), preferred_element_type=F32)


def _dot_tn(a, b):
    return lax.dot_general(a, b, (((0,), (0,)), ((), ())), preferred_element_type=F32)


def _ret_kernel(q_ref, k_ref, v_ref, g_ref, cos_ref, sin_ref, decay_ref, xi_ref, zeta_ref,
                o_ref, r_scr, *, chunk_decay):
    d = HEAD_DIM

    @pl.when(pl.program_id(1) == 0)
    def _():
        r_scr[...] = jnp.zeros_like(r_scr)

    cos = cos_ref[...]
    sin = sin_ref[...]

    def rot(t):
        return t * cos + pltpu.roll(t, d // 2, 1) * sin

    heads = [slice(h * d, (h + 1) * d) for h in range(RET_HEADS)]
    qbs, kbs, kzs = [], [], []
    for h, sl in enumerate(heads):
        kr = rot(k_ref[:, sl].astype(F32))
        qbs.append(rot(q_ref[:, sl].astype(F32)).astype(BF16))
        kbs.append(kr.astype(BF16))
        kzs.append((kr * zeta_ref[h]).astype(BF16))
    inners = [_dot_nt(qb, kb) for qb, kb in zip(qbs, kbs)]
    crosses = [jnp.dot(qb, r_scr[h].astype(BF16), preferred_element_type=F32) for h, qb in enumerate(qbs)]
    kvs = [_dot_tn(kz, v_ref[:, sl]) for kz, sl in zip(kzs, heads)]
    for h in range(RET_HEADS):
        r_scr[h] = r_scr[h] * chunk_decay[h] + kvs[h]
    pbs = [(inner * decay_ref[h]).astype(BF16) for h, inner in enumerate(inners)]
    outs = [jnp.dot(pb, v_ref[:, sl], preferred_element_type=F32) for pb, sl in zip(pbs, heads)]
    for h, sl in enumerate(heads):
        o = outs[h] + crosses[h] * xi_ref[h]
        mu = jnp.mean(o, axis=-1, keepdims=True)
        oc = o - mu
        var = jnp.mean(oc * oc, axis=-1, keepdims=True)
        g = g_ref[:, sl].astype(F32)
        o_ref[:, sl] = (oc * lax.rsqrt(var + LN_EPS) * (g * jax.nn.sigmoid(g))).astype(o_ref.dtype)


def _retention(h, batch, seq):
    n = h.shape[0]
    c = RET_CHUNK
    d = HEAD_DIM
    w = RET_HEADS * d
    nc = seq // c
    half = d // 2
    pos = jnp.arange(seq, dtype=F32)
    inv = ROPE_BASE ** (-jnp.arange(half, dtype=F32) / half)
    ang = pos[:, None] * inv[None, :]
    cos = jnp.concatenate([jnp.cos(ang), jnp.cos(ang)], axis=1)
    sin = jnp.concatenate([-jnp.sin(ang), jnp.sin(ang)], axis=1)
    gamma = 1.0 - jnp.exp2(-5.0 - jnp.arange(RET_HEADS, dtype=F32))
    lg = jnp.log(gamma)
    idx = jnp.arange(c, dtype=F32)
    diff = idx[:, None] - idx[None, :]
    causal = diff >= 0
    decay = jnp.where(causal[None], jnp.exp(jnp.where(causal, diff, 0.0)[None] * lg[:, None, None]), 0.0)
    xi = jnp.broadcast_to(jnp.exp((idx + 1.0)[None, :] * lg[:, None])[:, :, None], (RET_HEADS, c, d))
    zeta = jnp.broadcast_to(jnp.exp((c - 1.0 - idx)[None, :] * lg[:, None])[:, :, None], (RET_HEADS, c, d))
    chunk_decay = tuple(math.exp(c * math.log(1.0 - 2.0 ** (-5.0 - hh))) for hh in range(RET_HEADS))

    def col(j):
        return pl.BlockSpec((c, w), lambda b, t, j=j: (b * nc + t, j))

    tab = pl.BlockSpec((c, d), lambda b, t: (t, 0))
    hconst = pl.BlockSpec((RET_HEADS, c, d), lambda b, t: (0, 0, 0))
    return pl.pallas_call(
        functools.partial(_ret_kernel, chunk_decay=chunk_decay),
        out_shape=jax.ShapeDtypeStruct((n, w), BF16),
        grid=(batch, nc),
        in_specs=[col(0), col(1), col(2), col(3), tab, tab,
                  pl.BlockSpec((RET_HEADS, c, c), lambda b, t: (0, 0, 0)), hconst, hconst],
        out_specs=pl.BlockSpec((c, w), lambda b, t: (b * nc + t, 0)),
        scratch_shapes=[pltpu.VMEM((RET_HEADS, d, d), F32)],
        compiler_params=_params(("arbitrary", "arbitrary")),
        name="retention",
    )(h, h, h, h, cos, sin, decay, xi, zeta)


def _sb_kernel(q_ref, k_ref, v_ref, u_ref, o_ref, acc, carry):
    blk = SB_BLOCK
    d = HEAD_DIM
    qb = pl.program_id(1)
    acc[...] = jnp.zeros_like(acc)
    carry[...] = jnp.zeros_like(carry)
    row = lax.broadcasted_iota(jnp.int32, (blk, blk), 0)
    col = lax.broadcasted_iota(jnp.int32, (blk, blk), 1)
    heads = [slice(hh * d, (hh + 1) * d) for hh in range(SB_HEADS)]
    sign = jnp.uint32(0x80000000)

    def sweep(kb, nblk, mask):
        keys = pl.ds(pl.multiple_of(kb * blk, blk), nblk * blk)
        zs = [_dot_nt(q_ref[:, sl], k_ref[keys, sl]) for sl in heads]
        log_betas, hilos = [], []
        for z in zs:
            neg_abs = lax.bitcast_convert_type(lax.bitcast_convert_type(z, jnp.uint32) | sign, F32)
            log_beta = jnp.minimum(z, 0.0) - jnp.log2(1.0 + jnp.exp2(neg_abs))
            log_fail = log_beta - z
            if mask is not None:
                log_fail = jnp.where(mask, log_fail, 0.0)
            hi = log_fail.astype(BF16)
            lo = (log_fail - hi.astype(F32)).astype(BF16)
            log_betas.append(log_beta)
            for c in range(nblk):
                cols = slice(c * blk, (c + 1) * blk)
                hilos.append(jnp.concatenate([hi[:, cols], lo[:, cols]], axis=1))
        s_all = jnp.dot(jnp.concatenate(hilos, axis=0), u_ref[...], preferred_element_type=F32)
        ws = []
        for hh, sl in enumerate(heads):
            run = carry[:, sl]
            parts = [None] * nblk
            for c in reversed(range(nblk)):
                s = s_all[(hh * nblk + c) * blk:(hh * nblk + c + 1) * blk]
                parts[c] = log_betas[hh][:, c * blk:(c + 1) * blk] + run + s[:, :blk]
                run = run + s[:, blk:]
            carry[:, sl] = run
            w = jnp.exp2(parts[0] if nblk == 1 else jnp.concatenate(parts, axis=1))
            if mask is not None:
                w = jnp.where(mask, w, 0.0)
            ws.append(w.astype(BF16))
        for w, sl in zip(ws, heads):
            acc[:, sl] += jnp.dot(w, v_ref[keys, sl], preferred_element_type=F32)

    sweep(qb, 1, col < row)

    @pl.when(qb % 2 == 1)
    def _():
        sweep(qb - 1, 1, None)

    def body(i, c):
        sweep((qb // 2 - 1 - i) * 2, 2, None)
        return c

    lax.fori_loop(0, qb // 2, body, 0)
    o_ref[...] = acc[...].astype(o_ref.dtype)


def _stick_breaking(h, batch, seq, col0):
    n = h.shape[0]
    blk = SB_BLOCK
    w = SB_HEADS * HEAD_DIM
    nq = seq // blk
    idx = jnp.arange(blk)
    tri = (idx[:, None] > idx[None, :]).astype(BF16)
    half = jnp.concatenate([tri, jnp.ones((blk, blk), BF16)], axis=1)
    u = jnp.concatenate([half, half], axis=0)
    return pl.pallas_call(
        _sb_kernel,
        out_shape=jax.ShapeDtypeStruct((n, w), BF16),
        grid=(batch, nq),
        in_specs=[pl.BlockSpec((blk, w), lambda b, t: (b * nq + t, col0)),
                  pl.BlockSpec((seq, w), lambda b, t: (b, col0 + 1)),
                  pl.BlockSpec((seq, w), lambda b, t: (b, col0 + 2)),
                  pl.BlockSpec((2 * blk, 2 * blk), lambda b, t: (0, 0))],
        out_specs=pl.BlockSpec((blk, w), lambda b, t: (b * nq + t, 0)),
        scratch_shapes=[pltpu.VMEM((blk, w), F32), pltpu.VMEM((blk, w), F32)],
        compiler_params=_params(("arbitrary", "arbitrary")),
        name="stick_breaking",
    )(h, h, h, u)


def _ffn_kernel(be_ref, ns_ref, x_ref, w1_ref, w3_ref, w2_ref, y_hbm, acc, sem, *, all_rows):
    i = pl.program_id(0)
    j = pl.program_id(1)
    tm = x_ref.shape[0]
    pair = 2 * FFN_SUB
    nsub = ns_ref[i]

    @pl.when(j == 0)
    def _():
        acc[...] = jnp.zeros_like(acc)

    def chunk(start, size):
        rows = pl.ds(start, size)
        x = x_ref[rows, :]
        a = jnp.dot(x, w1_ref[...].astype(BF16), preferred_element_type=F32)
        b = jnp.dot(x, w3_ref[...].astype(BF16), preferred_element_type=F32)
        hmid = (a * jax.nn.sigmoid(a) * b).astype(BF16)
        acc[rows, :] += jnp.dot(hmid, w2_ref[...].astype(BF16), preferred_element_type=F32)

    if all_rows:
        for s in range(tm // pair):
            chunk(s * pair, pair)
    else:
        quad = 2 * pair

        def quad_body(s, carry):
            chunk(pl.multiple_of(s * quad, quad), quad)
            return carry

        lax.fori_loop(0, nsub // 4, quad_body, 0)

        @pl.when(nsub % 4 >= 2)
        def _():
            chunk(pl.multiple_of((nsub // 4) * quad, pair), pair)

        @pl.when(nsub % 2 == 1)
        def _():
            chunk(pl.multiple_of((nsub - 1) * FFN_SUB, FFN_SUB), FFN_SUB)

    @pl.when(j == pl.num_programs(1) - 1)
    def _():
        out = pltpu.make_async_copy(acc, y_hbm.at[pl.ds(pl.multiple_of(i * tm, tm), tm), :], sem)
        out.start()
        out.wait()


def _ffn(x, w1, w3, w2, block_e, block_nsub, name, all_rows=False):
    p, d = x.shape
    f = w1.shape[2]
    tm = min(FFN_ROWS, p)
    tf = min(FFN_TF, f)
    assert p % tm == 0 and f % tf == 0 and tm % FFN_SUB == 0
    nj = f // tf

    def jj(i, j, ns):
        return jnp.where(ns[i] > 0, j, nj - 1)

    return pl.pallas_call(
        functools.partial(_ffn_kernel, all_rows=all_rows),
        out_shape=jax.ShapeDtypeStruct((p, d), F32),
        grid_spec=pltpu.PrefetchScalarGridSpec(
            num_scalar_prefetch=2,
            grid=(p // tm, nj),
            in_specs=[pl.BlockSpec((tm, d), lambda i, j, be, ns: (i, 0)),
                      pl.BlockSpec((None, d, tf), lambda i, j, be, ns: (be[i], 0, jj(i, j, ns))),
                      pl.BlockSpec((None, d, tf), lambda i, j, be, ns: (be[i], 0, jj(i, j, ns))),
                      pl.BlockSpec((None, tf, d), lambda i, j, be, ns: (be[i], jj(i, j, ns), 0))],
            out_specs=pl.BlockSpec(memory_space=pl.ANY),
            scratch_shapes=[pltpu.VMEM((tm, d), F32), pltpu.SemaphoreType.DMA(())]),
        compiler_params=_params(("arbitrary", "arbitrary")),
        name=name,
    )(block_e, block_nsub, x, w1, w3, w2)


def _route_kernel(x_ref, w_ref, tri_ref, meta_ref, gate_ref, cnt_ref, run, *, n_experts):
    @pl.when(pl.program_id(0) == 0)
    def _():
        run[...] = jnp.zeros_like(run)

    x = x_ref[...]
    w = w_ref[...]
    xh = x.astype(BF16)
    wh = w.astype(BF16)
    xl = (x - xh.astype(F32)).astype(BF16)
    wl = (w - wh.astype(F32)).astype(BF16)
    logits = (jnp.dot(xh, wh, preferred_element_type=F32) + jnp.dot(xl, wh, preferred_element_type=F32)
              + jnp.dot(xh, wl, preferred_element_type=F32))
    lane_i = lax.broadcasted_iota(jnp.int32, logits.shape, 1)
    lane = lane_i.astype(F32)
    neg = jnp.float32(-jnp.inf)
    logits = jnp.where(lane_i < n_experts, logits, neg)
    m1 = jnp.max(logits, axis=-1, keepdims=True)
    i1 = jnp.min(jnp.where(logits == m1, lane, float(LANES)), axis=-1, keepdims=True)
    rest = jnp.where(lane == i1, neg, logits)
    m2 = jnp.max(rest, axis=-1, keepdims=True)
    i2 = jnp.min(jnp.where(rest == m2, lane, float(LANES)), axis=-1, keepdims=True)
    e2 = jnp.exp(m2 - m1)
    den = 1.0 + e2
    pick1 = jnp.where(lane == i1, 1.0, 0.0)
    pick2 = jnp.where(lane == i2, 1.0, 0.0)
    picks = pick1 + pick2
    before = run[...] + jnp.dot(tri_ref[...], picks.astype(BF16), preferred_element_type=F32)
    r1 = jnp.sum(before * pick1, axis=-1, keepdims=True)
    r2 = jnp.sum(before * pick2, axis=-1, keepdims=True)
    run[...] += jnp.sum(picks, axis=0, keepdims=True)
    cnt_ref[...] = run[...].astype(jnp.int32)
    meta = jnp.where(lane_i == 0, i1, jnp.where(lane_i == 1, i2, jnp.where(
        lane_i == 2, r1, jnp.where(lane_i == 3, r2, 0.0))))
    meta_ref[...] = meta.T[:META_ROWS].astype(jnp.int32)
    gate_ref[...] = jnp.where(lane_i == 0, 1.0 / den, jnp.where(lane_i == 1, e2 / den, 0.0))


def _route(x, w_router):
    n, d = x.shape
    e = w_router.shape[1]
    tm = min(ROUTE_ROWS, n)
    wpad = jnp.zeros((d, LANES), F32).at[:, :e].set(w_router)
    t = jnp.arange(tm)
    tri = (t[:, None] > t[None, :]).astype(BF16)
    meta, gate, cnt = pl.pallas_call(
        functools.partial(_route_kernel, n_experts=e),
        out_shape=(jax.ShapeDtypeStruct((META_ROWS, n), jnp.int32), jax.ShapeDtypeStruct((n, LANES), F32),
                   jax.ShapeDtypeStruct((1, LANES), jnp.int32)),
        grid=(n // tm,),
        in_specs=[pl.BlockSpec((tm, d), lambda i: (i, 0)), pl.BlockSpec((d, LANES), lambda i: (0, 0)),
                  pl.BlockSpec((tm, tm), lambda i: (0, 0))],
        out_specs=(pl.BlockSpec((META_ROWS, tm), lambda i: (0, i)), pl.BlockSpec((tm, LANES), lambda i: (i, 0)),
                   pl.BlockSpec((1, LANES), lambda i: (0, 0))),
        scratch_shapes=[pltpu.VMEM((1, LANES), F32)],
        compiler_params=_params(("arbitrary",)),
        name="route_top2",
    )(x, wpad, tri)
    return meta, gate, cnt[0, :e]


def _gather_kernel(tok_ref, ns_ref, x_hbm, o_ref, buf, sem):
    s = pl.program_id(0)
    per = FFN_ROWS // GATHER_ROWS

    def nonempty(step):
        return (step % per) < ns_ref[step // per]

    def issue(step):
        slot = step % 2
        base = step * GATHER_ROWS

        def row(r, carry):
            t = tok_ref[base + r]
            pltpu.make_async_copy(x_hbm.at[pl.ds(t, 1), :], buf.at[slot, pl.ds(r, 1), :], sem.at[slot]).start()
            return carry

        lax.fori_loop(0, GATHER_ROWS, row, 0, unroll=8)

    @pl.when(jnp.logical_and(s == 0, nonempty(0)))
    def _():
        issue(0)

    nxt = jnp.minimum(s + 1, pl.num_programs(0) - 1)

    @pl.when(jnp.logical_and(s + 1 < pl.num_programs(0), nonempty(nxt)))
    def _():
        issue(nxt)

    @pl.when(nonempty(s))
    def _():
        slot = s % 2
        pltpu.make_async_copy(x_hbm.at[pl.ds(0, GATHER_ROWS), :], buf.at[slot], sem.at[slot]).wait()
        o_ref[...] = buf[slot].astype(o_ref.dtype)

    @pl.when(jnp.logical_not(nonempty(s)))
    def _():
        o_ref[...] = jnp.zeros_like(o_ref)


def _gather_rows(x, slot_tok, block_nsub, p):
    n, d = x.shape
    return pl.pallas_call(
        _gather_kernel,
        out_shape=jax.ShapeDtypeStruct((p, d), BF16),
        grid_spec=pltpu.PrefetchScalarGridSpec(
            num_scalar_prefetch=2,
            grid=(p // GATHER_ROWS,),
            in_specs=[pl.BlockSpec(memory_space=pl.ANY)],
            out_specs=pl.BlockSpec((GATHER_ROWS, d), lambda s, tok, ns: (s, 0)),
            scratch_shapes=[pltpu.VMEM((2, GATHER_ROWS, d), F32), pltpu.SemaphoreType.DMA((2,))]),
        compiler_params=_params(("arbitrary",)),
        name="moe_gather",
    )(slot_tok, block_nsub, x)


def _combine_kernel(p0_ref, p1_ref, y_hbm, x_ref, ple_ref, gate_ref, lg_ref, lb_ref,
                    o_ref, ob_ref, buf, sem):
    i = pl.program_id(0)
    tm = x_ref.shape[0]

    def issue(tile):
        slot = tile % 2
        base = tile * tm

        def row(r, carry):
            pltpu.make_async_copy(y_hbm.at[pl.ds(p0_ref[base + r], 1), :], buf.at[slot, 0, pl.ds(r, 1), :],
                                  sem.at[slot]).start()
            pltpu.make_async_copy(y_hbm.at[pl.ds(p1_ref[base + r], 1), :], buf.at[slot, 1, pl.ds(r, 1), :],
                                  sem.at[slot]).start()
            return carry

        lax.fori_loop(0, tm, row, 0, unroll=4)

    @pl.when(i == 0)
    def _():
        issue(0)

    @pl.when(i + 1 < pl.num_programs(0))
    def _():
        issue(i + 1)

    slot = i % 2
    pltpu.make_async_copy(y_hbm.at[pl.ds(0, tm), :], buf.at[slot, 0], sem.at[slot]).wait()
    pltpu.make_async_copy(y_hbm.at[pl.ds(0, tm), :], buf.at[slot, 1], sem.at[slot]).wait()
    gate = gate_ref[...]
    f = gate[:, 0:1] * buf[slot, 0] + gate[:, 1:2] * buf[slot, 1]
    out = _ln_body(DEEPNORM_ALPHA * x_ref[...] + f + ple_ref[...].astype(F32), lg_ref[...], lb_ref[...])
    o_ref[...] = out
    ob_ref[...] = out.astype(BF16)


def _combine_ln(y, pos0, pos1, gate, x, ple, ln_g, ln_b):
    n, d = x.shape
    tm = min(LN_ROWS, n)
    tile = lambda: pl.BlockSpec((tm, d), lambda i, a, b: (i, 0))
    row = lambda: pl.BlockSpec((1, d), lambda i, a, b: (0, 0))
    return pl.pallas_call(
        _combine_kernel,
        out_shape=(jax.ShapeDtypeStruct((n, d), F32), jax.ShapeDtypeStruct((n, d), BF16)),
        grid_spec=pltpu.PrefetchScalarGridSpec(
            num_scalar_prefetch=2,
            grid=(n // tm,),
            in_specs=[pl.BlockSpec(memory_space=pl.ANY), tile(), tile(),
                      pl.BlockSpec((tm, LANES), lambda i, a, b: (i, 0)), row(), row()],
            out_specs=(tile(), tile()),
            scratch_shapes=[pltpu.VMEM((2, 2, tm, d), F32), pltpu.SemaphoreType.DMA((2,))]),
        compiler_params=_params(("arbitrary",)),
        name="moe_combine_ln",
    )(pos0, pos1, y, x, ple, gate, ln_g.reshape(1, d), ln_b.reshape(1, d))


def _moe_plan(meta, counts, n_experts):
    n = meta.shape[1]
    nk = n * TOP_K
    nblk = nk // FFN_ROWS + n_experts
    sub_per_blk = FFN_ROWS // FFN_SUB
    expert = meta[:TOP_K]
    rank = meta[TOP_K:2 * TOP_K]
    nsub_e = (counts + FFN_SUB - 1) // FFN_SUB
    nblk_e = (nsub_e + sub_per_blk - 1) // sub_per_blk
    rows_e = jnp.maximum((nsub_e + nblk_e - 1) // jnp.maximum(nblk_e, 1), 1) * FFN_SUB
    blk_end = jnp.cumsum(nblk_e)
    blk_off = blk_end - nblk_e
    rpb = rows_e[expert]
    pos = (blk_off[expert] + rank // rpb) * FFN_ROWS + rank % rpb
    blk = jnp.arange(nblk, dtype=jnp.int32)
    used = blk < blk_end[-1]
    block_e = jnp.clip(jnp.searchsorted(blk_end, blk, side="right"), 0, n_experts - 1).astype(jnp.int32)
    block_e = jnp.where(used, block_e, block_e[jnp.maximum(blk_end[-1] - 1, 0)])
    rows = jnp.clip(counts[block_e] - (blk - blk_off[block_e]) * rows_e[block_e], 0, rows_e[block_e])
    rows = jnp.where(used, rows, 0)
    block_nsub = ((rows + FFN_SUB - 1) // FFN_SUB).astype(jnp.int32)
    tok = jnp.tile(jnp.arange(n, dtype=jnp.int32), TOP_K)
    slot_tok = jnp.zeros((nblk * FFN_ROWS,), jnp.int32).at[pos.reshape(nk)].set(tok)
    return pos.astype(jnp.int32), slot_tok, block_e, block_nsub, nblk * FFN_ROWS


def _token_mixer(xb, layer, w_in, w_br_ret, w_br_sb, w_gate, b_gate, batch, seq):
    d = xb.shape[1]
    in_width = w_in.shape[2]
    rw = RET_HEADS * HEAD_DIM
    sw = SB_HEADS * HEAD_DIM
    col_scale = jnp.ones((in_width,), F32)
    col_scale = col_scale.at[rw:2 * rw].set(HEAD_DIM ** -0.5)
    col_scale = col_scale.at[4 * rw:4 * rw + sw].set(HEAD_DIM ** -0.5 * math.log2(math.e))
    h = _mm([xb[None]], [(0, w_in, 0)], [(col_scale.reshape(1, 1, in_width), 0)], layer, _ep_colscale,
            in_width, BF16, 1024, 1024, "in_proj")
    o_ret = _retention(h, batch, seq)
    o_sb = _stick_breaking(h, batch, seq, (4 * RET_HEADS) // SB_HEADS)
    tn = min(512, d)
    bg = b_gate.reshape(b_gate.shape[0], 1, 2 * d)
    return _mm([o_ret[None], o_sb[None], xb[None]],
               [(0, w_br_ret, 0), (1, w_br_sb, 0), (2, w_gate, 0), (2, w_gate, d // tn)],
               [(bg, 0), (bg, d // tn)], layer, _ep_merge, d, BF16, 512, tn, "branch_merge")


def kernel(x, p, w_in, w_br_ret, w_br_sb, w_gate, b_gate, w_o, ln1_g, ln1_b,
           ffn_w1, ffn_w3, ffn_w2, moe_router, moe_w1, moe_w3, moe_w2,
           ple_w, ple_gate_w, ln2_g, ln2_b):
    batch, seq, d = x.shape
    n = batch * seq
    depth = w_in.shape[0]
    n_experts = moe_router.shape[2]
    xf = x.reshape(n, d)
    xb = xf.astype(BF16)
    pf = p.reshape(depth, n, p.shape[3])
    ew1 = moe_w1.reshape((-1,) + moe_w1.shape[2:])
    ew3 = moe_w3.reshape((-1,) + moe_w3.shape[2:])
    ew2 = moe_w2.reshape((-1,) + moe_w2.shape[2:])
    for i in range(depth):
        merged = _token_mixer(xb, i, w_in, w_br_ret, w_br_sb, w_gate, b_gate, batch, seq)
        xf, xb = _proj_ln(merged, w_o, i, xf, ln1_g[i], ln1_b[i], "out_proj_ln")
        ple = _mm([xb[None], pf], [(0, ple_gate_w, 0), (1, ple_w, 0)], [], i, _ep_ple, d, BF16, 1024, 1024, "ple")
        if i % 2 == 0:
            nblk = n // min(FFN_ROWS, n)
            f = _ffn(xb, ffn_w1, ffn_w3, ffn_w2, jnp.full((nblk,), i // 2, jnp.int32),
                     jnp.full((nblk,), FFN_ROWS // FFN_SUB, jnp.int32), "dense_swiglu", all_rows=True)
            xf, xb = _ln(xf, [f, ple], ln2_g[i], ln2_b[i], "ln_ffn")
        else:
            meta, gate, counts = _route(xf, moe_router[i // 2])
            pos, slot_tok, block_e, block_nsub, slots = _moe_plan(meta, counts, n_experts)
            xs = _gather_rows(xf, slot_tok, block_nsub, slots)
            y = _ffn(xs, ew1, ew3, ew2, block_e + (i // 2) * n_experts, block_nsub, "expert_swiglu")
            xf, xb = _combine_ln(y, pos[0], pos[1], gate, xf, ple, ln2_g[i], ln2_b[i])
    return xf.reshape(batch, seq, d)
```

```python
import functools
import math

import jax
import jax.numpy as jnp
from jax import lax
from jax.experimental import pallas as pl
from jax.experimental.pallas import tpu as pltpu

F32 = jnp.float32
BF16 = jnp.bfloat16

RET_HEADS = 8
SB_HEADS = 8
HEAD_DIM = 128
RET_CHUNK = 128
ROPE_BASE = 10000.0
TOP_K = 2
DEPTH = 2
DEEPNORM_ALPHA = (2 * DEPTH) ** 0.25
LN_EPS = 1e-5

V7X_VMEM_BYTES = 64 * 1024 * 1024
VMEM_LIMIT = V7X_VMEM_BYTES - 8 * 1024 * 1024
LANES = 128

SB_BLOCK = 128
PROJ_LN_ROWS = 512
FFN_ROWS = 2048
FFN_SUB = 256
FFN_TF = 256
LN_ROWS = 256
ROUTE_ROWS = 512
META_ROWS = 8
GATHER_ROWS = FFN_SUB


def _params(sem):
    return pltpu.CompilerParams(dimension_semantics=sem, vmem_limit_bytes=VMEM_LIMIT)


def _mm_kernel(*refs, a_of, n_a, n_extra, epilogue):
    n_prod = len(a_of)
    a_refs = refs[:n_a]
    b_refs = refs[n_a:n_a + n_prod]
    e_refs = refs[n_a + n_prod:n_a + n_prod + n_extra]
    o_ref = refs[n_a + n_prod + n_extra]
    b_scr = refs[n_a + n_prod + n_extra + 1:]

    @pl.when(pl.program_id(1) == 0)
    def _():
        for b_ref, s in zip(b_refs, b_scr):
            s[...] = b_ref[...].astype(BF16)

    a_vals = [a[...].astype(BF16) for a in a_refs]
    accs = [jnp.dot(a_vals[ai], s[...], preferred_element_type=F32) for ai, s in zip(a_of, b_scr)]
    o_ref[...] = epilogue(accs, [e[...] for e in e_refs]).astype(o_ref.dtype)


def _mm(a_ops, products, extras, layer, epilogue, n_out, out_dtype, tm, tn, name):
    m = a_ops[0].shape[1]
    tm = min(tm, m)
    tn = min(tn, n_out)
    assert m % tm == 0 and n_out % tn == 0
    in_specs, args, scratch = [], [], []
    for a in a_ops:
        la = layer if a.shape[0] > 1 else 0
        in_specs.append(pl.BlockSpec((None, tm, a.shape[2]), lambda j, i, la=la: (la, i, 0)))
        args.append(a)
    for _, b, off in products:
        in_specs.append(pl.BlockSpec((None, b.shape[1], tn), lambda j, i, off=off: (layer, 0, j + off)))
        args.append(b)
        scratch.append(pltpu.VMEM((b.shape[1], tn), BF16))
    for e, off in extras:
        le = layer if e.shape[0] > 1 else 0
        in_specs.append(pl.BlockSpec((None, 1, tn), lambda j, i, off=off, le=le: (le, 0, j + off)))
        args.append(e)
    kern = functools.partial(_mm_kernel, a_of=tuple(ai for ai, _, _ in products), n_a=len(a_ops),
                             n_extra=len(extras), epilogue=epilogue)
    return pl.pallas_call(
        kern,
        out_shape=jax.ShapeDtypeStruct((m, n_out), out_dtype),
        grid=(n_out // tn, m // tm),
        in_specs=in_specs,
        out_specs=pl.BlockSpec((tm, tn), lambda j, i: (i, j)),
        scratch_shapes=scratch,
        compiler_params=_params(("arbitrary", "arbitrary")),
        name=name,
    )(*args)


def _ep_colscale(accs, extras):
    return accs[0] * extras[0]


def _ep_merge(accs, extras):
    o_r, o_s, z_r, z_s = accs
    b_r, b_s = extras
    return jax.nn.sigmoid(z_r + b_r) * o_r + jax.nn.sigmoid(z_s + b_s) * o_s


def _ep_ple(accs, extras):
    return jax.nn.sigmoid(accs[0]) * accs[1]


def _ln_body(y, g, b):
    mu = jnp.mean(y, axis=-1, keepdims=True)
    yc = y - mu
    var = jnp.mean(yc * yc, axis=-1, keepdims=True)
    return yc * lax.rsqrt(var + LN_EPS) * g + b


def _ln_kernel(*refs, n_add):
    x_ref = refs[0]
    add_refs = refs[1:1 + n_add]
    g_ref, b_ref, o_ref, ob_ref = refs[1 + n_add:]
    y = DEEPNORM_ALPHA * x_ref[...]
    for a in add_refs:
        y = y + a[...].astype(F32)
    out = _ln_body(y, g_ref[...], b_ref[...])
    o_ref[...] = out
    ob_ref[...] = out.astype(BF16)


def _ln(x, adds, g, b, name):
    n, d = x.shape
    tm = min(LN_ROWS, n)
    tile = pl.BlockSpec((tm, d), lambda i: (i, 0))
    row = pl.BlockSpec((1, d), lambda i: (0, 0))
    return pl.pallas_call(
        functools.partial(_ln_kernel, n_add=len(adds)),
        out_shape=(jax.ShapeDtypeStruct((n, d), F32), jax.ShapeDtypeStruct((n, d), BF16)),
        grid=(n // tm,),
        in_specs=[tile] * (1 + len(adds)) + [row, row],
        out_specs=(tile, tile),
        compiler_params=_params(("arbitrary",)),
        name=name,
    )(x, *adds, g.reshape(1, d), b.reshape(1, d))


def _proj_ln_kernel(a_ref, w_ref, x_ref, g_ref, b_ref, o_ref, ob_ref, w_scr):
    @pl.when(pl.program_id(0) == 0)
    def _():
        w_scr[...] = w_ref[...].astype(BF16)

    mix = jnp.dot(a_ref[...], w_scr[...], preferred_element_type=F32)
    out = _ln_body(DEEPNORM_ALPHA * x_ref[...] + mix, g_ref[...], b_ref[...])
    o_ref[...] = out
    ob_ref[...] = out.astype(BF16)


def _proj_ln(a, w, layer, x, g, b, name):
    n, d = x.shape
    k = a.shape[1]
    tm = min(PROJ_LN_ROWS, n)
    tile = pl.BlockSpec((tm, d), lambda i: (i, 0))
    row = pl.BlockSpec((1, d), lambda i: (0, 0))
    return pl.pallas_call(
        _proj_ln_kernel,
        out_shape=(jax.ShapeDtypeStruct((n, d), F32), jax.ShapeDtypeStruct((n, d), BF16)),
        grid=(n // tm,),
        in_specs=[pl.BlockSpec((tm, k), lambda i: (i, 0)),
                  pl.BlockSpec((None, k, d), lambda i: (layer, 0, 0), pipeline_mode=pl.Buffered(1)),
                  tile, row, row],
        out_specs=(tile, tile),
        scratch_shapes=[pltpu.VMEM((k, d), BF16)],
        compiler_params=_params(("arbitrary",)),
        name=name,
    )(a, w, x, g.reshape(1, d), b.reshape(1, d))


def _dot_nt(a, b):
    return lax.dot_general(a, b, (((1,), (1,)), ((), ())), preferred_element_type=F32)


def _dot_tn(a, b):
    return lax.dot_general(a, b, (((0,), (0,)), ((), ())), preferred_element_type=F32)


def _ret_kernel(q_ref, k_ref, v_ref, g_ref, cos_ref, sin_ref, decay_ref, xi_ref, zeta_ref,
                o_ref, r_scr, *, chunk_decay):
    d = HEAD_DIM

    @pl.when(pl.program_id(1) == 0)
    def _():
        r_scr[...] = jnp.zeros_like(r_scr)

    cos = cos_ref[...]
    sin = sin_ref[...]

    def rot(t):
        return t * cos + pltpu.roll(t, d // 2, 1) * sin

    heads = [slice(h * d, (h + 1) * d) for h in range(RET_HEADS)]
    qbs, kbs, kzs = [], [], []
    for h, sl in enumerate(heads):
        kr = rot(k_ref[:, sl].astype(F32))
        qbs.append(rot(q_ref[:, sl].astype(F32)).astype(BF16))
        kbs.append(kr.astype(BF16))
        kzs.append((kr * zeta_ref[h]).astype(BF16))
    inners = [_dot_nt(qb, kb) for qb, kb in zip(qbs, kbs)]
    crosses = [jnp.dot(qb, r_scr[h].astype(BF16), preferred_element_type=F32) for h, qb in enumerate(qbs)]
    kvs = [_dot_tn(kz, v_ref[:, sl]) for kz, sl in zip(kzs, heads)]
    for h in range(RET_HEADS):
        r_scr[h] = r_scr[h] * chunk_decay[h] + kvs[h]
    pbs = [(inner * decay_ref[h]).astype(BF16) for h, inner in enumerate(inners)]
    outs = [jnp.dot(pb, v_ref[:, sl], preferred_element_type=F32) for pb, sl in zip(pbs, heads)]
    for h, sl in enumerate(heads):
        o = outs[h] + crosses[h] * xi_ref[h]
        mu = jnp.mean(o, axis=-1, keepdims=True)
        oc = o - mu
        var = jnp.mean(oc * oc, axis=-1, keepdims=True)
        g = g_ref[:, sl].astype(F32)
        o_ref[:, sl] = (oc * lax.rsqrt(var + LN_EPS) * (g * jax.nn.sigmoid(g))).astype(o_ref.dtype)


def _retention(h, batch, seq):
    n = h.shape[0]
    c = RET_CHUNK
    d = HEAD_DIM
    w = RET_HEADS * d
    nc = seq // c
    half = d // 2
    pos = jnp.arange(seq, dtype=F32)
    inv = ROPE_BASE ** (-jnp.arange(half, dtype=F32) / half)
    ang = pos[:, None] * inv[None, :]
    cos = jnp.concatenate([jnp.cos(ang), jnp.cos(ang)], axis=1)
    sin = jnp.concatenate([-jnp.sin(ang), jnp.sin(ang)], axis=1)
    gamma = 1.0 - jnp.exp2(-5.0 - jnp.arange(RET_HEADS, dtype=F32))
    lg = jnp.log(gamma)
    idx = jnp.arange(c, dtype=F32)
    diff = idx[:, None] - idx[None, :]
    causal = diff >= 0
    decay = jnp.where(causal[None], jnp.exp(jnp.where(causal, diff, 0.0)[None] * lg[:, None, None]), 0.0)
    xi = jnp.broadcast_to(jnp.exp((idx + 1.0)[None, :] * lg[:, None])[:, :, None], (RET_HEADS, c, d))
    zeta = jnp.broadcast_to(jnp.exp((c - 1.0 - idx)[None, :] * lg[:, None])[:, :, None], (RET_HEADS, c, d))
    chunk_decay = tuple(math.exp(c * math.log(1.0 - 2.0 ** (-5.0 - hh))) for hh in range(RET_HEADS))

    def col(j):
        return pl.BlockSpec((c, w), lambda b, t, j=j: (b * nc + t, j))

    tab = pl.BlockSpec((c, d), lambda b, t: (t, 0))
    hconst = pl.BlockSpec((RET_HEADS, c, d), lambda b, t: (0, 0, 0))
    return pl.pallas_call(
        functools.partial(_ret_kernel, chunk_decay=chunk_decay),
        out_shape=jax.ShapeDtypeStruct((n, w), BF16),
        grid=(batch, nc),
        in_specs=[col(0), col(1), col(2), col(3), tab, tab,
                  pl.BlockSpec((RET_HEADS, c, c), lambda b, t: (0, 0, 0)), hconst, hconst],
        out_specs=pl.BlockSpec((c, w), lambda b, t: (b * nc + t, 0)),
        scratch_shapes=[pltpu.VMEM((RET_HEADS, d, d), F32)],
        compiler_params=_params(("arbitrary", "arbitrary")),
        name="retention",
    )(h, h, h, h, cos, sin, decay, xi, zeta)


def _sb_kernel(q_ref, k_ref, v_ref, u_ref, o_ref, acc, carry):
    blk = SB_BLOCK
    d = HEAD_DIM
    qb = pl.program_id(1)
    acc[...] = jnp.zeros_like(acc)
    carry[...] = jnp.zeros_like(carry)
    row = lax.broadcasted_iota(jnp.int32, (blk, blk), 0)
    col = lax.broadcasted_iota(jnp.int32, (blk, blk), 1)
    heads = [slice(hh * d, (hh + 1) * d) for hh in range(SB_HEADS)]
    sign = jnp.uint32(0x80000000)

    def sweep(kb, nblk, mask):
        keys = pl.ds(pl.multiple_of(kb * blk, blk), nblk * blk)
        zs = [_dot_nt(q_ref[:, sl], k_ref[keys, sl]) for sl in heads]
        log_betas, hilos = [], []
        for z in zs:
            neg_abs = lax.bitcast_convert_type(lax.bitcast_convert_type(z, jnp.uint32) | sign, F32)
            log_beta = jnp.minimum(z, 0.0) - jnp.log2(1.0 + jnp.exp2(neg_abs))
            log_fail = log_beta - z
            if mask is not None:
                log_fail = jnp.where(mask, log_fail, 0.0)
            hi = log_fail.astype(BF16)
            lo = (log_fail - hi.astype(F32)).astype(BF16)
            log_betas.append(log_beta)
            for c in range(nblk):
                cols = slice(c * blk, (c + 1) * blk)
                hilos.append(jnp.concatenate([hi[:, cols], lo[:, cols]], axis=1))
        s_all = jnp.dot(jnp.concatenate(hilos, axis=0), u_ref[...], preferred_element_type=F32)
        ws = []
        for hh, sl in enumerate(heads):
            run = carry[:, sl]
            parts = [None] * nblk
            for c in reversed(range(nblk)):
                s = s_all[(hh * nblk + c) * blk:(hh * nblk + c + 1) * blk]
                parts[c] = log_betas[hh][:, c * blk:(c + 1) * blk] + run + s[:, :blk]
                run = run + s[:, blk:]
            carry[:, sl] = run
            w = jnp.exp2(parts[0] if nblk == 1 else jnp.concatenate(parts, axis=1))
            if mask is not None:
                w = jnp.where(mask, w, 0.0)
            ws.append(w.astype(BF16))
        for w, sl in zip(ws, heads):
            acc[:, sl] += jnp.dot(w, v_ref[keys, sl], preferred_element_type=F32)

    sweep(qb, 1, col < row)

    @pl.when(qb % 2 == 1)
    def _():
        sweep(qb - 1, 1, None)

    @pl.when((qb // 2) % 2 == 1)
    def _():
        sweep((qb // 4) * 4, 2, None)

    def body(i, c):
        sweep((qb // 4 - 1 - i) * 4, 4, None)
        return c

    lax.fori_loop(0, qb // 4, body, 0)
    o_ref[...] = acc[...].astype(o_ref.dtype)


def _stick_breaking(h, batch, seq, col0):
    n = h.shape[0]
    blk = SB_BLOCK
    w = SB_HEADS * HEAD_DIM
    nq = seq // blk
    idx = jnp.arange(blk)
    tri = (idx[:, None] > idx[None, :]).astype(BF16)
    half = jnp.concatenate([tri, jnp.ones((blk, blk), BF16)], axis=1)
    u = jnp.concatenate([half, half], axis=0)
    return pl.pallas_call(
        _sb_kernel,
        out_shape=jax.ShapeDtypeStruct((n, w), BF16),
        grid=(batch, nq),
        in_specs=[pl.BlockSpec((blk, w), lambda b, t: (b * nq + t, col0)),
                  pl.BlockSpec((seq, w), lambda b, t: (b, col0 + 1)),
                  pl.BlockSpec((seq, w), lambda b, t: (b, col0 + 2)),
                  pl.BlockSpec((2 * blk, 2 * blk), lambda b, t: (0, 0))],
        out_specs=pl.BlockSpec((blk, w), lambda b, t: (b * nq + t, 0)),
        scratch_shapes=[pltpu.VMEM((blk, w), F32), pltpu.VMEM((blk, w), F32)],
        compiler_params=_params(("arbitrary", "arbitrary")),
        name="stick_breaking",
    )(h, h, h, u)


def _ffn_kernel(be_ref, ns_ref, x_ref, w1_ref, w3_ref, w2_ref, y_hbm, acc, sem, *, all_rows):
    i = pl.program_id(0)
    j = pl.program_id(1)
    tm = x_ref.shape[0]
    pair = 2 * FFN_SUB
    nsub = ns_ref[i]

    @pl.when(j == 0)
    def _():
        acc[...] = jnp.zeros_like(acc)

    def chunk(start, size):
        rows = pl.ds(start, size)
        x = x_ref[rows, :]
        a = jnp.dot(x, w1_ref[...].astype(BF16), preferred_element_type=F32)
        b = jnp.dot(x, w3_ref[...].astype(BF16), preferred_element_type=F32)
        hmid = (a * jax.nn.sigmoid(a) * b).astype(BF16)
        acc[rows, :] += jnp.dot(hmid, w2_ref[...].astype(BF16), preferred_element_type=F32)

    if all_rows:
        for s in range(tm // pair):
            chunk(s * pair, pair)
    else:
        quad = 2 * pair

        def quad_body(s, carry):
            chunk(pl.multiple_of(s * quad, quad), quad)
            return carry

        lax.fori_loop(0, nsub // 4, quad_body, 0)

        @pl.when(nsub % 4 >= 2)
        def _():
            chunk(pl.multiple_of((nsub // 4) * quad, pair), pair)

        @pl.when(nsub % 2 == 1)
        def _():
            chunk(pl.multiple_of((nsub - 1) * FFN_SUB, FFN_SUB), FFN_SUB)

    @pl.when(j == pl.num_programs(1) - 1)
    def _():
        out = pltpu.make_async_copy(acc, y_hbm.at[pl.ds(pl.multiple_of(i * tm, tm), tm), :], sem)
        out.start()
        out.wait()


def _ffn(x, w1, w3, w2, block_e, block_nsub, name, all_rows=False):
    p, d = x.shape
    f = w1.shape[2]
    tm = min(FFN_ROWS, p)
    tf = min(FFN_TF, f)
    assert p % tm == 0 and f % tf == 0 and tm % FFN_SUB == 0
    nj = f // tf

    def jj(i, j, ns):
        return jnp.where(ns[i] > 0, j, nj - 1)

    return pl.pallas_call(
        functools.partial(_ffn_kernel, all_rows=all_rows),
        out_shape=jax.ShapeDtypeStruct((p, d), F32),
        grid_spec=pltpu.PrefetchScalarGridSpec(
            num_scalar_prefetch=2,
            grid=(p // tm, nj),
            in_specs=[pl.BlockSpec((tm, d), lambda i, j, be, ns: (i, 0)),
                      pl.BlockSpec((None, d, tf), lambda i, j, be, ns: (be[i], 0, jj(i, j, ns))),
                      pl.BlockSpec((None, d, tf), lambda i, j, be, ns: (be[i], 0, jj(i, j, ns))),
                      pl.BlockSpec((None, tf, d), lambda i, j, be, ns: (be[i], jj(i, j, ns), 0))],
            out_specs=pl.BlockSpec(memory_space=pl.ANY),
            scratch_shapes=[pltpu.VMEM((tm, d), F32), pltpu.SemaphoreType.DMA(())]),
        compiler_params=_params(("arbitrary", "arbitrary")),
        name=name,
    )(block_e, block_nsub, x, w1, w3, w2)


def _route_kernel(x_ref, w_ref, tri_ref, meta_ref, gate_ref, cnt_ref, run, *, n_experts):
    @pl.when(pl.program_id(0) == 0)
    def _():
        run[...] = jnp.zeros_like(run)

    x = x_ref[...]
    w = w_ref[...]
    xh = x.astype(BF16)
    wh = w.astype(BF16)
    xl = (x - xh.astype(F32)).astype(BF16)
    wl = (w - wh.astype(F32)).astype(BF16)
    logits = (jnp.dot(xh, wh, preferred_element_type=F32) + jnp.dot(xl, wh, preferred_element_type=F32)
              + jnp.dot(xh, wl, preferred_element_type=F32))
    lane_i = lax.broadcasted_iota(jnp.int32, logits.shape, 1)
    lane = lane_i.astype(F32)
    neg = jnp.float32(-jnp.inf)
    logits = jnp.where(lane_i < n_experts, logits, neg)
    m1 = jnp.max(logits, axis=-1, keepdims=True)
    i1 = jnp.min(jnp.where(logits == m1, lane, float(LANES)), axis=-1, keepdims=True)
    rest = jnp.where(lane == i1, neg, logits)
    m2 = jnp.max(rest, axis=-1, keepdims=True)
    i2 = jnp.min(jnp.where(rest == m2, lane, float(LANES)), axis=-1, keepdims=True)
    e2 = jnp.exp(m2 - m1)
    den = 1.0 + e2
    pick1 = jnp.where(lane == i1, 1.0, 0.0)
    pick2 = jnp.where(lane == i2, 1.0, 0.0)
    picks = pick1 + pick2
    before = run[...] + jnp.dot(tri_ref[...], picks.astype(BF16), preferred_element_type=F32)
    r1 = jnp.sum(before * pick1, axis=-1, keepdims=True)
    r2 = jnp.sum(before * pick2, axis=-1, keepdims=True)
    run[...] += jnp.sum(picks, axis=0, keepdims=True)
    cnt_ref[...] = run[...].astype(jnp.int32)
    meta = jnp.where(lane_i == 0, i1, jnp.where(lane_i == 1, i2, jnp.where(
        lane_i == 2, r1, jnp.where(lane_i == 3, r2, 0.0))))
    meta_ref[...] = meta.T[:META_ROWS].astype(jnp.int32)
    gate_ref[...] = jnp.where(lane_i == 0, 1.0 / den, jnp.where(lane_i == 1, e2 / den, 0.0))


def _route(x, w_router):
    n, d = x.shape
    e = w_router.shape[1]
    tm = min(ROUTE_ROWS, n)
    wpad = jnp.zeros((d, LANES), F32).at[:, :e].set(w_router)
    t = jnp.arange(tm)
    tri = (t[:, None] > t[None, :]).astype(BF16)
    meta, gate, cnt = pl.pallas_call(
        functools.partial(_route_kernel, n_experts=e),
        out_shape=(jax.ShapeDtypeStruct((META_ROWS, n), jnp.int32), jax.ShapeDtypeStruct((n, LANES), F32),
                   jax.ShapeDtypeStruct((1, LANES), jnp.int32)),
        grid=(n // tm,),
        in_specs=[pl.BlockSpec((tm, d), lambda i: (i, 0)), pl.BlockSpec((d, LANES), lambda i: (0, 0)),
                  pl.BlockSpec((tm, tm), lambda i: (0, 0))],
        out_specs=(pl.BlockSpec((META_ROWS, tm), lambda i: (0, i)), pl.BlockSpec((tm, LANES), lambda i: (i, 0)),
                   pl.BlockSpec((1, LANES), lambda i: (0, 0))),
        scratch_shapes=[pltpu.VMEM((1, LANES), F32)],
        compiler_params=_params(("arbitrary",)),
        name="route_top2",
    )(x, wpad, tri)
    return meta, gate, cnt[0, :e]


def _gather_kernel(tok_ref, ns_ref, x_hbm, o_ref, buf, sem):
    s = pl.program_id(0)
    per = FFN_ROWS // GATHER_ROWS

    def nonempty(step):
        return (step % per) < ns_ref[step // per]

    def issue(step):
        slot = step % 2
        base = step * GATHER_ROWS

        def row(r, carry):
            t = tok_ref[base + r]
            pltpu.make_async_copy(x_hbm.at[pl.ds(t, 1), :], buf.at[slot, pl.ds(r, 1), :], sem.at[slot]).start()
            return carry

        lax.fori_loop(0, GATHER_ROWS, row, 0, unroll=8)

    @pl.when(jnp.logical_and(s == 0, nonempty(0)))
    def _():
        issue(0)

    nxt = jnp.minimum(s + 1, pl.num_programs(0) - 1)

    @pl.when(jnp.logical_and(s + 1 < pl.num_programs(0), nonempty(nxt)))
    def _():
        issue(nxt)

    @pl.when(nonempty(s))
    def _():
        slot = s % 2
        pltpu.make_async_copy(x_hbm.at[pl.ds(0, GATHER_ROWS), :], buf.at[slot], sem.at[slot]).wait()
        o_ref[...] = buf[slot].astype(o_ref.dtype)

    @pl.when(jnp.logical_not(nonempty(s)))
    def _():
        o_ref[...] = jnp.zeros_like(o_ref)


def _gather_rows(x, slot_tok, block_nsub, p):
    n, d = x.shape
    return pl.pallas_call(
        _gather_kernel,
        out_shape=jax.ShapeDtypeStruct((p, d), BF16),
        grid_spec=pltpu.PrefetchScalarGridSpec(
            num_scalar_prefetch=2,
            grid=(p // GATHER_ROWS,),
            in_specs=[pl.BlockSpec(memory_space=pl.ANY)],
            out_specs=pl.BlockSpec((GATHER_ROWS, d), lambda s, tok, ns: (s, 0)),
            scratch_shapes=[pltpu.VMEM((2, GATHER_ROWS, d), F32), pltpu.SemaphoreType.DMA((2,))]),
        compiler_params=_params(("arbitrary",)),
        name="moe_gather",
    )(slot_tok, block_nsub, x)


def _combine_kernel(p0_ref, p1_ref, y_hbm, x_ref, ple_ref, gate_ref, lg_ref, lb_ref,
                    o_ref, ob_ref, buf, sem):
    i = pl.program_id(0)
    tm = x_ref.shape[0]

    def issue(tile):
        slot = tile % 2
        base = tile * tm

        def row(r, carry):
            pltpu.make_async_copy(y_hbm.at[pl.ds(p0_ref[base + r], 1), :], buf.at[slot, 0, pl.ds(r, 1), :],
                                  sem.at[slot]).start()
            pltpu.make_async_copy(y_hbm.at[pl.ds(p1_ref[base + r], 1), :], buf.at[slot, 1, pl.ds(r, 1), :],
                                  sem.at[slot]).start()
            return carry

        lax.fori_loop(0, tm, row, 0, unroll=4)

    @pl.when(i == 0)
    def _():
        issue(0)

    @pl.when(i + 1 < pl.num_programs(0))
    def _():
        issue(i + 1)

    slot = i % 2
    pltpu.make_async_copy(y_hbm.at[pl.ds(0, tm), :], buf.at[slot, 0], sem.at[slot]).wait()
    pltpu.make_async_copy(y_hbm.at[pl.ds(0, tm), :], buf.at[slot, 1], sem.at[slot]).wait()
    gate = gate_ref[...]
    f = gate[:, 0:1] * buf[slot, 0] + gate[:, 1:2] * buf[slot, 1]
    out = _ln_body(DEEPNORM_ALPHA * x_ref[...] + f + ple_ref[...].astype(F32), lg_ref[...], lb_ref[...])
    o_ref[...] = out
    ob_ref[...] = out.astype(BF16)


def _combine_ln(y, pos0, pos1, gate, x, ple, ln_g, ln_b):
    n, d = x.shape
    tm = min(LN_ROWS, n)
    tile = lambda: pl.BlockSpec((tm, d), lambda i, a, b: (i, 0))
    row = lambda: pl.BlockSpec((1, d), lambda i, a, b: (0, 0))
    return pl.pallas_call(
        _combine_kernel,
        out_shape=(jax.ShapeDtypeStruct((n, d), F32), jax.ShapeDtypeStruct((n, d), BF16)),
        grid_spec=pltpu.PrefetchScalarGridSpec(
            num_scalar_prefetch=2,
            grid=(n // tm,),
            in_specs=[pl.BlockSpec(memory_space=pl.ANY), tile(), tile(),
                      pl.BlockSpec((tm, LANES), lambda i, a, b: (i, 0)), row(), row()],
            out_specs=(tile(), tile()),
            scratch_shapes=[pltpu.VMEM((2, 2, tm, d), F32), pltpu.SemaphoreType.DMA((2,))]),
        compiler_params=_params(("arbitrary",)),
        name="moe_combine_ln",
    )(pos0, pos1, y, x, ple, gate, ln_g.reshape(1, d), ln_b.reshape(1, d))


def _moe_plan(meta, counts, n_experts):
    n = meta.shape[1]
    nk = n * TOP_K
    nblk = nk // FFN_ROWS + n_experts
    sub_per_blk = FFN_ROWS // FFN_SUB
    expert = meta[:TOP_K]
    rank = meta[TOP_K:2 * TOP_K]
    nsub_e = (counts + FFN_SUB - 1) // FFN_SUB
    nblk_e = (nsub_e + sub_per_blk - 1) // sub_per_blk
    rows_e = jnp.maximum((nsub_e + nblk_e - 1) // jnp.maximum(nblk_e, 1), 1) * FFN_SUB
    blk_end = jnp.cumsum(nblk_e)
    blk_off = blk_end - nblk_e
    onehot = expert[None] == jnp.arange(n_experts, dtype=jnp.int32)[:, None, None]
    rpb = jnp.sum(jnp.where(onehot, rows_e[:, None, None], 0), axis=0)
    first = jnp.sum(jnp.where(onehot, blk_off[:, None, None], 0), axis=0)
    pos = (first + rank // rpb) * FFN_ROWS + rank % rpb
    blk = jnp.arange(nblk, dtype=jnp.int32)
    used = blk < blk_end[-1]
    block_e = jnp.clip(jnp.searchsorted(blk_end, blk, side="right"), 0, n_experts - 1).astype(jnp.int32)
    block_e = jnp.where(used, block_e, block_e[jnp.maximum(blk_end[-1] - 1, 0)])
    rows = jnp.clip(counts[block_e] - (blk - blk_off[block_e]) * rows_e[block_e], 0, rows_e[block_e])
    rows = jnp.where(used, rows, 0)
    block_nsub = ((rows + FFN_SUB - 1) // FFN_SUB).astype(jnp.int32)
    tok = jnp.tile(jnp.arange(n, dtype=jnp.int32), TOP_K)
    slot_tok = jnp.zeros((nblk * FFN_ROWS,), jnp.int32).at[pos.reshape(nk)].set(
        tok, unique_indices=True)
    return pos.astype(jnp.int32), slot_tok, block_e, block_nsub, nblk * FFN_ROWS


def _token_mixer(xb, layer, w_in, w_br_ret, w_br_sb, w_gate, b_gate, batch, seq):
    d = xb.shape[1]
    in_width = w_in.shape[2]
    rw = RET_HEADS * HEAD_DIM
    sw = SB_HEADS * HEAD_DIM
    col_scale = jnp.ones((in_width,), F32)
    col_scale = col_scale.at[rw:2 * rw].set(HEAD_DIM ** -0.5)
    col_scale = col_scale.at[4 * rw:4 * rw + sw].set(HEAD_DIM ** -0.5 * math.log2(math.e))
    h = _mm([xb[None]], [(0, w_in, 0)], [(col_scale.reshape(1, 1, in_width), 0)], layer, _ep_colscale,
            in_width, BF16, 1024, 1024, "in_proj")
    o_ret = _retention(h, batch, seq)
    o_sb = _stick_breaking(h, batch, seq, (4 * RET_HEADS) // SB_HEADS)
    tn = min(512, d)
    bg = b_gate.reshape(b_gate.shape[0], 1, 2 * d)
    return _mm([o_ret[None], o_sb[None], xb[None]],
               [(0, w_br_ret, 0), (1, w_br_sb, 0), (2, w_gate, 0), (2, w_gate, d // tn)],
               [(bg, 0), (bg, d // tn)], layer, _ep_merge, d, BF16, 512, tn, "branch_merge")


def kernel(x, p, w_in, w_br_ret, w_br_sb, w_gate, b_gate, w_o, ln1_g, ln1_b,
           ffn_w1, ffn_w3, ffn_w2, moe_router, moe_w1, moe_w3, moe_w2,
           ple_w, ple_gate_w, ln2_g, ln2_b):
    batch, seq, d = x.shape
    n = batch * seq
    depth = w_in.shape[0]
    n_experts = moe_router.shape[2]
    xf = x.reshape(n, d)
    xb = xf.astype(BF16)
    pf = p.reshape(depth, n, p.shape[3])
    ew1 = moe_w1.reshape((-1,) + moe_w1.shape[2:])
    ew3 = moe_w3.reshape((-1,) + moe_w3.shape[2:])
    ew2 = moe_w2.reshape((-1,) + moe_w2.shape[2:])
    for i in range(depth):
        merged = _token_mixer(xb, i, w_in, w_br_ret, w_br_sb, w_gate, b_gate, batch, seq)
        xf, xb = _proj_ln(merged, w_o, i, xf, ln1_g[i], ln1_b[i], "out_proj_ln")
        ple = _mm([xb[None], pf], [(0, ple_gate_w, 0), (1, ple_w, 0)], [], i, _ep_ple, d, BF16, 1024, 1024, "ple")
        if i % 2 == 0:
            nblk = n // min(FFN_ROWS, n)
            f = _ffn(xb, ffn_w1, ffn_w3, ffn_w2, jnp.full((nblk,), i // 2, jnp.int32),
                     jnp.full((nblk,), FFN_ROWS // FFN_SUB, jnp.int32), "dense_swiglu", all_rows=True)
            xf, xb = _ln(xf, [f, ple], ln2_g[i], ln2_b[i], "ln_ffn")
        else:
            meta, gate, counts = _route(xf, moe_router[i // 2])
            pos, slot_tok, block_e, block_nsub, slots = _moe_plan(meta, counts, n_experts)
            xs = _gather_rows(xf, slot_tok, block_nsub, slots)
            y = _ffn(xs, ew1, ew3, ew2, block_e + (i // 2) * n_experts, block_nsub, "expert_swiglu")
            xf, xb = _combine_ln(y, pos[0], pos[1], gate, xf, ple, ln2_g[i], ln2_b[i])
    return xf.reshape(batch, seq, d)
```

```python
import functools
import math

import jax
import jax.numpy as jnp
from jax import lax
from jax.experimental import pallas as pl
from jax.experimental.pallas import tpu as pltpu

F32 = jnp.float32
BF16 = jnp.bfloat16

RET_HEADS = 8
SB_HEADS = 8
HEAD_DIM = 128
RET_CHUNK = 128
ROPE_BASE = 10000.0
TOP_K = 2
DEPTH = 2
DEEPNORM_ALPHA = (2 * DEPTH) ** 0.25
LN_EPS = 1e-5

V7X_VMEM_BYTES = 64 * 1024 * 1024
VMEM_LIMIT = V7X_VMEM_BYTES - 8 * 1024 * 1024
LANES = 128

SB_BLOCK = 128
PROJ_LN_ROWS = 512
FFN_ROWS = 2048
FFN_SUB = 128
FFN_CHUNK = 1024
FFN_TF = 256
LN_ROWS = 256
ROUTE_ROWS = 512
META_ROWS = 8
GATHER_ROWS = 512


def _params(sem):
    return pltpu.CompilerParams(dimension_semantics=sem, vmem_limit_bytes=VMEM_LIMIT)


def _mm_kernel(*refs, a_of, n_a, n_extra, epilogue):
    n_prod = len(a_of)
    a_refs = refs[:n_a]
    b_refs = refs[n_a:n_a + n_prod]
    e_refs = refs[n_a + n_prod:n_a + n_prod + n_extra]
    o_ref = refs[n_a + n_prod + n_extra]
    b_scr = refs[n_a + n_prod + n_extra + 1:]

    @pl.when(pl.program_id(1) == 0)
    def _():
        for b_ref, s in zip(b_refs, b_scr):
            s[...] = b_ref[...].astype(BF16)

    a_vals = [a[...].astype(BF16) for a in a_refs]
    accs = [jnp.dot(a_vals[ai], s[...], preferred_element_type=F32) for ai, s in zip(a_of, b_scr)]
    o_ref[...] = epilogue(accs, [e[...] for e in e_refs]).astype(o_ref.dtype)


def _mm(a_ops, products, extras, layer, epilogue, n_out, out_dtype, tm, tn, name):
    m = a_ops[0].shape[1]
    tm = min(tm, m)
    tn = min(tn, n_out)
    assert m % tm == 0 and n_out % tn == 0
    in_specs, args, scratch = [], [], []
    for a in a_ops:
        la = layer if a.shape[0] > 1 else 0
        in_specs.append(pl.BlockSpec((None, tm, a.shape[2]), lambda j, i, la=la: (la, i, 0)))
        args.append(a)
    for _, b, off in products:
        in_specs.append(pl.BlockSpec((None, b.shape[1], tn), lambda j, i, off=off: (layer, 0, j + off)))
        args.append(b)
        scratch.append(pltpu.VMEM((b.shape[1], tn), BF16))
    for e, off in extras:
        le = layer if e.shape[0] > 1 else 0
        in_specs.append(pl.BlockSpec((None, 1, tn), lambda j, i, off=off, le=le: (le, 0, j + off)))
        args.append(e)
    kern = functools.partial(_mm_kernel, a_of=tuple(ai for ai, _, _ in products), n_a=len(a_ops),
                             n_extra=len(extras), epilogue=epilogue)
    return pl.pallas_call(
        kern,
        out_shape=jax.ShapeDtypeStruct((m, n_out), out_dtype),
        grid=(n_out // tn, m // tm),
        in_specs=in_specs,
        out_specs=pl.BlockSpec((tm, tn), lambda j, i: (i, j)),
        scratch_shapes=scratch,
        compiler_params=_params(("arbitrary", "arbitrary")),
        name=name,
    )(*args)


def _ep_colscale(accs, extras):
    return accs[0] * extras[0]


def _ep_merge(accs, extras):
    o_r, o_s, z_r, z_s = accs
    b_r, b_s = extras
    return jax.nn.sigmoid(z_r + b_r) * o_r + jax.nn.sigmoid(z_s + b_s) * o_s


def _ep_ple(accs, extras):
    return jax.nn.sigmoid(accs[0]) * accs[1]


def _ln_body(y, g, b):
    mu = jnp.mean(y, axis=-1, keepdims=True)
    yc = y - mu
    var = jnp.mean(yc * yc, axis=-1, keepdims=True)
    return yc * lax.rsqrt(var + LN_EPS) * g + b


def _ln_kernel(*refs, n_add):
    x_ref = refs[0]
    add_refs = refs[1:1 + n_add]
    g_ref, b_ref, o_ref, ob_ref = refs[1 + n_add:]
    y = DEEPNORM_ALPHA * x_ref[...]
    for a in add_refs:
        y = y + a[...].astype(F32)
    out = _ln_body(y, g_ref[...], b_ref[...])
    o_ref[...] = out
    ob_ref[...] = out.astype(BF16)


def _ln(x, adds, g, b, name):
    n, d = x.shape
    tm = min(LN_ROWS, n)
    tile = pl.BlockSpec((tm, d), lambda i: (i, 0))
    row = pl.BlockSpec((1, d), lambda i: (0, 0))
    return pl.pallas_call(
        functools.partial(_ln_kernel, n_add=len(adds)),
        out_shape=(jax.ShapeDtypeStruct((n, d), F32), jax.ShapeDtypeStruct((n, d), BF16)),
        grid=(n // tm,),
        in_specs=[tile] * (1 + len(adds)) + [row, row],
        out_specs=(tile, tile),
        compiler_params=_params(("arbitrary",)),
        name=name,
    )(x, *adds, g.reshape(1, d), b.reshape(1, d))


def _proj_ln_kernel(a_ref, w_ref, x_ref, g_ref, b_ref, o_ref, ob_ref, w_scr):
    @pl.when(pl.program_id(0) == 0)
    def _():
        w_scr[...] = w_ref[...].astype(BF16)

    mix = jnp.dot(a_ref[...], w_scr[...], preferred_element_type=F32)
    out = _ln_body(DEEPNORM_ALPHA * x_ref[...] + mix, g_ref[...], b_ref[...])
    o_ref[...] = out
    ob_ref[...] = out.astype(BF16)


def _proj_ln(a, w, layer, x, g, b, name):
    n, d = x.shape
    k = a.shape[1]
    tm = min(PROJ_LN_ROWS, n)
    tile = pl.BlockSpec((tm, d), lambda i: (i, 0))
    row = pl.BlockSpec((1, d), lambda i: (0, 0))
    return pl.pallas_call(
        _proj_ln_kernel,
        out_shape=(jax.ShapeDtypeStruct((n, d), F32), jax.ShapeDtypeStruct((n, d), BF16)),
        grid=(n // tm,),
        in_specs=[pl.BlockSpec((tm, k), lambda i: (i, 0)),
                  pl.BlockSpec((None, k, d), lambda i: (layer, 0, 0), pipeline_mode=pl.Buffered(1)),
                  tile, row, row],
        out_specs=(tile, tile),
        scratch_shapes=[pltpu.VMEM((k, d), BF16)],
        compiler_params=_params(("arbitrary",)),
        name=name,
    )(a, w, x, g.reshape(1, d), b.reshape(1, d))


def _dot_nt(a, b):
    return lax.dot_general(a, b, (((1,), (1,)), ((), ())), preferred_element_type=F32)


def _dot_tn(a, b):
    return lax.dot_general(a, b, (((0,), (0,)), ((), ())), preferred_element_type=F32)


def _ret_kernel(q_ref, k_ref, v_ref, g_ref, cos_ref, sin_ref, decay_ref, xi_ref, zeta_ref,
                o_ref, r_scr, *, chunk_decay):
    d = HEAD_DIM

    @pl.when(pl.program_id(1) == 0)
    def _():
        r_scr[...] = jnp.zeros_like(r_scr)

    cos = cos_ref[...]
    sin = sin_ref[...]

    def rot(t):
        return t * cos + pltpu.roll(t, d // 2, 1) * sin

    heads = [slice(h * d, (h + 1) * d) for h in range(RET_HEADS)]
    qbs, kbs, kzs = [], [], []
    for h, sl in enumerate(heads):
        kr = rot(k_ref[:, sl].astype(F32))
        qbs.append(rot(q_ref[:, sl].astype(F32)).astype(BF16))
        kbs.append(kr.astype(BF16))
        kzs.append((kr * zeta_ref[h]).astype(BF16))
    inners = [_dot_nt(qb, kb) for qb, kb in zip(qbs, kbs)]
    crosses = [jnp.dot(qb, r_scr[h].astype(BF16), preferred_element_type=F32) for h, qb in enumerate(qbs)]
    kvs = [_dot_tn(kz, v_ref[:, sl]) for kz, sl in zip(kzs, heads)]
    for h in range(RET_HEADS):
        r_scr[h] = r_scr[h] * chunk_decay[h] + kvs[h]
    pbs = [(inner * decay_ref[h]).astype(BF16) for h, inner in enumerate(inners)]
    outs = [jnp.dot(pb, v_ref[:, sl], preferred_element_type=F32) for pb, sl in zip(pbs, heads)]
    for h, sl in enumerate(heads):
        o = outs[h] + crosses[h] * xi_ref[h]
        mu = jnp.mean(o, axis=-1, keepdims=True)
        oc = o - mu
        var = jnp.mean(oc * oc, axis=-1, keepdims=True)
        g = g_ref[:, sl].astype(F32)
        o_ref[:, sl] = (oc * lax.rsqrt(var + LN_EPS) * (g * jax.nn.sigmoid(g))).astype(o_ref.dtype)


def _retention(h, batch, seq):
    n = h.shape[0]
    c = RET_CHUNK
    d = HEAD_DIM
    w = RET_HEADS * d
    nc = seq // c
    half = d // 2
    pos = jnp.arange(seq, dtype=F32)
    inv = ROPE_BASE ** (-jnp.arange(half, dtype=F32) / half)
    ang = pos[:, None] * inv[None, :]
    cos = jnp.concatenate([jnp.cos(ang), jnp.cos(ang)], axis=1)
    sin = jnp.concatenate([-jnp.sin(ang), jnp.sin(ang)], axis=1)
    gamma = 1.0 - jnp.exp2(-5.0 - jnp.arange(RET_HEADS, dtype=F32))
    lg = jnp.log(gamma)
    idx = jnp.arange(c, dtype=F32)
    diff = idx[:, None] - idx[None, :]
    causal = diff >= 0
    decay = jnp.where(causal[None], jnp.exp(jnp.where(causal, diff, 0.0)[None] * lg[:, None, None]), 0.0)
    xi = jnp.broadcast_to(jnp.exp((idx + 1.0)[None, :] * lg[:, None])[:, :, None], (RET_HEADS, c, d))
    zeta = jnp.broadcast_to(jnp.exp((c - 1.0 - idx)[None, :] * lg[:, None])[:, :, None], (RET_HEADS, c, d))
    chunk_decay = tuple(math.exp(c * math.log(1.0 - 2.0 ** (-5.0 - hh))) for hh in range(RET_HEADS))

    def col(j):
        return pl.BlockSpec((c, w), lambda b, t, j=j: (b * nc + t, j))

    tab = pl.BlockSpec((c, d), lambda b, t: (t, 0))
    hconst = pl.BlockSpec((RET_HEADS, c, d), lambda b, t: (0, 0, 0))
    return pl.pallas_call(
        functools.partial(_ret_kernel, chunk_decay=chunk_decay),
        out_shape=jax.ShapeDtypeStruct((n, w), BF16),
        grid=(batch, nc),
        in_specs=[col(0), col(1), col(2), col(3), tab, tab,
                  pl.BlockSpec((RET_HEADS, c, c), lambda b, t: (0, 0, 0)), hconst, hconst],
        out_specs=pl.BlockSpec((c, w), lambda b, t: (b * nc + t, 0)),
        scratch_shapes=[pltpu.VMEM((RET_HEADS, d, d), F32)],
        compiler_params=_params(("arbitrary", "arbitrary")),
        name="retention",
    )(h, h, h, h, cos, sin, decay, xi, zeta)


def _sb_kernel(q_ref, k_ref, v_ref, u_ref, o_ref, acc, carry):
    blk = SB_BLOCK
    d = HEAD_DIM
    qb = pl.program_id(1)
    acc[...] = jnp.zeros_like(acc)
    carry[...] = jnp.zeros_like(carry)
    row = lax.broadcasted_iota(jnp.int32, (blk, blk), 0)
    col = lax.broadcasted_iota(jnp.int32, (blk, blk), 1)
    heads = [slice(hh * d, (hh + 1) * d) for hh in range(SB_HEADS)]
    sign = jnp.uint32(0x80000000)

    def sweep(kb, nblk, mask):
        keys = pl.ds(pl.multiple_of(kb * blk, blk), nblk * blk)
        zs = [_dot_nt(q_ref[:, sl], k_ref[keys, sl]) for sl in heads]
        log_betas, hilos = [], []
        for z in zs:
            neg_abs = lax.bitcast_convert_type(lax.bitcast_convert_type(z, jnp.uint32) | sign, F32)
            log_beta = jnp.minimum(z, 0.0) - jnp.log2(1.0 + jnp.exp2(neg_abs))
            log_fail = log_beta - z
            if mask is not None:
                log_fail = jnp.where(mask, log_fail, 0.0)
            hi = log_fail.astype(BF16)
            lo = (log_fail - hi.astype(F32)).astype(BF16)
            log_betas.append(log_beta)
            for c in range(nblk):
                cols = slice(c * blk, (c + 1) * blk)
                hilos.append(jnp.concatenate([hi[:, cols], lo[:, cols]], axis=1))
        s_all = jnp.dot(jnp.concatenate(hilos, axis=0), u_ref[...], preferred_element_type=F32)
        ws = []
        for hh, sl in enumerate(heads):
            run = carry[:, sl]
            parts = [None] * nblk
            for c in reversed(range(nblk)):
                s = s_all[(hh * nblk + c) * blk:(hh * nblk + c + 1) * blk]
                parts[c] = log_betas[hh][:, c * blk:(c + 1) * blk] + run + s[:, :blk]
                run = run + s[:, blk:]
            carry[:, sl] = run
            w = jnp.exp2(parts[0] if nblk == 1 else jnp.concatenate(parts, axis=1))
            if mask is not None:
                w = jnp.where(mask, w, 0.0)
            ws.append(w.astype(BF16))
        for w, sl in zip(ws, heads):
            acc[:, sl] += jnp.dot(w, v_ref[keys, sl], preferred_element_type=F32)

    sweep(qb, 1, col < row)

    @pl.when(qb % 2 == 1)
    def _():
        sweep(qb - 1, 1, None)

    @pl.when((qb // 2) % 2 == 1)
    def _():
        sweep((qb // 4) * 4, 2, None)

    def body(i, c):
        sweep((qb // 4 - 1 - i) * 4, 4, None)
        return c

    lax.fori_loop(0, qb // 4, body, 0)
    o_ref[...] = acc[...].astype(o_ref.dtype)


def _stick_breaking(h, batch, seq, col0):
    n = h.shape[0]
    blk = SB_BLOCK
    w = SB_HEADS * HEAD_DIM
    nq = seq // blk
    idx = jnp.arange(blk)
    tri = (idx[:, None] > idx[None, :]).astype(BF16)
    half = jnp.concatenate([tri, jnp.ones((blk, blk), BF16)], axis=1)
    u = jnp.concatenate([half, half], axis=0)
    return pl.pallas_call(
        _sb_kernel,
        out_shape=jax.ShapeDtypeStruct((n, w), BF16),
        grid=(batch, nq),
        in_specs=[pl.BlockSpec((blk, w), lambda b, t: (b * nq + t, col0)),
                  pl.BlockSpec((seq, w), lambda b, t: (b, col0 + 1)),
                  pl.BlockSpec((seq, w), lambda b, t: (b, col0 + 2)),
                  pl.BlockSpec((2 * blk, 2 * blk), lambda b, t: (0, 0))],
        out_specs=pl.BlockSpec((blk, w), lambda b, t: (b * nq + t, 0)),
        scratch_shapes=[pltpu.VMEM((blk, w), F32), pltpu.VMEM((blk, w), F32)],
        compiler_params=_params(("arbitrary", "arbitrary")),
        name="stick_breaking",
    )(h, h, h, u)


def _ffn_kernel(be_ref, ns_ref, x_ref, w1_ref, w3_ref, w2_ref, y_hbm, acc, sem, *, all_rows):
    i = pl.program_id(0)
    j = pl.program_id(1)
    tm = x_ref.shape[0]
    nsub = ns_ref[i]

    @pl.when(j == 0)
    def _():
        acc[...] = jnp.zeros_like(acc)

    def chunk(start, size):
        rows = pl.ds(start, size)
        x = x_ref[rows, :]
        a = jnp.dot(x, w1_ref[...].astype(BF16), preferred_element_type=F32)
        b = jnp.dot(x, w3_ref[...].astype(BF16), preferred_element_type=F32)
        hmid = (a * jax.nn.sigmoid(a) * b).astype(BF16)
        acc[rows, :] += jnp.dot(hmid, w2_ref[...].astype(BF16), preferred_element_type=F32)

    last = j == pl.num_programs(1) - 1
    half = tm // 2

    def copy_out(h):
        return pltpu.make_async_copy(acc.at[pl.ds(h * half, half), :],
                                     y_hbm.at[pl.ds(pl.multiple_of(i * tm + h * half, half), half), :], sem.at[h])

    if all_rows:
        step = min(FFN_CHUNK // 2, half)
        for s in range(tm // step):
            chunk(s * step, step)
            if (s + 1) * step == half:
                @pl.when(last)
                def _():
                    copy_out(0).start()
    else:
        assert half == FFN_CHUNK and FFN_CHUNK == 8 * FFN_SUB
        nfull = nsub // 8

        def full_body(s, carry):
            chunk(pl.multiple_of(s * FFN_CHUNK, FFN_CHUNK), FFN_CHUNK)

            @pl.when(jnp.logical_and(last, s == 0))
            def _():
                copy_out(0).start()

            return carry

        lax.fori_loop(0, nfull, full_body, 0)
        rem = nsub % 8
        off = nfull * FFN_CHUNK
        for subs in (4, 2, 1):
            size = subs * FFN_SUB

            @pl.when((rem & subs) != 0)
            def _(off=off, size=size):
                chunk(pl.multiple_of(off, size), size)

            off = off + (rem & subs) * FFN_SUB

        @pl.when(jnp.logical_and(last, nfull == 0))
        def _():
            copy_out(0).start()

    @pl.when(last)
    def _():
        copy_out(1).start()
        copy_out(0).wait()
        copy_out(1).wait()


def _ffn(x, w1, w3, w2, block_e, block_nsub, name, all_rows=False):
    p, d = x.shape
    f = w1.shape[2]
    tm = min(FFN_ROWS, p)
    tf = min(FFN_TF, f)
    assert p % tm == 0 and f % tf == 0 and tm % FFN_SUB == 0
    nj = f // tf

    def jj(i, j, ns):
        return jnp.where(ns[i] > 0, j, nj - 1)

    return pl.pallas_call(
        functools.partial(_ffn_kernel, all_rows=all_rows),
        out_shape=jax.ShapeDtypeStruct((p, d), F32),
        grid_spec=pltpu.PrefetchScalarGridSpec(
            num_scalar_prefetch=2,
            grid=(p // tm, nj),
            in_specs=[pl.BlockSpec((tm, d), lambda i, j, be, ns: (i, 0)),
                      pl.BlockSpec((None, d, tf), lambda i, j, be, ns: (be[i], 0, jj(i, j, ns))),
                      pl.BlockSpec((None, d, tf), lambda i, j, be, ns: (be[i], 0, jj(i, j, ns))),
                      pl.BlockSpec((None, tf, d), lambda i, j, be, ns: (be[i], jj(i, j, ns), 0))],
            out_specs=pl.BlockSpec(memory_space=pl.ANY),
            scratch_shapes=[pltpu.VMEM((tm, d), F32), pltpu.SemaphoreType.DMA((2,))]),
        compiler_params=_params(("arbitrary", "arbitrary")),
        name=name,
    )(block_e, block_nsub, x, w1, w3, w2)


def _route_kernel(x_ref, w_ref, tri_ref, meta_ref, gate_ref, cnt_ref, run, *, n_experts):
    @pl.when(pl.program_id(0) == 0)
    def _():
        run[...] = jnp.zeros_like(run)

    x = x_ref[...]
    w = w_ref[...]
    xh = x.astype(BF16)
    wh = w.astype(BF16)
    xl = (x - xh.astype(F32)).astype(BF16)
    wl = (w - wh.astype(F32)).astype(BF16)
    logits = (jnp.dot(xh, wh, preferred_element_type=F32) + jnp.dot(xl, wh, preferred_element_type=F32)
              + jnp.dot(xh, wl, preferred_element_type=F32))
    lane_i = lax.broadcasted_iota(jnp.int32, logits.shape, 1)
    lane = lane_i.astype(F32)
    neg = jnp.float32(-jnp.inf)
    logits = jnp.where(lane_i < n_experts, logits, neg)
    m1 = jnp.max(logits, axis=-1, keepdims=True)
    i1 = jnp.min(jnp.where(logits == m1, lane, float(LANES)), axis=-1, keepdims=True)
    rest = jnp.where(lane == i1, neg, logits)
    m2 = jnp.max(rest, axis=-1, keepdims=True)
    i2 = jnp.min(jnp.where(rest == m2, lane, float(LANES)), axis=-1, keepdims=True)
    e2 = jnp.exp(m2 - m1)
    den = 1.0 + e2
    pick1 = jnp.where(lane == i1, 1.0, 0.0)
    pick2 = jnp.where(lane == i2, 1.0, 0.0)
    picks = pick1 + pick2
    before = run[...] + jnp.dot(tri_ref[...], picks.astype(BF16), preferred_element_type=F32)
    r1 = jnp.sum(before * pick1, axis=-1, keepdims=True)
    r2 = jnp.sum(before * pick2, axis=-1, keepdims=True)
    run[...] += jnp.sum(picks, axis=0, keepdims=True)
    cnt_ref[...] = run[...].astype(jnp.int32)
    meta = jnp.where(lane_i == 0, i1, jnp.where(lane_i == 1, i2, jnp.where(
        lane_i == 2, r1, jnp.where(lane_i == 3, r2, 0.0))))
    meta_ref[...] = meta.T[:META_ROWS].astype(jnp.int32)
    gate_ref[...] = jnp.where(lane_i == 0, 1.0 / den, jnp.where(lane_i == 1, e2 / den, 0.0))


def _route(x, w_router):
    n, d = x.shape
    e = w_router.shape[1]
    tm = min(ROUTE_ROWS, n)
    wpad = jnp.zeros((d, LANES), F32).at[:, :e].set(w_router)
    t = jnp.arange(tm)
    tri = (t[:, None] > t[None, :]).astype(BF16)
    meta, gate, cnt = pl.pallas_call(
        functools.partial(_route_kernel, n_experts=e),
        out_shape=(jax.ShapeDtypeStruct((META_ROWS, n), jnp.int32), jax.ShapeDtypeStruct((n, LANES), F32),
                   jax.ShapeDtypeStruct((1, LANES), jnp.int32)),
        grid=(n // tm,),
        in_specs=[pl.BlockSpec((tm, d), lambda i: (i, 0)), pl.BlockSpec((d, LANES), lambda i: (0, 0)),
                  pl.BlockSpec((tm, tm), lambda i: (0, 0))],
        out_specs=(pl.BlockSpec((META_ROWS, tm), lambda i: (0, i)), pl.BlockSpec((tm, LANES), lambda i: (i, 0)),
                   pl.BlockSpec((1, LANES), lambda i: (0, 0))),
        scratch_shapes=[pltpu.VMEM((1, LANES), F32)],
        compiler_params=_params(("arbitrary",)),
        name="route_top2",
    )(x, wpad, tri)
    return meta, gate, cnt[0, :e]


def _gather_kernel(tok_ref, ns_ref, x_hbm, o_ref, buf, sem):
    s = pl.program_id(0)
    per = FFN_ROWS // GATHER_ROWS

    def nonempty(step):
        return (step % per) * GATHER_ROWS < ns_ref[step // per] * FFN_SUB

    def issue(step):
        slot = step % 2
        base = step * GATHER_ROWS

        def row(r, carry):
            t = tok_ref[base + r]
            pltpu.make_async_copy(x_hbm.at[pl.ds(t, 1), :], buf.at[slot, pl.ds(r, 1), :], sem.at[slot]).start()
            return carry

        lax.fori_loop(0, GATHER_ROWS, row, 0, unroll=8)

    @pl.when(jnp.logical_and(s == 0, nonempty(0)))
    def _():
        issue(0)

    nxt = jnp.minimum(s + 1, pl.num_programs(0) - 1)

    @pl.when(jnp.logical_and(s + 1 < pl.num_programs(0), nonempty(nxt)))
    def _():
        issue(nxt)

    @pl.when(nonempty(s))
    def _():
        slot = s % 2
        pltpu.make_async_copy(x_hbm.at[pl.ds(0, GATHER_ROWS), :], buf.at[slot], sem.at[slot]).wait()
        o_ref[...] = buf[slot].astype(o_ref.dtype)

    @pl.when(jnp.logical_not(nonempty(s)))
    def _():
        o_ref[...] = jnp.zeros_like(o_ref)


def _gather_rows(x, slot_tok, block_nsub, p):
    n, d = x.shape
    return pl.pallas_call(
        _gather_kernel,
        out_shape=jax.ShapeDtypeStruct((p, d), BF16),
        grid_spec=pltpu.PrefetchScalarGridSpec(
            num_scalar_prefetch=2,
            grid=(p // GATHER_ROWS,),
            in_specs=[pl.BlockSpec(memory_space=pl.ANY)],
            out_specs=pl.BlockSpec((GATHER_ROWS, d), lambda s, tok, ns: (s, 0)),
            scratch_shapes=[pltpu.VMEM((2, GATHER_ROWS, d), F32), pltpu.SemaphoreType.DMA((2,))]),
        compiler_params=_params(("arbitrary",)),
        name="moe_gather",
    )(slot_tok, block_nsub, x)


def _combine_kernel(p0_ref, p1_ref, y_hbm, x_ref, ple_ref, gate_ref, lg_ref, lb_ref,
                    o_ref, ob_ref, buf, sem):
    i = pl.program_id(0)
    tm = x_ref.shape[0]

    def issue(tile):
        slot = tile % 2
        base = tile * tm

        def row(r, carry):
            pltpu.make_async_copy(y_hbm.at[pl.ds(p0_ref[base + r], 1), :], buf.at[slot, 0, pl.ds(r, 1), :],
                                  sem.at[slot]).start()
            pltpu.make_async_copy(y_hbm.at[pl.ds(p1_ref[base + r], 1), :], buf.at[slot, 1, pl.ds(r, 1), :],
                                  sem.at[slot]).start()
            return carry

        lax.fori_loop(0, tm, row, 0, unroll=4)

    @pl.when(i == 0)
    def _():
        issue(0)

    @pl.when(i + 1 < pl.num_programs(0))
    def _():
        issue(i + 1)

    slot = i % 2
    pltpu.make_async_copy(y_hbm.at[pl.ds(0, tm), :], buf.at[slot, 0], sem.at[slot]).wait()
    pltpu.make_async_copy(y_hbm.at[pl.ds(0, tm), :], buf.at[slot, 1], sem.at[slot]).wait()
    gate = gate_ref[...]
    f = gate[:, 0:1] * buf[slot, 0] + gate[:, 1:2] * buf[slot, 1]
    out = _ln_body(DEEPNORM_ALPHA * x_ref[...] + f + ple_ref[...].astype(F32), lg_ref[...], lb_ref[...])
    o_ref[...] = out
    ob_ref[...] = out.astype(BF16)


def _combine_ln(y, pos0, pos1, gate, x, ple, ln_g, ln_b):
    n, d = x.shape
    tm = min(LN_ROWS, n)
    tile = lambda: pl.BlockSpec((tm, d), lambda i, a, b: (i, 0))
    row = lambda: pl.BlockSpec((1, d), lambda i, a, b: (0, 0))
    return pl.pallas_call(
        _combine_kernel,
        out_shape=(jax.ShapeDtypeStruct((n, d), F32), jax.ShapeDtypeStruct((n, d), BF16)),
        grid_spec=pltpu.PrefetchScalarGridSpec(
            num_scalar_prefetch=2,
            grid=(n // tm,),
            in_specs=[pl.BlockSpec(memory_space=pl.ANY), tile(), tile(),
                      pl.BlockSpec((tm, LANES), lambda i, a, b: (i, 0)), row(), row()],
            out_specs=(tile(), tile()),
            scratch_shapes=[pltpu.VMEM((2, 2, tm, d), F32), pltpu.SemaphoreType.DMA((2,))]),
        compiler_params=_params(("arbitrary",)),
        name="moe_combine_ln",
    )(pos0, pos1, y, x, ple, gate, ln_g.reshape(1, d), ln_b.reshape(1, d))


def _moe_plan(meta, counts, n_experts):
    n = meta.shape[1]
    nk = n * TOP_K
    nblk = nk // FFN_ROWS + n_experts
    sub_per_blk = FFN_ROWS // FFN_SUB
    expert = meta[:TOP_K]
    rank = meta[TOP_K:2 * TOP_K]
    nsub_e = (counts + FFN_SUB - 1) // FFN_SUB
    nblk_e = (nsub_e + sub_per_blk - 1) // sub_per_blk
    rows_e = jnp.maximum((nsub_e + nblk_e - 1) // jnp.maximum(nblk_e, 1), 1) * FFN_SUB
    blk_end = jnp.cumsum(nblk_e)
    blk_off = blk_end - nblk_e
    onehot = expert[None] == jnp.arange(n_experts, dtype=jnp.int32)[:, None, None]
    rpb = jnp.sum(jnp.where(onehot, rows_e[:, None, None], 0), axis=0)
    first = jnp.sum(jnp.where(onehot, blk_off[:, None, None], 0), axis=0)
    pos = (first + rank // rpb) * FFN_ROWS + rank % rpb
    blk = jnp.arange(nblk, dtype=jnp.int32)
    used = blk < blk_end[-1]
    block_e = jnp.clip(jnp.searchsorted(blk_end, blk, side="right"), 0, n_experts - 1).astype(jnp.int32)
    block_e = jnp.where(used, block_e, block_e[jnp.maximum(blk_end[-1] - 1, 0)])
    rows = jnp.clip(counts[block_e] - (blk - blk_off[block_e]) * rows_e[block_e], 0, rows_e[block_e])
    rows = jnp.where(used, rows, 0)
    block_nsub = ((rows + FFN_SUB - 1) // FFN_SUB).astype(jnp.int32)
    tok = jnp.tile(jnp.arange(n, dtype=jnp.int32), TOP_K)
    slot_tok = jnp.zeros((nblk * FFN_ROWS,), jnp.int32).at[pos.reshape(nk)].set(
        tok, unique_indices=True)
    return pos.astype(jnp.int32), slot_tok, block_e, block_nsub, nblk * FFN_ROWS


def _token_mixer(xb, layer, w_in, w_br_ret, w_br_sb, w_gate, b_gate, batch, seq):
    d = xb.shape[1]
    in_width = w_in.shape[2]
    rw = RET_HEADS * HEAD_DIM
    sw = SB_HEADS * HEAD_DIM
    col_scale = jnp.ones((in_width,), F32)
    col_scale = col_scale.at[rw:2 * rw].set(HEAD_DIM ** -0.5)
    col_scale = col_scale.at[4 * rw:4 * rw + sw].set(HEAD_DIM ** -0.5 * math.log2(math.e))
    h = _mm([xb[None]], [(0, w_in, 0)], [(col_scale.reshape(1, 1, in_width), 0)], layer, _ep_colscale,
            in_width, BF16, 1024, 1024, "in_proj")
    o_ret = _retention(h, batch, seq)
    o_sb = _stick_breaking(h, batch, seq, (4 * RET_HEADS) // SB_HEADS)
    tn = min(512, d)
    bg = b_gate.reshape(b_gate.shape[0], 1, 2 * d)
    return _mm([o_ret[None], o_sb[None], xb[None]],
               [(0, w_br_ret, 0), (1, w_br_sb, 0), (2, w_gate, 0), (2, w_gate, d // tn)],
               [(bg, 0), (bg, d // tn)], layer, _ep_merge, d, BF16, 512, tn, "branch_merge")


def kernel(x, p, w_in, w_br_ret, w_br_sb, w_gate, b_gate, w_o, ln1_g, ln1_b,
           ffn_w1, ffn_w3, ffn_w2, moe_router, moe_w1, moe_w3, moe_w2,
           ple_w, ple_gate_w, ln2_g, ln2_b):
    batch, seq, d = x.shape
    n = batch * seq
    depth = w_in.shape[0]
    n_experts = moe_router.shape[2]
    xf = x.reshape(n, d)
    xb = xf.astype(BF16)
    pf = p.reshape(depth, n, p.shape[3])
    ew1 = moe_w1.reshape((-1,) + moe_w1.shape[2:])
    ew3 = moe_w3.reshape((-1,) + moe_w3.shape[2:])
    ew2 = moe_w2.reshape((-1,) + moe_w2.shape[2:])
    for i in range(depth):
        merged = _token_mixer(xb, i, w_in, w_br_ret, w_br_sb, w_gate, b_gate, batch, seq)
        xf, xb = _proj_ln(merged, w_o, i, xf, ln1_g[i], ln1_b[i], "out_proj_ln")
        ple = _mm([xb[None], pf], [(0, ple_gate_w, 0), (1, ple_w, 0)], [], i, _ep_ple, d, BF16, 1024, 1024, "ple")
        if i % 2 == 0:
            nblk = n // min(FFN_ROWS, n)
            f = _ffn(xb, ffn_w1, ffn_w3, ffn_w2, jnp.full((nblk,), i // 2, jnp.int32),
                     jnp.full((nblk,), FFN_ROWS // FFN_SUB, jnp.int32), "dense_swiglu", all_rows=True)
            xf, xb = _ln(xf, [f, ple], ln2_g[i], ln2_b[i], "ln_ffn")
        else:
            meta, gate, counts = _route(xf, moe_router[i // 2])
            pos, slot_tok, block_e, block_nsub, slots = _moe_plan(meta, counts, n_experts)
            xs = _gather_rows(xf, slot_tok, block_nsub, slots)
            y = _ffn(xs, ew1, ew3, ew2, block_e + (i // 2) * n_experts, block_nsub, "expert_swiglu")
            xf, xb = _combine_ln(y, pos[0], pos[1], gate, xf, ple, ln2_g[i], ln2_b[i])
    return xf.reshape(batch, seq, d)
```

```python
import functools
import math

import jax
import jax.numpy as jnp
from jax import lax
from jax.experimental import pallas as pl
from jax.experimental.pallas import tpu as pltpu

F32 = jnp.float32
BF16 = jnp.bfloat16

RET_HEADS = 8
SB_HEADS = 8
HEAD_DIM = 128
RET_CHUNK = 128
ROPE_BASE = 10000.0
TOP_K = 2
DEPTH = 2
DEEPNORM_ALPHA = (2 * DEPTH) ** 0.25
LN_EPS = 1e-5

V7X_VMEM_BYTES = 64 * 1024 * 1024
VMEM_LIMIT = V7X_VMEM_BYTES - 8 * 1024 * 1024
LANES = 128

RET_STEP_CHUNKS = 2
MM_EPILOGUE_COLS = 256
SB_BLOCK = 128
PROJ_LN_ROWS = 512
FFN_ROWS = 2048
FFN_SUB = 128
FFN_CHUNK = 1024
FFN_TF = 256
LN_ROWS = 256
ROUTE_ROWS = 512
META_ROWS = 8
GATHER_ROWS = 256


def _params(sem):
    return pltpu.CompilerParams(dimension_semantics=sem, vmem_limit_bytes=VMEM_LIMIT)


def _mm_kernel(*refs, a_of, n_a, n_extra, epilogue):
    n_prod = len(a_of)
    a_refs = refs[:n_a]
    b_refs = refs[n_a:n_a + n_prod]
    e_refs = refs[n_a + n_prod:n_a + n_prod + n_extra]
    o_ref = refs[n_a + n_prod + n_extra]
    b_scr = refs[n_a + n_prod + n_extra + 1:]

    @pl.when(pl.program_id(1) == 0)
    def _():
        for b_ref, s in zip(b_refs, b_scr):
            s[...] = b_ref[...].astype(BF16)

    a_vals = [a[...].astype(BF16) for a in a_refs]
    tn = o_ref.shape[1]
    cw = min(MM_EPILOGUE_COLS, tn)
    for c in range(tn // cw):
        cols = slice(c * cw, (c + 1) * cw)
        accs = [jnp.dot(a_vals[ai], s[:, cols], preferred_element_type=F32) for ai, s in zip(a_of, b_scr)]
        o_ref[:, cols] = epilogue(accs, [e[:, cols] for e in e_refs]).astype(o_ref.dtype)


def _mm(a_ops, products, extras, layer, epilogue, n_out, out_dtype, tm, tn, name):
    m = a_ops[0].shape[1]
    tm = min(tm, m)
    tn = min(tn, n_out)
    assert m % tm == 0 and n_out % tn == 0
    in_specs, args, scratch = [], [], []
    for a in a_ops:
        la = layer if a.shape[0] > 1 else 0
        in_specs.append(pl.BlockSpec((None, tm, a.shape[2]), lambda j, i, la=la: (la, i, 0)))
        args.append(a)
    for _, b, off in products:
        in_specs.append(pl.BlockSpec((None, b.shape[1], tn), lambda j, i, off=off: (layer, 0, j + off)))
        args.append(b)
        scratch.append(pltpu.VMEM((b.shape[1], tn), BF16))
    for e, off in extras:
        le = layer if e.shape[0] > 1 else 0
        in_specs.append(pl.BlockSpec((None, 1, tn), lambda j, i, off=off, le=le: (le, 0, j + off)))
        args.append(e)
    kern = functools.partial(_mm_kernel, a_of=tuple(ai for ai, _, _ in products), n_a=len(a_ops),
                             n_extra=len(extras), epilogue=epilogue)
    return pl.pallas_call(
        kern,
        out_shape=jax.ShapeDtypeStruct((m, n_out), out_dtype),
        grid=(n_out // tn, m // tm),
        in_specs=in_specs,
        out_specs=pl.BlockSpec((tm, tn), lambda j, i: (i, j)),
        scratch_shapes=scratch,
        compiler_params=_params(("arbitrary", "arbitrary")),
        name=name,
    )(*args)


def _ep_colscale(accs, extras):
    return accs[0] * extras[0]


def _ep_merge(accs, extras):
    o_r, o_s, z_r, z_s = accs
    b_r, b_s = extras
    return jax.nn.sigmoid(z_r + b_r) * o_r + jax.nn.sigmoid(z_s + b_s) * o_s


def _ep_ple(accs, extras):
    return jax.nn.sigmoid(accs[0]) * accs[1]


def _ln_body(y, g, b):
    mu = jnp.mean(y, axis=-1, keepdims=True)
    yc = y - mu
    var = jnp.mean(yc * yc, axis=-1, keepdims=True)
    return yc * lax.rsqrt(var + LN_EPS) * g + b


def _ln_kernel(*refs, n_add):
    x_ref = refs[0]
    add_refs = refs[1:1 + n_add]
    g_ref, b_ref, o_ref, ob_ref = refs[1 + n_add:]
    y = DEEPNORM_ALPHA * x_ref[...]
    for a in add_refs:
        y = y + a[...].astype(F32)
    out = _ln_body(y, g_ref[...], b_ref[...])
    o_ref[...] = out
    ob_ref[...] = out.astype(BF16)


def _ln(x, adds, g, b, name):
    n, d = x.shape
    tm = min(LN_ROWS, n)
    tile = pl.BlockSpec((tm, d), lambda i: (i, 0))
    row = pl.BlockSpec((1, d), lambda i: (0, 0))
    return pl.pallas_call(
        functools.partial(_ln_kernel, n_add=len(adds)),
        out_shape=(jax.ShapeDtypeStruct((n, d), F32), jax.ShapeDtypeStruct((n, d), BF16)),
        grid=(n // tm,),
        in_specs=[tile] * (1 + len(adds)) + [row, row],
        out_specs=(tile, tile),
        compiler_params=_params(("arbitrary",)),
        name=name,
    )(x, *adds, g.reshape(1, d), b.reshape(1, d))


def _proj_ln_kernel(a_ref, w_ref, x_ref, g_ref, b_ref, o_ref, ob_ref, w_scr):
    @pl.when(pl.program_id(0) == 0)
    def _():
        w_scr[...] = w_ref[...].astype(BF16)

    mix = jnp.dot(a_ref[...], w_scr[...], preferred_element_type=F32)
    out = _ln_body(DEEPNORM_ALPHA * x_ref[...] + mix, g_ref[...], b_ref[...])
    o_ref[...] = out
    ob_ref[...] = out.astype(BF16)


def _proj_ln(a, w, layer, x, g, b, name):
    n, d = x.shape
    k = a.shape[1]
    tm = min(PROJ_LN_ROWS, n)
    tile = pl.BlockSpec((tm, d), lambda i: (i, 0))
    row = pl.BlockSpec((1, d), lambda i: (0, 0))
    return pl.pallas_call(
        _proj_ln_kernel,
        out_shape=(jax.ShapeDtypeStruct((n, d), F32), jax.ShapeDtypeStruct((n, d), BF16)),
        grid=(n // tm,),
        in_specs=[pl.BlockSpec((tm, k), lambda i: (i, 0)),
                  pl.BlockSpec((None, k, d), lambda i: (layer, 0, 0), pipeline_mode=pl.Buffered(1)),
                  tile, row, row],
        out_specs=(tile, tile),
        scratch_shapes=[pltpu.VMEM((k, d), BF16)],
        compiler_params=_params(("arbitrary",)),
        name=name,
    )(a, w, x, g.reshape(1, d), b.reshape(1, d))


def _dot_nt(a, b):
    return lax.dot_general(a, b, (((1,), (1,)), ((), ())), preferred_element_type=F32)


def _dot_tn(a, b):
    return lax.dot_general(a, b, (((0,), (0,)), ((), ())), preferred_element_type=F32)


def _ret_kernel(q_ref, k_ref, v_ref, g_ref, cos_ref, sin_ref, decay_ref, xi_ref, zeta_ref,
                o_ref, r_scr, *, chunk_decay):
    d = HEAD_DIM

    @pl.when(pl.program_id(1) == 0)
    def _():
        r_scr[...] = jnp.zeros_like(r_scr)

    c = RET_CHUNK
    pairs = [(slice(ci * c, (ci + 1) * c), h, slice(h * d, (h + 1) * d))
             for ci in range(RET_STEP_CHUNKS) for h in range(RET_HEADS)]
    qbs, kbs, kzs = [], [], []
    for rows, h, sl in pairs:
        cos = cos_ref[rows, :]
        sin = sin_ref[rows, :]

        def rot(t):
            return t * cos + pltpu.roll(t, d // 2, 1) * sin

        kr = rot(k_ref[rows, sl].astype(F32))
        qbs.append(rot(q_ref[rows, sl].astype(F32)).astype(BF16))
        kbs.append(kr.astype(BF16))
        kzs.append((kr * zeta_ref[h]).astype(BF16))
    inners = [_dot_nt(qb, kb) for qb, kb in zip(qbs, kbs)]
    kvs = [_dot_tn(kz, v_ref[rows, sl]) for kz, (rows, h, sl) in zip(kzs, pairs)]
    states = [None] * len(pairs)
    for h in range(RET_HEADS):
        r = r_scr[h]
        for ci in range(RET_STEP_CHUNKS):
            states[ci * RET_HEADS + h] = r
            r = r * chunk_decay[h] + kvs[ci * RET_HEADS + h]
        r_scr[h] = r
    crosses = [jnp.dot(qb, r.astype(BF16), preferred_element_type=F32) for qb, r in zip(qbs, states)]
    pbs = [(inner * decay_ref[h]).astype(BF16) for inner, (rows, h, sl) in zip(inners, pairs)]
    outs = [jnp.dot(pb, v_ref[rows, sl], preferred_element_type=F32) for pb, (rows, h, sl) in zip(pbs, pairs)]
    for out, cross, (rows, h, sl) in zip(outs, crosses, pairs):
        o = out + cross * xi_ref[h]
        mu = jnp.mean(o, axis=-1, keepdims=True)
        oc = o - mu
        var = jnp.mean(oc * oc, axis=-1, keepdims=True)
        g = g_ref[rows, sl].astype(F32)
        o_ref[rows, sl] = (oc * lax.rsqrt(var + LN_EPS) * (g * jax.nn.sigmoid(g))).astype(o_ref.dtype)


def _retention(h, batch, seq):
    n = h.shape[0]
    c = RET_CHUNK
    d = HEAD_DIM
    w = RET_HEADS * d
    half = d // 2
    pos = jnp.arange(seq, dtype=F32)
    inv = ROPE_BASE ** (-jnp.arange(half, dtype=F32) / half)
    ang = pos[:, None] * inv[None, :]
    cos = jnp.concatenate([jnp.cos(ang), jnp.cos(ang)], axis=1)
    sin = jnp.concatenate([-jnp.sin(ang), jnp.sin(ang)], axis=1)
    gamma = 1.0 - jnp.exp2(-5.0 - jnp.arange(RET_HEADS, dtype=F32))
    lg = jnp.log(gamma)
    idx = jnp.arange(c, dtype=F32)
    diff = idx[:, None] - idx[None, :]
    causal = diff >= 0
    decay = jnp.where(causal[None], jnp.exp(jnp.where(causal, diff, 0.0)[None] * lg[:, None, None]), 0.0)
    xi = jnp.broadcast_to(jnp.exp((idx + 1.0)[None, :] * lg[:, None])[:, :, None], (RET_HEADS, c, d))
    zeta = jnp.broadcast_to(jnp.exp((c - 1.0 - idx)[None, :] * lg[:, None])[:, :, None], (RET_HEADS, c, d))
    chunk_decay = tuple(math.exp(c * math.log(1.0 - 2.0 ** (-5.0 - hh))) for hh in range(RET_HEADS))

    rows = RET_STEP_CHUNKS * c
    ns = seq // rows
    assert seq % rows == 0

    def col(j):
        return pl.BlockSpec((rows, w), lambda b, t, j=j: (b * ns + t, j))

    tab = pl.BlockSpec((rows, d), lambda b, t: (t, 0))
    hconst = pl.BlockSpec((RET_HEADS, c, d), lambda b, t: (0, 0, 0))
    return pl.pallas_call(
        functools.partial(_ret_kernel, chunk_decay=chunk_decay),
        out_shape=jax.ShapeDtypeStruct((n, w), BF16),
        grid=(batch, ns),
        in_specs=[col(0), col(1), col(2), col(3), tab, tab,
                  pl.BlockSpec((RET_HEADS, c, c), lambda b, t: (0, 0, 0)), hconst, hconst],
        out_specs=pl.BlockSpec((rows, w), lambda b, t: (b * ns + t, 0)),
        scratch_shapes=[pltpu.VMEM((RET_HEADS, d, d), F32)],
        compiler_params=_params(("arbitrary", "arbitrary")),
        name="retention",
    )(h, h, h, h, cos, sin, decay, xi, zeta)


def _sb_kernel(q_ref, k_ref, v_ref, u_ref, o_ref, acc, carry):
    blk = SB_BLOCK
    d = HEAD_DIM
    qb = pl.program_id(1)
    acc[...] = jnp.zeros_like(acc)
    carry[...] = jnp.zeros_like(carry)
    row = lax.broadcasted_iota(jnp.int32, (blk, blk), 0)
    col = lax.broadcasted_iota(jnp.int32, (blk, blk), 1)
    heads = [slice(hh * d, (hh + 1) * d) for hh in range(SB_HEADS)]
    sign = jnp.uint32(0x80000000)

    def sweep(kb, nblk, mask):
        keys = pl.ds(pl.multiple_of(kb * blk, blk), nblk * blk)
        zs = [_dot_nt(q_ref[:, sl], k_ref[keys, sl]) for sl in heads]
        log_betas, hilos = [], []
        for z in zs:
            neg_abs = lax.bitcast_convert_type(lax.bitcast_convert_type(z, jnp.uint32) | sign, F32)
            log_beta = jnp.minimum(z, 0.0) - jnp.log2(1.0 + jnp.exp2(neg_abs))
            log_fail = log_beta - z
            if mask is not None:
                log_fail = jnp.where(mask, log_fail, 0.0)
            hi = log_fail.astype(BF16)
            lo = (log_fail - hi.astype(F32)).astype(BF16)
            log_betas.append(log_beta)
            for c in range(nblk):
                cols = slice(c * blk, (c + 1) * blk)
                hilos.append(jnp.concatenate([hi[:, cols], lo[:, cols]], axis=1))
        s_all = jnp.dot(jnp.concatenate(hilos, axis=0), u_ref[...], preferred_element_type=F32)
        ws = []
        for hh, sl in enumerate(heads):
            run = carry[:, sl]
            parts = [None] * nblk
            for c in reversed(range(nblk)):
                s = s_all[(hh * nblk + c) * blk:(hh * nblk + c + 1) * blk]
                parts[c] = log_betas[hh][:, c * blk:(c + 1) * blk] + run + s[:, :blk]
                run = run + s[:, blk:]
            carry[:, sl] = run
            w = jnp.exp2(parts[0] if nblk == 1 else jnp.concatenate(parts, axis=1))
            if mask is not None:
                w = jnp.where(mask, w, 0.0)
            ws.append(w.astype(BF16))
        for w, sl in zip(ws, heads):
            acc[:, sl] += jnp.dot(w, v_ref[keys, sl], preferred_element_type=F32)

    sweep(qb, 1, col < row)

    @pl.when(qb % 2 == 1)
    def _():
        sweep(qb - 1, 1, None)

    @pl.when((qb // 2) % 2 == 1)
    def _():
        sweep((qb // 4) * 4, 2, None)

    def body(i, c):
        sweep((qb // 4 - 1 - i) * 4, 4, None)
        return c

    lax.fori_loop(0, qb // 4, body, 0)
    o_ref[...] = acc[...].astype(o_ref.dtype)


def _stick_breaking(h, batch, seq, col0):
    n = h.shape[0]
    blk = SB_BLOCK
    w = SB_HEADS * HEAD_DIM
    nq = seq // blk
    idx = jnp.arange(blk)
    tri = (idx[:, None] > idx[None, :]).astype(BF16)
    half = jnp.concatenate([tri, jnp.ones((blk, blk), BF16)], axis=1)
    u = jnp.concatenate([half, half], axis=0)
    return pl.pallas_call(
        _sb_kernel,
        out_shape=jax.ShapeDtypeStruct((n, w), BF16),
        grid=(batch, nq),
        in_specs=[pl.BlockSpec((blk, w), lambda b, t: (b * nq + t, col0)),
                  pl.BlockSpec((seq, w), lambda b, t: (b, col0 + 1)),
                  pl.BlockSpec((seq, w), lambda b, t: (b, col0 + 2)),
                  pl.BlockSpec((2 * blk, 2 * blk), lambda b, t: (0, 0))],
        out_specs=pl.BlockSpec((blk, w), lambda b, t: (b * nq + t, 0)),
        scratch_shapes=[pltpu.VMEM((blk, w), F32), pltpu.VMEM((blk, w), F32)],
        compiler_params=_params(("arbitrary", "arbitrary")),
        name="stick_breaking",
    )(h, h, h, u)


def _ffn_kernel(be_ref, ns_ref, x_ref, w1_ref, w3_ref, w2_ref, y_hbm, acc, sem, *, all_rows):
    i = pl.program_id(0)
    j = pl.program_id(1)
    tm = x_ref.shape[0]
    nsub = ns_ref[i]

    @pl.when(j == 0)
    def _():
        acc[...] = jnp.zeros_like(acc)

    def chunk(start, size):
        rows = pl.ds(start, size)
        x = x_ref[rows, :]
        a = jnp.dot(x, w1_ref[...].astype(BF16), preferred_element_type=F32)
        b = jnp.dot(x, w3_ref[...].astype(BF16), preferred_element_type=F32)
        hmid = (a * jax.nn.sigmoid(a) * b).astype(BF16)
        acc[rows, :] += jnp.dot(hmid, w2_ref[...].astype(BF16), preferred_element_type=F32)

    last = j == pl.num_programs(1) - 1
    half = tm // 2

    def copy_out(h):
        return pltpu.make_async_copy(acc.at[pl.ds(h * half, half), :],
                                     y_hbm.at[pl.ds(pl.multiple_of(i * tm + h * half, half), half), :], sem.at[h])

    if all_rows:
        step = min(FFN_CHUNK // 2, half)
        for s in range(tm // step):
            chunk(s * step, step)
            if (s + 1) * step == half:
                @pl.when(last)
                def _():
                    copy_out(0).start()
    else:
        assert half == FFN_CHUNK and FFN_CHUNK == 8 * FFN_SUB
        nfull = nsub // 8

        def full_body(s, carry):
            chunk(pl.multiple_of(s * FFN_CHUNK, FFN_CHUNK), FFN_CHUNK)

            @pl.when(jnp.logical_and(last, s == 0))
            def _():
                copy_out(0).start()

            return carry

        lax.fori_loop(0, nfull, full_body, 0)
        rem = nsub % 8
        off = nfull * FFN_CHUNK
        for subs in (4, 2, 1):
            size = subs * FFN_SUB

            @pl.when((rem & subs) != 0)
            def _(off=off, size=size):
                chunk(pl.multiple_of(off, size), size)

            off = off + (rem & subs) * FFN_SUB

        @pl.when(jnp.logical_and(last, nfull == 0))
        def _():
            copy_out(0).start()

    @pl.when(last)
    def _():
        copy_out(1).start()
        copy_out(0).wait()
        copy_out(1).wait()


def _ffn(x, w1, w3, w2, block_e, block_nsub, name, all_rows=False):
    p, d = x.shape
    f = w1.shape[2]
    tm = min(FFN_ROWS, p)
    tf = min(FFN_TF, f)
    assert p % tm == 0 and f % tf == 0 and tm % FFN_SUB == 0
    nj = f // tf

    def jj(i, j, ns):
        return jnp.where(ns[i] > 0, j, nj - 1)

    return pl.pallas_call(
        functools.partial(_ffn_kernel, all_rows=all_rows),
        out_shape=jax.ShapeDtypeStruct((p, d), F32),
        grid_spec=pltpu.PrefetchScalarGridSpec(
            num_scalar_prefetch=2,
            grid=(p // tm, nj),
            in_specs=[pl.BlockSpec((tm, d), lambda i, j, be, ns: (i, 0)),
                      pl.BlockSpec((None, d, tf), lambda i, j, be, ns: (be[i], 0, jj(i, j, ns))),
                      pl.BlockSpec((None, d, tf), lambda i, j, be, ns: (be[i], 0, jj(i, j, ns))),
                      pl.BlockSpec((None, tf, d), lambda i, j, be, ns: (be[i], jj(i, j, ns), 0))],
            out_specs=pl.BlockSpec(memory_space=pl.ANY),
            scratch_shapes=[pltpu.VMEM((tm, d), F32), pltpu.SemaphoreType.DMA((2,))]),
        compiler_params=_params(("arbitrary", "arbitrary")),
        name=name,
    )(block_e, block_nsub, x, w1, w3, w2)


def _route_kernel(x_ref, w_ref, tri_ref, meta_ref, gate_ref, cnt_ref, run, *, n_experts):
    @pl.when(pl.program_id(0) == 0)
    def _():
        run[...] = jnp.zeros_like(run)

    x = x_ref[...]
    w = w_ref[...]
    xh = x.astype(BF16)
    wh = w.astype(BF16)
    xl = (x - xh.astype(F32)).astype(BF16)
    wl = (w - wh.astype(F32)).astype(BF16)
    logits = (jnp.dot(xh, wh, preferred_element_type=F32) + jnp.dot(xl, wh, preferred_element_type=F32)
              + jnp.dot(xh, wl, preferred_element_type=F32))
    lane_i = lax.broadcasted_iota(jnp.int32, logits.shape, 1)
    lane = lane_i.astype(F32)
    neg = jnp.float32(-jnp.inf)
    logits = jnp.where(lane_i < n_experts, logits, neg)
    m1 = jnp.max(logits, axis=-1, keepdims=True)
    i1 = jnp.min(jnp.where(logits == m1, lane, float(LANES)), axis=-1, keepdims=True)
    rest = jnp.where(lane == i1, neg, logits)
    m2 = jnp.max(rest, axis=-1, keepdims=True)
    i2 = jnp.min(jnp.where(rest == m2, lane, float(LANES)), axis=-1, keepdims=True)
    e2 = jnp.exp(m2 - m1)
    den = 1.0 + e2
    pick1 = jnp.where(lane == i1, 1.0, 0.0)
    pick2 = jnp.where(lane == i2, 1.0, 0.0)
    picks = pick1 + pick2
    before = run[...] + jnp.dot(tri_ref[...], picks.astype(BF16), preferred_element_type=F32)
    r1 = jnp.sum(before * pick1, axis=-1, keepdims=True)
    r2 = jnp.sum(before * pick2, axis=-1, keepdims=True)
    run[...] += jnp.sum(picks, axis=0, keepdims=True)
    cnt_ref[...] = run[...].astype(jnp.int32)
    meta = jnp.where(lane_i == 0, i1, jnp.where(lane_i == 1, i2, jnp.where(
        lane_i == 2, r1, jnp.where(lane_i == 3, r2, 0.0))))
    meta_ref[...] = meta.T[:META_ROWS].astype(jnp.int32)
    gate_ref[...] = jnp.where(lane_i == 0, 1.0 / den, jnp.where(lane_i == 1, e2 / den, 0.0))


def _route(x, w_router):
    n, d = x.shape
    e = w_router.shape[1]
    tm = min(ROUTE_ROWS, n)
    wpad = jnp.zeros((d, LANES), F32).at[:, :e].set(w_router)
    t = jnp.arange(tm)
    tri = (t[:, None] > t[None, :]).astype(BF16)
    meta, gate, cnt = pl.pallas_call(
        functools.partial(_route_kernel, n_experts=e),
        out_shape=(jax.ShapeDtypeStruct((META_ROWS, n), jnp.int32), jax.ShapeDtypeStruct((n, LANES), F32),
                   jax.ShapeDtypeStruct((1, LANES), jnp.int32)),
        grid=(n // tm,),
        in_specs=[pl.BlockSpec((tm, d), lambda i: (i, 0)), pl.BlockSpec((d, LANES), lambda i: (0, 0)),
                  pl.BlockSpec((tm, tm), lambda i: (0, 0))],
        out_specs=(pl.BlockSpec((META_ROWS, tm), lambda i: (0, i)), pl.BlockSpec((tm, LANES), lambda i: (i, 0)),
                   pl.BlockSpec((1, LANES), lambda i: (0, 0))),
        scratch_shapes=[pltpu.VMEM((1, LANES), F32)],
        compiler_params=_params(("arbitrary",)),
        name="route_top2",
    )(x, wpad, tri)
    return meta, gate, cnt[0, :e]


def _gather_kernel(tok_ref, ns_ref, x_hbm, o_ref, buf, sem):
    s = pl.program_id(0)
    per = FFN_ROWS // GATHER_ROWS

    def nonempty(step):
        return (step % per) * GATHER_ROWS < ns_ref[step // per] * FFN_SUB

    def issue(step):
        slot = step % 2
        base = step * GATHER_ROWS

        def row(r, carry):
            t = tok_ref[base + r]
            pltpu.make_async_copy(x_hbm.at[pl.ds(t, 1), :], buf.at[slot, pl.ds(r, 1), :], sem.at[slot]).start()
            return carry

        lax.fori_loop(0, GATHER_ROWS, row, 0, unroll=8)

    @pl.when(jnp.logical_and(s == 0, nonempty(0)))
    def _():
        issue(0)

    nxt = jnp.minimum(s + 1, pl.num_programs(0) - 1)

    @pl.when(jnp.logical_and(s + 1 < pl.num_programs(0), nonempty(nxt)))
    def _():
        issue(nxt)

    @pl.when(nonempty(s))
    def _():
        slot = s % 2
        pltpu.make_async_copy(x_hbm.at[pl.ds(0, GATHER_ROWS), :], buf.at[slot], sem.at[slot]).wait()
        o_ref[...] = buf[slot].astype(o_ref.dtype)

    @pl.when(jnp.logical_not(nonempty(s)))
    def _():
        o_ref[...] = jnp.zeros_like(o_ref)


def _gather_rows(x, slot_tok, block_nsub, p):
    n, d = x.shape
    return pl.pallas_call(
        _gather_kernel,
        out_shape=jax.ShapeDtypeStruct((p, d), BF16),
        grid_spec=pltpu.PrefetchScalarGridSpec(
            num_scalar_prefetch=2,
            grid=(p // GATHER_ROWS,),
            in_specs=[pl.BlockSpec(memory_space=pl.ANY)],
            out_specs=pl.BlockSpec((GATHER_ROWS, d), lambda s, tok, ns: (s, 0)),
            scratch_shapes=[pltpu.VMEM((2, GATHER_ROWS, d), F32), pltpu.SemaphoreType.DMA((2,))]),
        compiler_params=_params(("arbitrary",)),
        name="moe_gather",
    )(slot_tok, block_nsub, x)


def _combine_kernel(p0_ref, p1_ref, y_hbm, x_ref, ple_ref, gate_ref, lg_ref, lb_ref,
                    o_ref, ob_ref, buf, sem):
    i = pl.program_id(0)
    tm = x_ref.shape[0]

    def issue(tile):
        slot = tile % 2
        base = tile * tm

        def row(r, carry):
            pltpu.make_async_copy(y_hbm.at[pl.ds(p0_ref[base + r], 1), :], buf.at[slot, 0, pl.ds(r, 1), :],
                                  sem.at[slot]).start()
            pltpu.make_async_copy(y_hbm.at[pl.ds(p1_ref[base + r], 1), :], buf.at[slot, 1, pl.ds(r, 1), :],
                                  sem.at[slot]).start()
            return carry

        lax.fori_loop(0, tm, row, 0, unroll=4)

    @pl.when(i == 0)
    def _():
        issue(0)

    @pl.when(i + 1 < pl.num_programs(0))
    def _():
        issue(i + 1)

    slot = i % 2
    pltpu.make_async_copy(y_hbm.at[pl.ds(0, tm), :], buf.at[slot, 0], sem.at[slot]).wait()
    pltpu.make_async_copy(y_hbm.at[pl.ds(0, tm), :], buf.at[slot, 1], sem.at[slot]).wait()
    gate = gate_ref[...]
    f = gate[:, 0:1] * buf[slot, 0] + gate[:, 1:2] * buf[slot, 1]
    out = _ln_body(DEEPNORM_ALPHA * x_ref[...] + f + ple_ref[...].astype(F32), lg_ref[...], lb_ref[...])
    o_ref[...] = out
    ob_ref[...] = out.astype(BF16)


def _combine_ln(y, pos0, pos1, gate, x, ple, ln_g, ln_b):
    n, d = x.shape
    tm = min(LN_ROWS, n)
    tile = lambda: pl.BlockSpec((tm, d), lambda i, a, b: (i, 0))
    row = lambda: pl.BlockSpec((1, d), lambda i, a, b: (0, 0))
    return pl.pallas_call(
        _combine_kernel,
        out_shape=(jax.ShapeDtypeStruct((n, d), F32), jax.ShapeDtypeStruct((n, d), BF16)),
        grid_spec=pltpu.PrefetchScalarGridSpec(
            num_scalar_prefetch=2,
            grid=(n // tm,),
            in_specs=[pl.BlockSpec(memory_space=pl.ANY), tile(), tile(),
                      pl.BlockSpec((tm, LANES), lambda i, a, b: (i, 0)), row(), row()],
            out_specs=(tile(), tile()),
            scratch_shapes=[pltpu.VMEM((2, 2, tm, d), F32), pltpu.SemaphoreType.DMA((2,))]),
        compiler_params=_params(("arbitrary",)),
        name="moe_combine_ln",
    )(pos0, pos1, y, x, ple, gate, ln_g.reshape(1, d), ln_b.reshape(1, d))


def _moe_plan(meta, counts, n_experts):
    n = meta.shape[1]
    nk = n * TOP_K
    nblk = nk // FFN_ROWS + n_experts
    sub_per_blk = FFN_ROWS // FFN_SUB
    expert = meta[:TOP_K]
    rank = meta[TOP_K:2 * TOP_K]
    nsub_e = (counts + FFN_SUB - 1) // FFN_SUB
    nblk_e = (nsub_e + sub_per_blk - 1) // sub_per_blk
    rows_e = jnp.maximum((nsub_e + nblk_e - 1) // jnp.maximum(nblk_e, 1), 1) * FFN_SUB
    blk_end = jnp.cumsum(nblk_e)
    blk_off = blk_end - nblk_e
    onehot = expert[None] == jnp.arange(n_experts, dtype=jnp.int32)[:, None, None]
    rpb = jnp.sum(jnp.where(onehot, rows_e[:, None, None], 0), axis=0)
    first = jnp.sum(jnp.where(onehot, blk_off[:, None, None], 0), axis=0)
    pos = (first + rank // rpb) * FFN_ROWS + rank % rpb
    blk = jnp.arange(nblk, dtype=jnp.int32)
    used = blk < blk_end[-1]
    block_e = jnp.clip(jnp.searchsorted(blk_end, blk, side="right"), 0, n_experts - 1).astype(jnp.int32)
    block_e = jnp.where(used, block_e, block_e[jnp.maximum(blk_end[-1] - 1, 0)])
    rows = jnp.clip(counts[block_e] - (blk - blk_off[block_e]) * rows_e[block_e], 0, rows_e[block_e])
    rows = jnp.where(used, rows, 0)
    block_nsub = ((rows + FFN_SUB - 1) // FFN_SUB).astype(jnp.int32)
    tok = jnp.tile(jnp.arange(n, dtype=jnp.int32), TOP_K)
    slot_tok = jnp.zeros((nblk * FFN_ROWS,), jnp.int32).at[pos.reshape(nk)].set(
        tok, unique_indices=True)
    return pos.astype(jnp.int32), slot_tok, block_e, block_nsub, nblk * FFN_ROWS


def _token_mixer(xb, layer, w_in, w_br_ret, w_br_sb, w_gate, b_gate, batch, seq):
    d = xb.shape[1]
    in_width = w_in.shape[2]
    rw = RET_HEADS * HEAD_DIM
    sw = SB_HEADS * HEAD_DIM
    col_scale = jnp.ones((in_width,), F32)
    col_scale = col_scale.at[rw:2 * rw].set(HEAD_DIM ** -0.5)
    col_scale = col_scale.at[4 * rw:4 * rw + sw].set(HEAD_DIM ** -0.5 * math.log2(math.e))
    h = _mm([xb[None]], [(0, w_in, 0)], [(col_scale.reshape(1, 1, in_width), 0)], layer, _ep_colscale,
            in_width, BF16, 2048, 1024, "in_proj")
    o_ret = _retention(h, batch, seq)
    o_sb = _stick_breaking(h, batch, seq, (4 * RET_HEADS) // SB_HEADS)
    tn = min(512, d)
    bg = b_gate.reshape(b_gate.shape[0], 1, 2 * d)
    return _mm([o_ret[None], o_sb[None], xb[None]],
               [(0, w_br_ret, 0), (1, w_br_sb, 0), (2, w_gate, 0), (2, w_gate, d // tn)],
               [(bg, 0), (bg, d // tn)], layer, _ep_merge, d, BF16, 512, tn, "branch_merge")


def kernel(x, p, w_in, w_br_ret, w_br_sb, w_gate, b_gate, w_o, ln1_g, ln1_b,
           ffn_w1, ffn_w3, ffn_w2, moe_router, moe_w1, moe_w3, moe_w2,
           ple_w, ple_gate_w, ln2_g, ln2_b):
    batch, seq, d = x.shape
    n = batch * seq
    depth = w_in.shape[0]
    n_experts = moe_router.shape[2]
    xf = x.reshape(n, d)
    xb = xf.astype(BF16)
    pf = p.reshape(depth, n, p.shape[3])
    ew1 = moe_w1.reshape((-1,) + moe_w1.shape[2:])
    ew3 = moe_w3.reshape((-1,) + moe_w3.shape[2:])
    ew2 = moe_w2.reshape((-1,) + moe_w2.shape[2:])
    for i in range(depth):
        merged = _token_mixer(xb, i, w_in, w_br_ret, w_br_sb, w_gate, b_gate, batch, seq)
        xf, xb = _proj_ln(merged, w_o, i, xf, ln1_g[i], ln1_b[i], "out_proj_ln")
        ple = _mm([xb[None], pf], [(0, ple_gate_w, 0), (1, ple_w, 0)], [], i, _ep_ple, d, BF16, 1024, 1024, "ple")
        if i % 2 == 0:
            nblk = n // min(FFN_ROWS, n)
            f = _ffn(xb, ffn_w1, ffn_w3, ffn_w2, jnp.full((nblk,), i // 2, jnp.int32),
                     jnp.full((nblk,), FFN_ROWS // FFN_SUB, jnp.int32), "dense_swiglu", all_rows=True)
            xf, xb = _ln(xf, [f, ple], ln2_g[i], ln2_b[i], "ln_ffn")
        else:
            meta, gate, counts = _route(xf, moe_router[i // 2])
            pos, slot_tok, block_e, block_nsub, slots = _moe_plan(meta, counts, n_experts)
            xs = _gather_rows(xf, slot_tok, block_nsub, slots)
            y = _ffn(xs, ew1, ew3, ew2, block_e + (i // 2) * n_experts, block_nsub, "expert_swiglu")
            xf, xb = _combine_ln(y, pos[0], pos[1], gate, xf, ple, ln2_g[i], ln2_b[i])
    return xf.reshape(batch, seq, d)
```

```python
import functools
import math

import jax
import jax.numpy as jnp
from jax import lax
from jax.experimental import pallas as pl
from jax.experimental.pallas import tpu as pltpu

F32 = jnp.float32
BF16 = jnp.bfloat16

RET_HEADS = 8
SB_HEADS = 8
HEAD_DIM = 128
RET_CHUNK = 128
ROPE_BASE = 10000.0
TOP_K = 2
DEPTH = 2
DEEPNORM_ALPHA = (2 * DEPTH) ** 0.25
LN_EPS = 1e-5

V7X_VMEM_BYTES = 64 * 1024 * 1024
VMEM_LIMIT = V7X_VMEM_BYTES - 8 * 1024 * 1024
LANES = 128

RET_STEP_CHUNKS = 2
MM_EPILOGUE_COLS = 256
SB_BLOCK = 128
PROJ_LN_ROWS = 512
FFN_ROWS = 2048
FFN_SUB = 128
FFN_CHUNK = 1024
FFN_TF = 256
LN_ROWS = 256
ROUTE_ROWS = 512
META_ROWS = 8
GATHER_ROWS = 256


def _params(sem):
    return pltpu.CompilerParams(dimension_semantics=sem, vmem_limit_bytes=VMEM_LIMIT)


def _mm_kernel(*refs, a_of, n_a, n_extra, epilogue):
    n_prod = len(a_of)
    a_refs = refs[:n_a]
    b_refs = refs[n_a:n_a + n_prod]
    e_refs = refs[n_a + n_prod:n_a + n_prod + n_extra]
    o_ref = refs[n_a + n_prod + n_extra]
    b_scr = refs[n_a + n_prod + n_extra + 1:]

    @pl.when(pl.program_id(1) == 0)
    def _():
        for b_ref, s in zip(b_refs, b_scr):
            s[...] = b_ref[...].astype(BF16)

    a_vals = [a[...].astype(BF16) for a in a_refs]
    tn = o_ref.shape[1]
    cw = min(MM_EPILOGUE_COLS, tn)
    for c in range(tn // cw):
        cols = slice(c * cw, (c + 1) * cw)
        accs = [jnp.dot(a_vals[ai], s[:, cols], preferred_element_type=F32) for ai, s in zip(a_of, b_scr)]
        o_ref[:, cols] = epilogue(accs, [e[:, cols] for e in e_refs]).astype(o_ref.dtype)


def _mm(a_ops, products, extras, layer, epilogue, n_out, out_dtype, tm, tn, name):
    m = a_ops[0].shape[1]
    tm = min(tm, m)
    tn = min(tn, n_out)
    assert m % tm == 0 and n_out % tn == 0
    in_specs, args, scratch = [], [], []
    for a in a_ops:
        la = layer if a.shape[0] > 1 else 0
        in_specs.append(pl.BlockSpec((None, tm, a.shape[2]), lambda j, i, la=la: (la, i, 0)))
        args.append(a)
    for _, b, off in products:
        in_specs.append(pl.BlockSpec((None, b.shape[1], tn), lambda j, i, off=off: (layer, 0, j + off)))
        args.append(b)
        scratch.append(pltpu.VMEM((b.shape[1], tn), BF16))
    for e, off in extras:
        le = layer if e.shape[0] > 1 else 0
        in_specs.append(pl.BlockSpec((None, 1, tn), lambda j, i, off=off, le=le: (le, 0, j + off)))
        args.append(e)
    kern = functools.partial(_mm_kernel, a_of=tuple(ai for ai, _, _ in products), n_a=len(a_ops),
                             n_extra=len(extras), epilogue=epilogue)
    return pl.pallas_call(
        kern,
        out_shape=jax.ShapeDtypeStruct((m, n_out), out_dtype),
        grid=(n_out // tn, m // tm),
        in_specs=in_specs,
        out_specs=pl.BlockSpec((tm, tn), lambda j, i: (i, j)),
        scratch_shapes=scratch,
        compiler_params=_params(("arbitrary", "arbitrary")),
        name=name,
    )(*args)


def _ep_colscale(accs, extras):
    return accs[0] * extras[0]


def _ep_merge(accs, extras):
    o_r, o_s, z_r, z_s = accs
    b_r, b_s = extras
    return jax.nn.sigmoid(z_r + b_r) * o_r + jax.nn.sigmoid(z_s + b_s) * o_s


def _ep_ple(accs, extras):
    return jax.nn.sigmoid(accs[0]) * accs[1]


def _ln_body(y, g, b):
    mu = jnp.mean(y, axis=-1, keepdims=True)
    yc = y - mu
    var = jnp.mean(yc * yc, axis=-1, keepdims=True)
    return yc * lax.rsqrt(var + LN_EPS) * g + b


def _ln_kernel(*refs, n_add):
    x_ref = refs[0]
    add_refs = refs[1:1 + n_add]
    g_ref, b_ref, o_ref, ob_ref = refs[1 + n_add:]
    y = DEEPNORM_ALPHA * x_ref[...]
    for a in add_refs:
        y = y + a[...].astype(F32)
    out = _ln_body(y, g_ref[...], b_ref[...])
    o_ref[...] = out
    ob_ref[...] = out.astype(BF16)


def _ln(x, adds, g, b, name):
    n, d = x.shape
    tm = min(LN_ROWS, n)
    tile = pl.BlockSpec((tm, d), lambda i: (i, 0))
    row = pl.BlockSpec((1, d), lambda i: (0, 0))
    return pl.pallas_call(
        functools.partial(_ln_kernel, n_add=len(adds)),
        out_shape=(jax.ShapeDtypeStruct((n, d), F32), jax.ShapeDtypeStruct((n, d), BF16)),
        grid=(n // tm,),
        in_specs=[tile] * (1 + len(adds)) + [row, row],
        out_specs=(tile, tile),
        compiler_params=_params(("arbitrary",)),
        name=name,
    )(x, *adds, g.reshape(1, d), b.reshape(1, d))


def _proj_ln_kernel(a_ref, w_ref, x_ref, g_ref, b_ref, o_ref, ob_ref, w_scr):
    @pl.when(pl.program_id(0) == 0)
    def _():
        w_scr[...] = w_ref[...].astype(BF16)

    mix = jnp.dot(a_ref[...], w_scr[...], preferred_element_type=F32)
    out = _ln_body(DEEPNORM_ALPHA * x_ref[...] + mix, g_ref[...], b_ref[...])
    o_ref[...] = out
    ob_ref[...] = out.astype(BF16)


def _proj_ln(a, w, layer, x, g, b, name):
    n, d = x.shape
    k = a.shape[1]
    tm = min(PROJ_LN_ROWS, n)
    tile = pl.BlockSpec((tm, d), lambda i: (i, 0))
    row = pl.BlockSpec((1, d), lambda i: (0, 0))
    return pl.pallas_call(
        _proj_ln_kernel,
        out_shape=(jax.ShapeDtypeStruct((n, d), F32), jax.ShapeDtypeStruct((n, d), BF16)),
        grid=(n // tm,),
        in_specs=[pl.BlockSpec((tm, k), lambda i: (i, 0)),
                  pl.BlockSpec((None, k, d), lambda i: (layer, 0, 0), pipeline_mode=pl.Buffered(1)),
                  tile, row, row],
        out_specs=(tile, tile),
        scratch_shapes=[pltpu.VMEM((k, d), BF16)],
        compiler_params=_params(("arbitrary",)),
        name=name,
    )(a, w, x, g.reshape(1, d), b.reshape(1, d))


def _dot_nt(a, b):
    return lax.dot_general(a, b, (((1,), (1,)), ((), ())), preferred_element_type=F32)


def _dot_tn(a, b):
    return lax.dot_general(a, b, (((0,), (0,)), ((), ())), preferred_element_type=F32)


def _ret_kernel(q_ref, k_ref, v_ref, g_ref, cos_ref, sin_ref, decay_ref, xi_ref, zeta_ref,
                o_ref, r_scr, *, chunk_decay):
    d = HEAD_DIM

    @pl.when(pl.program_id(1) == 0)
    def _():
        r_scr[...] = jnp.zeros_like(r_scr)

    c = RET_CHUNK
    pairs = [(slice(ci * c, (ci + 1) * c), h, slice(h * d, (h + 1) * d))
             for ci in range(RET_STEP_CHUNKS) for h in range(RET_HEADS)]
    qbs, kbs, kzs = [], [], []
    for rows, h, sl in pairs:
        cos = cos_ref[rows, :]
        sin = sin_ref[rows, :]

        def rot(t):
            return t * cos + pltpu.roll(t, d // 2, 1) * sin

        kr = rot(k_ref[rows, sl].astype(F32))
        qbs.append(rot(q_ref[rows, sl].astype(F32)).astype(BF16))
        kbs.append(kr.astype(BF16))
        kzs.append((kr * zeta_ref[h]).astype(BF16))
    inners = [_dot_nt(qb, kb) for qb, kb in zip(qbs, kbs)]
    kvs = [_dot_tn(kz, v_ref[rows, sl]) for kz, (rows, h, sl) in zip(kzs, pairs)]
    states = [None] * len(pairs)
    for h in range(RET_HEADS):
        r = r_scr[h]
        for ci in range(RET_STEP_CHUNKS):
            states[ci * RET_HEADS + h] = r
            r = r * chunk_decay[h] + kvs[ci * RET_HEADS + h]
        r_scr[h] = r
    crosses = [jnp.dot(qb, r.astype(BF16), preferred_element_type=F32) for qb, r in zip(qbs, states)]
    pbs = [(inner * decay_ref[h]).astype(BF16) for inner, (rows, h, sl) in zip(inners, pairs)]
    outs = [jnp.dot(pb, v_ref[rows, sl], preferred_element_type=F32) for pb, (rows, h, sl) in zip(pbs, pairs)]
    for out, cross, (rows, h, sl) in zip(outs, crosses, pairs):
        o = out + cross * xi_ref[h]
        mu = jnp.mean(o, axis=-1, keepdims=True)
        oc = o - mu
        var = jnp.mean(oc * oc, axis=-1, keepdims=True)
        g = g_ref[rows, sl].astype(F32)
        o_ref[rows, sl] = (oc * lax.rsqrt(var + LN_EPS) * (g * jax.nn.sigmoid(g))).astype(o_ref.dtype)


def _retention(h, batch, seq):
    n = h.shape[0]
    c = RET_CHUNK
    d = HEAD_DIM
    w = RET_HEADS * d
    half = d // 2
    pos = jnp.arange(seq, dtype=F32)
    inv = ROPE_BASE ** (-jnp.arange(half, dtype=F32) / half)
    ang = pos[:, None] * inv[None, :]
    cos = jnp.concatenate([jnp.cos(ang), jnp.cos(ang)], axis=1)
    sin = jnp.concatenate([-jnp.sin(ang), jnp.sin(ang)], axis=1)
    gamma = 1.0 - jnp.exp2(-5.0 - jnp.arange(RET_HEADS, dtype=F32))
    lg = jnp.log(gamma)
    idx = jnp.arange(c, dtype=F32)
    diff = idx[:, None] - idx[None, :]
    causal = diff >= 0
    decay = jnp.where(causal[None], jnp.exp(jnp.where(causal, diff, 0.0)[None] * lg[:, None, None]), 0.0)
    xi = jnp.broadcast_to(jnp.exp((idx + 1.0)[None, :] * lg[:, None])[:, :, None], (RET_HEADS, c, d))
    zeta = jnp.broadcast_to(jnp.exp((c - 1.0 - idx)[None, :] * lg[:, None])[:, :, None], (RET_HEADS, c, d))
    chunk_decay = tuple(math.exp(c * math.log(1.0 - 2.0 ** (-5.0 - hh))) for hh in range(RET_HEADS))

    rows = RET_STEP_CHUNKS * c
    ns = seq // rows
    assert seq % rows == 0

    def col(j):
        return pl.BlockSpec((rows, w), lambda b, t, j=j: (b * ns + t, j))

    tab = pl.BlockSpec((rows, d), lambda b, t: (t, 0))
    hconst = pl.BlockSpec((RET_HEADS, c, d), lambda b, t: (0, 0, 0))
    return pl.pallas_call(
        functools.partial(_ret_kernel, chunk_decay=chunk_decay),
        out_shape=jax.ShapeDtypeStruct((n, w), BF16),
        grid=(batch, ns),
        in_specs=[col(0), col(1), col(2), col(3), tab, tab,
                  pl.BlockSpec((RET_HEADS, c, c), lambda b, t: (0, 0, 0)), hconst, hconst],
        out_specs=pl.BlockSpec((rows, w), lambda b, t: (b * ns + t, 0)),
        scratch_shapes=[pltpu.VMEM((RET_HEADS, d, d), F32)],
        compiler_params=_params(("arbitrary", "arbitrary")),
        name="retention",
    )(h, h, h, h, cos, sin, decay, xi, zeta)


def _sb_kernel(q_ref, k_ref, v_ref, u_ref, o_ref, acc, carry):
    blk = SB_BLOCK
    d = HEAD_DIM
    qb = pl.program_id(1)
    acc[...] = jnp.zeros_like(acc)
    carry[...] = jnp.zeros_like(carry)
    row = lax.broadcasted_iota(jnp.int32, (blk, blk), 0)
    col = lax.broadcasted_iota(jnp.int32, (blk, blk), 1)
    heads = [slice(hh * d, (hh + 1) * d) for hh in range(SB_HEADS)]
    sign = jnp.uint32(0x80000000)

    def sweep(kb, nblk, mask):
        keys = pl.ds(pl.multiple_of(kb * blk, blk), nblk * blk)
        zs = [_dot_nt(q_ref[:, sl], k_ref[keys, sl]) for sl in heads]
        log_betas, hilos = [], []
        for z in zs:
            neg_abs = lax.bitcast_convert_type(lax.bitcast_convert_type(z, jnp.uint32) | sign, F32)
            log_beta = jnp.minimum(z, 0.0) - jnp.log2(1.0 + jnp.exp2(neg_abs))
            log_fail = log_beta - z
            if mask is not None:
                log_fail = jnp.where(mask, log_fail, 0.0)
            hi = log_fail.astype(BF16)
            lo = (log_fail - hi.astype(F32)).astype(BF16)
            log_betas.append(log_beta)
            for c in range(nblk):
                cols = slice(c * blk, (c + 1) * blk)
                hilos.append(jnp.concatenate([hi[:, cols], lo[:, cols]], axis=1))
        s_all = jnp.dot(jnp.concatenate(hilos, axis=0), u_ref[...], preferred_element_type=F32)
        ws = []
        for hh, sl in enumerate(heads):
            run = carry[:, sl]
            parts = [None] * nblk
            for c in reversed(range(nblk)):
                s = s_all[(hh * nblk + c) * blk:(hh * nblk + c + 1) * blk]
                parts[c] = log_betas[hh][:, c * blk:(c + 1) * blk] + run + s[:, :blk]
                run = run + s[:, blk:]
            carry[:, sl] = run
            w = jnp.exp2(parts[0] if nblk == 1 else jnp.concatenate(parts, axis=1))
            if mask is not None:
                w = jnp.where(mask, w, 0.0)
            ws.append(w.astype(BF16))
        for w, sl in zip(ws, heads):
            acc[:, sl] += jnp.dot(w, v_ref[keys, sl], preferred_element_type=F32)

    @pl.when(qb % 2 == 0)
    def _():
        sweep(qb, 1, col < row)

    @pl.when(qb % 2 == 1)
    def _():
        row2 = lax.broadcasted_iota(jnp.int32, (blk, 2 * blk), 0)
        col2 = lax.broadcasted_iota(jnp.int32, (blk, 2 * blk), 1)
        sweep(qb - 1, 2, col2 < row2 + blk)

    @pl.when((qb // 2) % 2 == 1)
    def _():
        sweep((qb // 4) * 4, 2, None)

    def body(i, c):
        sweep((qb // 4 - 1 - i) * 4, 4, None)
        return c

    lax.fori_loop(0, qb // 4, body, 0)
    o_ref[...] = acc[...].astype(o_ref.dtype)


def _stick_breaking(h, batch, seq, col0):
    n = h.shape[0]
    blk = SB_BLOCK
    w = SB_HEADS * HEAD_DIM
    nq = seq // blk
    idx = jnp.arange(blk)
    tri = (idx[:, None] > idx[None, :]).astype(BF16)
    half = jnp.concatenate([tri, jnp.ones((blk, blk), BF16)], axis=1)
    u = jnp.concatenate([half, half], axis=0)
    return pl.pallas_call(
        _sb_kernel,
        out_shape=jax.ShapeDtypeStruct((n, w), BF16),
        grid=(batch, nq),
        in_specs=[pl.BlockSpec((blk, w), lambda b, t: (b * nq + t, col0)),
                  pl.BlockSpec((seq, w), lambda b, t: (b, col0 + 1)),
                  pl.BlockSpec((seq, w), lambda b, t: (b, col0 + 2)),
                  pl.BlockSpec((2 * blk, 2 * blk), lambda b, t: (0, 0))],
        out_specs=pl.BlockSpec((blk, w), lambda b, t: (b * nq + t, 0)),
        scratch_shapes=[pltpu.VMEM((blk, w), F32), pltpu.VMEM((blk, w), F32)],
        compiler_params=_params(("arbitrary", "arbitrary")),
        name="stick_breaking",
    )(h, h, h, u)


def _ffn_kernel(be_ref, ns_ref, x_ref, w1_ref, w3_ref, w2_ref, y_hbm, acc, sem, *, all_rows):
    i = pl.program_id(0)
    j = pl.program_id(1)
    tm = x_ref.shape[0]
    nsub = ns_ref[i]

    @pl.when(j == 0)
    def _():
        acc[...] = jnp.zeros_like(acc)

    def chunk(start, size):
        rows = pl.ds(start, size)
        x = x_ref[rows, :]
        a = jnp.dot(x, w1_ref[...].astype(BF16), preferred_element_type=F32)
        b = jnp.dot(x, w3_ref[...].astype(BF16), preferred_element_type=F32)
        hmid = (a * jax.nn.sigmoid(a) * b).astype(BF16)
        acc[rows, :] += jnp.dot(hmid, w2_ref[...].astype(BF16), preferred_element_type=F32)

    last = j == pl.num_programs(1) - 1
    half = tm // 2

    def copy_out(h):
        return pltpu.make_async_copy(acc.at[pl.ds(h * half, half), :],
                                     y_hbm.at[pl.ds(pl.multiple_of(i * tm + h * half, half), half), :], sem.at[h])

    if all_rows:
        step = min(FFN_CHUNK // 2, half)
        for s in range(tm // step):
            chunk(s * step, step)
            if (s + 1) * step == half:
                @pl.when(last)
                def _():
                    copy_out(0).start()
    else:
        assert half == FFN_CHUNK and FFN_CHUNK == 8 * FFN_SUB
        nfull = nsub // 8

        def full_body(s, carry):
            chunk(pl.multiple_of(s * FFN_CHUNK, FFN_CHUNK), FFN_CHUNK)

            @pl.when(jnp.logical_and(last, s == 0))
            def _():
                copy_out(0).start()

            return carry

        lax.fori_loop(0, nfull, full_body, 0)
        rem = nsub % 8
        off = nfull * FFN_CHUNK
        for subs in (4, 2, 1):
            size = subs * FFN_SUB

            @pl.when((rem & subs) != 0)
            def _(off=off, size=size):
                chunk(pl.multiple_of(off, size), size)

            off = off + (rem & subs) * FFN_SUB

        @pl.when(jnp.logical_and(last, nfull == 0))
        def _():
            copy_out(0).start()

    @pl.when(last)
    def _():
        copy_out(1).start()
        copy_out(0).wait()
        copy_out(1).wait()


def _ffn(x, w1, w3, w2, block_e, block_nsub, name, all_rows=False):
    p, d = x.shape
    f = w1.shape[2]
    tm = min(FFN_ROWS, p)
    tf = min(FFN_TF, f)
    assert p % tm == 0 and f % tf == 0 and tm % FFN_SUB == 0
    nj = f // tf

    def jj(i, j, ns):
        return jnp.where(ns[i] > 0, j, nj - 1)

    return pl.pallas_call(
        functools.partial(_ffn_kernel, all_rows=all_rows),
        out_shape=jax.ShapeDtypeStruct((p, d), F32),
        grid_spec=pltpu.PrefetchScalarGridSpec(
            num_scalar_prefetch=2,
            grid=(p // tm, nj),
            in_specs=[pl.BlockSpec((tm, d), lambda i, j, be, ns: (i, 0)),
                      pl.BlockSpec((None, d, tf), lambda i, j, be, ns: (be[i], 0, jj(i, j, ns))),
                      pl.BlockSpec((None, d, tf), lambda i, j, be, ns: (be[i], 0, jj(i, j, ns))),
                      pl.BlockSpec((None, tf, d), lambda i, j, be, ns: (be[i], jj(i, j, ns), 0))],
            out_specs=pl.BlockSpec(memory_space=pl.ANY),
            scratch_shapes=[pltpu.VMEM((tm, d), F32), pltpu.SemaphoreType.DMA((2,))]),
        compiler_params=_params(("arbitrary", "arbitrary")),
        name=name,
    )(block_e, block_nsub, x, w1, w3, w2)


def _route_kernel(x_ref, w_ref, tri_ref, meta_ref, gate_ref, cnt_ref, run, *, n_experts):
    @pl.when(pl.program_id(0) == 0)
    def _():
        run[...] = jnp.zeros_like(run)

    x = x_ref[...]
    w = w_ref[...]
    xh = x.astype(BF16)
    wh = w.astype(BF16)
    xl = (x - xh.astype(F32)).astype(BF16)
    wl = (w - wh.astype(F32)).astype(BF16)
    logits = (jnp.dot(xh, wh, preferred_element_type=F32) + jnp.dot(xl, wh, preferred_element_type=F32)
              + jnp.dot(xh, wl, preferred_element_type=F32))
    lane_i = lax.broadcasted_iota(jnp.int32, logits.shape, 1)
    lane = lane_i.astype(F32)
    neg = jnp.float32(-jnp.inf)
    logits = jnp.where(lane_i < n_experts, logits, neg)
    m1 = jnp.max(logits, axis=-1, keepdims=True)
    i1 = jnp.min(jnp.where(logits == m1, lane, float(LANES)), axis=-1, keepdims=True)
    rest = jnp.where(lane == i1, neg, logits)
    m2 = jnp.max(rest, axis=-1, keepdims=True)
    i2 = jnp.min(jnp.where(rest == m2, lane, float(LANES)), axis=-1, keepdims=True)
    e2 = jnp.exp(m2 - m1)
    den = 1.0 + e2
    pick1 = jnp.where(lane == i1, 1.0, 0.0)
    pick2 = jnp.where(lane == i2, 1.0, 0.0)
    picks = pick1 + pick2
    before = run[...] + jnp.dot(tri_ref[...], picks.astype(BF16), preferred_element_type=F32)
    r1 = jnp.sum(before * pick1, axis=-1, keepdims=True)
    r2 = jnp.sum(before * pick2, axis=-1, keepdims=True)
    run[...] += jnp.sum(picks, axis=0, keepdims=True)
    cnt_ref[...] = run[...].astype(jnp.int32)
    meta = jnp.where(lane_i == 0, i1, jnp.where(lane_i == 1, i2, jnp.where(
        lane_i == 2, r1, jnp.where(lane_i == 3, r2, 0.0))))
    meta_ref[...] = meta.T[:META_ROWS].astype(jnp.int32)
    gate_ref[...] = jnp.where(lane_i == 0, 1.0 / den, jnp.where(lane_i == 1, e2 / den, 0.0))


def _route(x, w_router):
    n, d = x.shape
    e = w_router.shape[1]
    tm = min(ROUTE_ROWS, n)
    wpad = jnp.zeros((d, LANES), F32).at[:, :e].set(w_router)
    t = jnp.arange(tm)
    tri = (t[:, None] > t[None, :]).astype(BF16)
    meta, gate, cnt = pl.pallas_call(
        functools.partial(_route_kernel, n_experts=e),
        out_shape=(jax.ShapeDtypeStruct((META_ROWS, n), jnp.int32), jax.ShapeDtypeStruct((n, LANES), F32),
                   jax.ShapeDtypeStruct((1, LANES), jnp.int32)),
        grid=(n // tm,),
        in_specs=[pl.BlockSpec((tm, d), lambda i: (i, 0)), pl.BlockSpec((d, LANES), lambda i: (0, 0)),
                  pl.BlockSpec((tm, tm), lambda i: (0, 0))],
        out_specs=(pl.BlockSpec((META_ROWS, tm), lambda i: (0, i)), pl.BlockSpec((tm, LANES), lambda i: (i, 0)),
                   pl.BlockSpec((1, LANES), lambda i: (0, 0))),
        scratch_shapes=[pltpu.VMEM((1, LANES), F32)],
        compiler_params=_params(("arbitrary",)),
        name="route_top2",
    )(x, wpad, tri)
    return meta, gate, cnt[0, :e]


def _gather_kernel(tok_ref, ns_ref, x_hbm, o_ref, buf, sem):
    s = pl.program_id(0)
    per = FFN_ROWS // GATHER_ROWS

    def nonempty(step):
        return (step % per) * GATHER_ROWS < ns_ref[step // per] * FFN_SUB

    def issue(step):
        slot = step % 2
        base = step * GATHER_ROWS

        def row(r, carry):
            t = tok_ref[base + r]
            pltpu.make_async_copy(x_hbm.at[pl.ds(t, 1), :], buf.at[slot, pl.ds(r, 1), :], sem.at[slot]).start()
            return carry

        lax.fori_loop(0, GATHER_ROWS, row, 0, unroll=8)

    @pl.when(jnp.logical_and(s == 0, nonempty(0)))
    def _():
        issue(0)

    nxt = jnp.minimum(s + 1, pl.num_programs(0) - 1)

    @pl.when(jnp.logical_and(s + 1 < pl.num_programs(0), nonempty(nxt)))
    def _():
        issue(nxt)

    @pl.when(nonempty(s))
    def _():
        slot = s % 2
        pltpu.make_async_copy(x_hbm.at[pl.ds(0, GATHER_ROWS), :], buf.at[slot], sem.at[slot]).wait()
        o_ref[...] = buf[slot].astype(o_ref.dtype)

    @pl.when(jnp.logical_not(nonempty(s)))
    def _():
        o_ref[...] = jnp.zeros_like(o_ref)


def _gather_rows(x, slot_tok, block_nsub, p):
    n, d = x.shape
    return pl.pallas_call(
        _gather_kernel,
        out_shape=jax.ShapeDtypeStruct((p, d), BF16),
        grid_spec=pltpu.PrefetchScalarGridSpec(
            num_scalar_prefetch=2,
            grid=(p // GATHER_ROWS,),
            in_specs=[pl.BlockSpec(memory_space=pl.ANY)],
            out_specs=pl.BlockSpec((GATHER_ROWS, d), lambda s, tok, ns: (s, 0)),
            scratch_shapes=[pltpu.VMEM((2, GATHER_ROWS, d), F32), pltpu.SemaphoreType.DMA((2,))]),
        compiler_params=_params(("arbitrary",)),
        name="moe_gather",
    )(slot_tok, block_nsub, x)


def _combine_kernel(p0_ref, p1_ref, y_hbm, x_ref, ple_ref, gate_ref, lg_ref, lb_ref,
                    o_ref, ob_ref, buf, sem):
    i = pl.program_id(0)
    tm = x_ref.shape[0]

    def issue(tile):
        slot = tile % 2
        base = tile * tm

        def row(r, carry):
            pltpu.make_async_copy(y_hbm.at[pl.ds(p0_ref[base + r], 1), :], buf.at[slot, 0, pl.ds(r, 1), :],
                                  sem.at[slot]).start()
            pltpu.make_async_copy(y_hbm.at[pl.ds(p1_ref[base + r], 1), :], buf.at[slot, 1, pl.ds(r, 1), :],
                                  sem.at[slot]).start()
            return carry

        lax.fori_loop(0, tm, row, 0, unroll=4)

    @pl.when(i == 0)
    def _():
        issue(0)

    @pl.when(i + 1 < pl.num_programs(0))
    def _():
        issue(i + 1)

    slot = i % 2
    pltpu.make_async_copy(y_hbm.at[pl.ds(0, tm), :], buf.at[slot, 0], sem.at[slot]).wait()
    pltpu.make_async_copy(y_hbm.at[pl.ds(0, tm), :], buf.at[slot, 1], sem.at[slot]).wait()
    gate = gate_ref[...]
    f = gate[:, 0:1] * buf[slot, 0] + gate[:, 1:2] * buf[slot, 1]
    out = _ln_body(DEEPNORM_ALPHA * x_ref[...] + f + ple_ref[...].astype(F32), lg_ref[...], lb_ref[...])
    o_ref[...] = out
    ob_ref[...] = out.astype(BF16)


def _combine_ln(y, pos0, pos1, gate, x, ple, ln_g, ln_b):
    n, d = x.shape
    tm = min(LN_ROWS, n)
    tile = lambda: pl.BlockSpec((tm, d), lambda i, a, b: (i, 0))
    row = lambda: pl.BlockSpec((1, d), lambda i, a, b: (0, 0))
    return pl.pallas_call(
        _combine_kernel,
        out_shape=(jax.ShapeDtypeStruct((n, d), F32), jax.ShapeDtypeStruct((n, d), BF16)),
        grid_spec=pltpu.PrefetchScalarGridSpec(
            num_scalar_prefetch=2,
            grid=(n // tm,),
            in_specs=[pl.BlockSpec(memory_space=pl.ANY), tile(), tile(),
                      pl.BlockSpec((tm, LANES), lambda i, a, b: (i, 0)), row(), row()],
            out_specs=(tile(), tile()),
            scratch_shapes=[pltpu.VMEM((2, 2, tm, d), F32), pltpu.SemaphoreType.DMA((2,))]),
        compiler_params=_params(("arbitrary",)),
        name="moe_combine_ln",
    )(pos0, pos1, y, x, ple, gate, ln_g.reshape(1, d), ln_b.reshape(1, d))


def _moe_plan(meta, counts, n_experts):
    n = meta.shape[1]
    nk = n * TOP_K
    nblk = (nk - n_experts) // FFN_ROWS + n_experts
    sub_per_blk = FFN_ROWS // FFN_SUB
    expert = meta[:TOP_K]
    rank = meta[TOP_K:2 * TOP_K]
    nsub_e = (counts + FFN_SUB - 1) // FFN_SUB
    nblk_e = (nsub_e + sub_per_blk - 1) // sub_per_blk
    rows_e = jnp.maximum((nsub_e + nblk_e - 1) // jnp.maximum(nblk_e, 1), 1) * FFN_SUB
    blk_end = jnp.cumsum(nblk_e)
    blk_off = blk_end - nblk_e
    onehot = expert[None] == jnp.arange(n_experts, dtype=jnp.int32)[:, None, None]
    rpb = jnp.sum(jnp.where(onehot, rows_e[:, None, None], 0), axis=0)
    first = jnp.sum(jnp.where(onehot, blk_off[:, None, None], 0), axis=0)
    pos = (first + rank // rpb) * FFN_ROWS + rank % rpb
    blk = jnp.arange(nblk, dtype=jnp.int32)
    used = blk < blk_end[-1]
    block_e = jnp.clip(jnp.searchsorted(blk_end, blk, side="right"), 0, n_experts - 1).astype(jnp.int32)
    block_e = jnp.where(used, block_e, block_e[jnp.maximum(blk_end[-1] - 1, 0)])
    rows = jnp.clip(counts[block_e] - (blk - blk_off[block_e]) * rows_e[block_e], 0, rows_e[block_e])
    rows = jnp.where(used, rows, 0)
    block_nsub = ((rows + FFN_SUB - 1) // FFN_SUB).astype(jnp.int32)
    tok = jnp.tile(jnp.arange(n, dtype=jnp.int32), TOP_K)
    slot_tok = jnp.zeros((nblk * FFN_ROWS,), jnp.int32).at[pos.reshape(nk)].set(
        tok, unique_indices=True)
    return pos.astype(jnp.int32), slot_tok, block_e, block_nsub, nblk * FFN_ROWS


def _token_mixer(xb, layer, w_in, w_br_ret, w_br_sb, w_gate, b_gate, batch, seq):
    d = xb.shape[1]
    in_width = w_in.shape[2]
    rw = RET_HEADS * HEAD_DIM
    sw = SB_HEADS * HEAD_DIM
    col_scale = jnp.ones((in_width,), F32)
    col_scale = col_scale.at[rw:2 * rw].set(HEAD_DIM ** -0.5)
    col_scale = col_scale.at[4 * rw:4 * rw + sw].set(HEAD_DIM ** -0.5 * math.log2(math.e))
    h = _mm([xb[None]], [(0, w_in, 0)], [(col_scale.reshape(1, 1, in_width), 0)], layer, _ep_colscale,
            in_width, BF16, 2048, 1024, "in_proj")
    o_ret = _retention(h, batch, seq)
    o_sb = _stick_breaking(h, batch, seq, (4 * RET_HEADS) // SB_HEADS)
    tn = min(512, d)
    bg = b_gate.reshape(b_gate.shape[0], 1, 2 * d)
    return _mm([o_ret[None], o_sb[None], xb[None]],
               [(0, w_br_ret, 0), (1, w_br_sb, 0), (2, w_gate, 0), (2, w_gate, d // tn)],
               [(bg, 0), (bg, d // tn)], layer, _ep_merge, d, BF16, 512, tn, "branch_merge")


def kernel(x, p, w_in, w_br_ret, w_br_sb, w_gate, b_gate, w_o, ln1_g, ln1_b,
           ffn_w1, ffn_w3, ffn_w2, moe_router, moe_w1, moe_w3, moe_w2,
           ple_w, ple_gate_w, ln2_g, ln2_b):
    batch, seq, d = x.shape
    n = batch * seq
    depth = w_in.shape[0]
    n_experts = moe_router.shape[2]
    xf = x.reshape(n, d)
    xb = xf.astype(BF16)
    pf = p.reshape(depth, n, p.shape[3])
    ew1 = moe_w1.reshape((-1,) + moe_w1.shape[2:])
    ew3 = moe_w3.reshape((-1,) + moe_w3.shape[2:])
    ew2 = moe_w2.reshape((-1,) + moe_w2.shape[2:])
    for i in range(depth):
        merged = _token_mixer(xb, i, w_in, w_br_ret, w_br_sb, w_gate, b_gate, batch, seq)
        xf, xb = _proj_ln(merged, w_o, i, xf, ln1_g[i], ln1_b[i], "out_proj_ln")
        ple = _mm([xb[None], pf], [(0, ple_gate_w, 0), (1, ple_w, 0)], [], i, _ep_ple, d, BF16, 1024, 1024, "ple")
        if i % 2 == 0:
            nblk = n // min(FFN_ROWS, n)
            f = _ffn(xb, ffn_w1, ffn_w3, ffn_w2, jnp.full((nblk,), i // 2, jnp.int32),
                     jnp.full((nblk,), FFN_ROWS // FFN_SUB, jnp.int32), "dense_swiglu", all_rows=True)
            xf, xb = _ln(xf, [f, ple], ln2_g[i], ln2_b[i], "ln_ffn")
        else:
            meta, gate, counts = _route(xf, moe_router[i // 2])
            pos, slot_tok, block_e, block_nsub, slots = _moe_plan(meta, counts, n_experts)
            xs = _gather_rows(xf, slot_tok, block_nsub, slots)
            y = _ffn(xs, ew1, ew3, ew2, block_e + (i // 2) * n_experts, block_nsub, "expert_swiglu")
            xf, xb = _combine_ln(y, pos[0], pos[1], gate, xf, ple, ln2_g[i], ln2_b[i])
    return xf.reshape(batch, seq, d)
```

```python
import functools
import math

import jax
import jax.numpy as jnp
from jax import lax
from jax.experimental import pallas as pl
from jax.experimental.pallas import tpu as pltpu

F32 = jnp.float32
BF16 = jnp.bfloat16

RET_HEADS = 8
SB_HEADS = 8
HEAD_DIM = 128
RET_CHUNK = 128
ROPE_BASE = 10000.0
TOP_K = 2
DEPTH = 2
DEEPNORM_ALPHA = (2 * DEPTH) ** 0.25
LN_EPS = 1e-5

V7X_VMEM_BYTES = 64 * 1024 * 1024
VMEM_LIMIT = V7X_VMEM_BYTES - 8 * 1024 * 1024
LANES = 128

RET_STEP_CHUNKS = 2
MM_EPILOGUE_COLS = 256
SB_BLOCK = 128
PROJ_LN_ROWS = 512
FFN_ROWS = 2048
FFN_SUB = 128
FFN_CHUNK = 1024
FFN_TF = 256
LN_ROWS = 256
ROUTE_ROWS = 512
META_ROWS = 8
GATHER_ROWS = 256


def _params(sem):
    return pltpu.CompilerParams(dimension_semantics=sem, vmem_limit_bytes=VMEM_LIMIT)


def _mm_kernel(*refs, a_of, n_a, n_extra, epilogue):
    n_prod = len(a_of)
    a_refs = refs[:n_a]
    b_refs = refs[n_a:n_a + n_prod]
    e_refs = refs[n_a + n_prod:n_a + n_prod + n_extra]
    o_ref = refs[n_a + n_prod + n_extra]
    b_scr = refs[n_a + n_prod + n_extra + 1:]

    @pl.when(pl.program_id(1) == 0)
    def _():
        for b_ref, s in zip(b_refs, b_scr):
            s[...] = b_ref[...].astype(BF16)

    a_vals = [a[...].astype(BF16) for a in a_refs]
    tn = o_ref.shape[1]
    cw = min(MM_EPILOGUE_COLS, tn)
    for c in range(tn // cw):
        cols = slice(c * cw, (c + 1) * cw)
        accs = [jnp.dot(a_vals[ai], s[:, cols], preferred_element_type=F32) for ai, s in zip(a_of, b_scr)]
        o_ref[:, cols] = epilogue(accs, [e[:, cols] for e in e_refs]).astype(o_ref.dtype)


def _mm(a_ops, products, extras, layer, epilogue, n_out, out_dtype, tm, tn, name, single_buffer_b=False):
    b_mode = dict(pipeline_mode=pl.Buffered(1)) if single_buffer_b else {}
    m = a_ops[0].shape[1]
    tm = min(tm, m)
    tn = min(tn, n_out)
    assert m % tm == 0 and n_out % tn == 0
    in_specs, args, scratch = [], [], []
    for a in a_ops:
        la = layer if a.shape[0] > 1 else 0
        in_specs.append(pl.BlockSpec((None, tm, a.shape[2]), lambda j, i, la=la: (la, i, 0)))
        args.append(a)
    for _, b, off in products:
        in_specs.append(pl.BlockSpec((None, b.shape[1], tn), lambda j, i, off=off: (layer, 0, j + off),
                                     **b_mode))
        args.append(b)
        scratch.append(pltpu.VMEM((b.shape[1], tn), BF16))
    for e, off in extras:
        le = layer if e.shape[0] > 1 else 0
        in_specs.append(pl.BlockSpec((None, 1, tn), lambda j, i, off=off, le=le: (le, 0, j + off)))
        args.append(e)
    kern = functools.partial(_mm_kernel, a_of=tuple(ai for ai, _, _ in products), n_a=len(a_ops),
                             n_extra=len(extras), epilogue=epilogue)
    return pl.pallas_call(
        kern,
        out_shape=jax.ShapeDtypeStruct((m, n_out), out_dtype),
        grid=(n_out // tn, m // tm),
        in_specs=in_specs,
        out_specs=pl.BlockSpec((tm, tn), lambda j, i: (i, j)),
        scratch_shapes=scratch,
        compiler_params=_params(("arbitrary", "arbitrary")),
        name=name,
    )(*args)


def _ep_colscale(accs, extras):
    return accs[0] * extras[0]


def _ep_merge(accs, extras):
    o_r, o_s, z_r, z_s = accs
    b_r, b_s = extras
    return jax.nn.sigmoid(z_r + b_r) * o_r + jax.nn.sigmoid(z_s + b_s) * o_s


def _ep_ple(accs, extras):
    return jax.nn.sigmoid(accs[0]) * accs[1]


def _ln_body(y, g, b):
    mu = jnp.mean(y, axis=-1, keepdims=True)
    yc = y - mu
    var = jnp.mean(yc * yc, axis=-1, keepdims=True)
    return yc * lax.rsqrt(var + LN_EPS) * g + b


def _ln_kernel(*refs, n_add):
    x_ref = refs[0]
    add_refs = refs[1:1 + n_add]
    g_ref, b_ref, o_ref, ob_ref = refs[1 + n_add:]
    y = DEEPNORM_ALPHA * x_ref[...]
    for a in add_refs:
        y = y + a[...].astype(F32)
    out = _ln_body(y, g_ref[...], b_ref[...])
    o_ref[...] = out
    ob_ref[...] = out.astype(BF16)


def _ln(x, adds, g, b, name):
    n, d = x.shape
    tm = min(2 * LN_ROWS, n)
    tile = pl.BlockSpec((tm, d), lambda i: (i, 0))
    row = pl.BlockSpec((1, d), lambda i: (0, 0))
    return pl.pallas_call(
        functools.partial(_ln_kernel, n_add=len(adds)),
        out_shape=(jax.ShapeDtypeStruct((n, d), F32), jax.ShapeDtypeStruct((n, d), BF16)),
        grid=(n // tm,),
        in_specs=[tile] * (1 + len(adds)) + [row, row],
        out_specs=(tile, tile),
        compiler_params=_params(("arbitrary",)),
        name=name,
    )(x, *adds, g.reshape(1, d), b.reshape(1, d))


def _proj_ln_kernel(a_ref, w_ref, x_ref, g_ref, b_ref, o_ref, ob_ref, w_scr):
    @pl.when(pl.program_id(0) == 0)
    def _():
        w_scr[...] = w_ref[...].astype(BF16)

    mix = jnp.dot(a_ref[...], w_scr[...], preferred_element_type=F32)
    out = _ln_body(DEEPNORM_ALPHA * x_ref[...] + mix, g_ref[...], b_ref[...])
    o_ref[...] = out
    ob_ref[...] = out.astype(BF16)


def _proj_ln(a, w, layer, x, g, b, name):
    n, d = x.shape
    k = a.shape[1]
    tm = min(PROJ_LN_ROWS, n)
    tile = pl.BlockSpec((tm, d), lambda i: (i, 0))
    row = pl.BlockSpec((1, d), lambda i: (0, 0))
    return pl.pallas_call(
        _proj_ln_kernel,
        out_shape=(jax.ShapeDtypeStruct((n, d), F32), jax.ShapeDtypeStruct((n, d), BF16)),
        grid=(n // tm,),
        in_specs=[pl.BlockSpec((tm, k), lambda i: (i, 0)),
                  pl.BlockSpec((None, k, d), lambda i: (layer, 0, 0), pipeline_mode=pl.Buffered(1)),
                  tile, row, row],
        out_specs=(tile, tile),
        scratch_shapes=[pltpu.VMEM((k, d), BF16)],
        compiler_params=_params(("arbitrary",)),
        name=name,
    )(a, w, x, g.reshape(1, d), b.reshape(1, d))


def _dot_nt(a, b):
    return lax.dot_general(a, b, (((1,), (1,)), ((), ())), preferred_element_type=F32)


def _dot_tn(a, b):
    return lax.dot_general(a, b, (((0,), (0,)), ((), ())), preferred_element_type=F32)


def _ret_kernel(q_ref, k_ref, v_ref, g_ref, cos_ref, sin_ref, decay_ref, xi_ref, zeta_ref,
                o_ref, r_scr, *, chunk_decay):
    d = HEAD_DIM

    @pl.when(pl.program_id(1) == 0)
    def _():
        r_scr[...] = jnp.zeros_like(r_scr)

    c = RET_CHUNK
    pairs = [(slice(ci * c, (ci + 1) * c), h, slice(h * d, (h + 1) * d))
             for ci in range(RET_STEP_CHUNKS) for h in range(RET_HEADS)]
    qbs, kbs, kzs = [], [], []
    for rows, h, sl in pairs:
        cos = cos_ref[rows, :]
        sin = sin_ref[rows, :]

        def rot(t):
            return t * cos + pltpu.roll(t, d // 2, 1) * sin

        kr = rot(k_ref[rows, sl].astype(F32))
        qbs.append(rot(q_ref[rows, sl].astype(F32)).astype(BF16))
        kbs.append(kr.astype(BF16))
        kzs.append((kr * zeta_ref[h]).astype(BF16))
    inners = [_dot_nt(qb, kb) for qb, kb in zip(qbs, kbs)]
    kvs = [_dot_tn(kz, v_ref[rows, sl]) for kz, (rows, h, sl) in zip(kzs, pairs)]
    states = [None] * len(pairs)
    for h in range(RET_HEADS):
        r = r_scr[h]
        for ci in range(RET_STEP_CHUNKS):
            states[ci * RET_HEADS + h] = r
            r = r * chunk_decay[h] + kvs[ci * RET_HEADS + h]
        r_scr[h] = r
    crosses = [jnp.dot(qb, r.astype(BF16), preferred_element_type=F32) for qb, r in zip(qbs, states)]
    pbs = [(inner * decay_ref[h]).astype(BF16) for inner, (rows, h, sl) in zip(inners, pairs)]
    outs = [jnp.dot(pb, v_ref[rows, sl], preferred_element_type=F32) for pb, (rows, h, sl) in zip(pbs, pairs)]
    for out, cross, (rows, h, sl) in zip(outs, crosses, pairs):
        o = out + cross * xi_ref[h]
        mu = jnp.mean(o, axis=-1, keepdims=True)
        oc = o - mu
        var = jnp.mean(oc * oc, axis=-1, keepdims=True)
        g = g_ref[rows, sl].astype(F32)
        o_ref[rows, sl] = (oc * lax.rsqrt(var + LN_EPS) * (g * jax.nn.sigmoid(g))).astype(o_ref.dtype)


def _retention(h, batch, seq):
    n = h.shape[0]
    c = RET_CHUNK
    d = HEAD_DIM
    w = RET_HEADS * d
    half = d // 2
    pos = jnp.arange(seq, dtype=F32)
    inv = ROPE_BASE ** (-jnp.arange(half, dtype=F32) / half)
    ang = pos[:, None] * inv[None, :]
    cos = jnp.concatenate([jnp.cos(ang), jnp.cos(ang)], axis=1)
    sin = jnp.concatenate([-jnp.sin(ang), jnp.sin(ang)], axis=1)
    gamma = 1.0 - jnp.exp2(-5.0 - jnp.arange(RET_HEADS, dtype=F32))
    lg = jnp.log(gamma)
    idx = jnp.arange(c, dtype=F32)
    diff = idx[:, None] - idx[None, :]
    causal = diff >= 0
    decay = jnp.where(causal[None], jnp.exp(jnp.where(causal, diff, 0.0)[None] * lg[:, None, None]), 0.0)
    xi = jnp.broadcast_to(jnp.exp((idx + 1.0)[None, :] * lg[:, None])[:, :, None], (RET_HEADS, c, d))
    zeta = jnp.broadcast_to(jnp.exp((c - 1.0 - idx)[None, :] * lg[:, None])[:, :, None], (RET_HEADS, c, d))
    chunk_decay = tuple(math.exp(c * math.log(1.0 - 2.0 ** (-5.0 - hh))) for hh in range(RET_HEADS))

    rows = RET_STEP_CHUNKS * c
    ns = seq // rows
    assert seq % rows == 0

    def col(j):
        return pl.BlockSpec((rows, w), lambda b, t, j=j: (b * ns + t, j))

    tab = pl.BlockSpec((rows, d), lambda b, t: (t, 0))
    hconst = pl.BlockSpec((RET_HEADS, c, d), lambda b, t: (0, 0, 0))
    return pl.pallas_call(
        functools.partial(_ret_kernel, chunk_decay=chunk_decay),
        out_shape=jax.ShapeDtypeStruct((n, w), BF16),
        grid=(batch, ns),
        in_specs=[col(0), col(1), col(2), col(3), tab, tab,
                  pl.BlockSpec((RET_HEADS, c, c), lambda b, t: (0, 0, 0)), hconst, hconst],
        out_specs=pl.BlockSpec((rows, w), lambda b, t: (b * ns + t, 0)),
        scratch_shapes=[pltpu.VMEM((RET_HEADS, d, d), F32)],
        compiler_params=_params(("arbitrary", "arbitrary")),
        name="retention",
    )(h, h, h, h, cos, sin, decay, xi, zeta)


def _sb_kernel(q_ref, k_ref, v_ref, u_ref, o_ref, acc, carry):
    blk = SB_BLOCK
    d = HEAD_DIM
    qb = pl.program_id(1)
    acc[...] = jnp.zeros_like(acc)
    carry[...] = jnp.zeros_like(carry)
    row = lax.broadcasted_iota(jnp.int32, (blk, blk), 0)
    col = lax.broadcasted_iota(jnp.int32, (blk, blk), 1)
    heads = [slice(hh * d, (hh + 1) * d) for hh in range(SB_HEADS)]
    sign = jnp.uint32(0x80000000)

    def sweep(kb, nblk, mask):
        keys = pl.ds(pl.multiple_of(kb * blk, blk), nblk * blk)
        zs = [_dot_nt(q_ref[:, sl], k_ref[keys, sl]) for sl in heads]
        log_betas, hilos = [], []
        for z in zs:
            neg_abs = lax.bitcast_convert_type(lax.bitcast_convert_type(z, jnp.uint32) | sign, F32)
            log_beta = jnp.minimum(z, 0.0) - jnp.log2(1.0 + jnp.exp2(neg_abs))
            log_fail = log_beta - z
            if mask is not None:
                log_fail = jnp.where(mask, log_fail, 0.0)
            hi = log_fail.astype(BF16)
            lo = (log_fail - hi.astype(F32)).astype(BF16)
            log_betas.append(log_beta)
            for c in range(nblk):
                cols = slice(c * blk, (c + 1) * blk)
                hilos.append(jnp.concatenate([hi[:, cols], lo[:, cols]], axis=1))
        s_all = jnp.dot(jnp.concatenate(hilos, axis=0), u_ref[...], preferred_element_type=F32)
        ws = []
        for hh, sl in enumerate(heads):
            run = carry[:, sl]
            parts = [None] * nblk
            for c in reversed(range(nblk)):
                s = s_all[(hh * nblk + c) * blk:(hh * nblk + c + 1) * blk]
                parts[c] = log_betas[hh][:, c * blk:(c + 1) * blk] + run + s[:, :blk]
                run = run + s[:, blk:]
            carry[:, sl] = run
            w = jnp.exp2(parts[0] if nblk == 1 else jnp.concatenate(parts, axis=1))
            if mask is not None:
                w = jnp.where(mask, w, 0.0)
            ws.append(w.astype(BF16))
        for w, sl in zip(ws, heads):
            acc[:, sl] += jnp.dot(w, v_ref[keys, sl], preferred_element_type=F32)

    @pl.when(qb % 2 == 0)
    def _():
        sweep(qb, 1, col < row)

    @pl.when(qb % 2 == 1)
    def _():
        row2 = lax.broadcasted_iota(jnp.int32, (blk, 2 * blk), 0)
        col2 = lax.broadcasted_iota(jnp.int32, (blk, 2 * blk), 1)
        sweep(qb - 1, 2, col2 < row2 + blk)

    @pl.when((qb // 2) % 2 == 1)
    def _():
        sweep((qb // 4) * 4, 2, None)

    def body(i, c):
        sweep((qb // 4 - 1 - i) * 4, 4, None)
        return c

    lax.fori_loop(0, qb // 4, body, 0)
    o_ref[...] = acc[...].astype(o_ref.dtype)


def _stick_breaking(h, batch, seq, col0):
    n = h.shape[0]
    blk = SB_BLOCK
    w = SB_HEADS * HEAD_DIM
    nq = seq // blk
    idx = jnp.arange(blk)
    tri = (idx[:, None] > idx[None, :]).astype(BF16)
    half = jnp.concatenate([tri, jnp.ones((blk, blk), BF16)], axis=1)
    u = jnp.concatenate([half, half], axis=0)
    return pl.pallas_call(
        _sb_kernel,
        out_shape=jax.ShapeDtypeStruct((n, w), BF16),
        grid=(batch, nq),
        in_specs=[pl.BlockSpec((blk, w), lambda b, t: (b * nq + t, col0)),
                  pl.BlockSpec((seq, w), lambda b, t: (b, col0 + 1)),
                  pl.BlockSpec((seq, w), lambda b, t: (b, col0 + 2)),
                  pl.BlockSpec((2 * blk, 2 * blk), lambda b, t: (0, 0))],
        out_specs=pl.BlockSpec((blk, w), lambda b, t: (b * nq + t, 0)),
        scratch_shapes=[pltpu.VMEM((blk, w), F32), pltpu.VMEM((blk, w), F32)],
        compiler_params=_params(("arbitrary", "arbitrary")),
        name="stick_breaking",
    )(h, h, h, u)


def _ffn_kernel(be_ref, ns_ref, x_ref, w1_ref, w3_ref, w2_ref, y_hbm, acc, sem, *, all_rows):
    i = pl.program_id(0)
    j = pl.program_id(1)
    tm = x_ref.shape[0]
    nsub = ns_ref[i]

    @pl.when(jnp.logical_and(i == 0, j == 0))
    def _():
        acc[...] = jnp.zeros_like(acc)

    def chunk(start, size):
        rows = pl.ds(start, size)
        x = x_ref[rows, :]
        a = jnp.dot(x, w1_ref[...].astype(BF16), preferred_element_type=F32)
        b = jnp.dot(x, w3_ref[...].astype(BF16), preferred_element_type=F32)
        hmid = (a * jax.nn.sigmoid(a) * b).astype(BF16)
        y = jnp.dot(hmid, w2_ref[...].astype(BF16), preferred_element_type=F32)
        acc[rows, :] = jnp.where(j == 0, y, acc[rows, :] + y)

    last = j == pl.num_programs(1) - 1
    half = tm // 2

    def copy_out(h):
        return pltpu.make_async_copy(acc.at[pl.ds(h * half, half), :],
                                     y_hbm.at[pl.ds(pl.multiple_of(i * tm + h * half, half), half), :], sem.at[h])

    if all_rows:
        step = min(FFN_CHUNK // 2, half)
        for s in range(tm // step):
            chunk(s * step, step)
            if (s + 1) * step == half:
                @pl.when(last)
                def _():
                    copy_out(0).start()
    else:
        assert half == FFN_CHUNK and FFN_CHUNK == 8 * FFN_SUB
        nfull = nsub // 8

        def full_body(s, carry):
            chunk(pl.multiple_of(s * FFN_CHUNK, FFN_CHUNK), FFN_CHUNK)

            @pl.when(jnp.logical_and(last, s == 0))
            def _():
                copy_out(0).start()

            return carry

        lax.fori_loop(0, nfull, full_body, 0)
        rem = nsub % 8
        off = nfull * FFN_CHUNK
        for subs in (4, 2, 1):
            size = subs * FFN_SUB

            @pl.when((rem & subs) != 0)
            def _(off=off, size=size):
                chunk(pl.multiple_of(off, size), size)

            off = off + (rem & subs) * FFN_SUB

        @pl.when(jnp.logical_and(last, nfull == 0))
        def _():
            copy_out(0).start()

    @pl.when(last)
    def _():
        copy_out(1).start()
        copy_out(0).wait()
        copy_out(1).wait()


def _ffn(x, w1, w3, w2, block_e, block_nsub, name, all_rows=False):
    p, d = x.shape
    f = w1.shape[2]
    tm = min(FFN_ROWS, p)
    tf = min(FFN_TF, f)
    assert p % tm == 0 and f % tf == 0 and tm % FFN_SUB == 0
    nj = f // tf

    def jj(i, j, ns):
        return jnp.where(ns[i] > 0, j, nj - 1)

    return pl.pallas_call(
        functools.partial(_ffn_kernel, all_rows=all_rows),
        out_shape=jax.ShapeDtypeStruct((p, d), F32),
        grid_spec=pltpu.PrefetchScalarGridSpec(
            num_scalar_prefetch=2,
            grid=(p // tm, nj),
            in_specs=[pl.BlockSpec((tm, d), lambda i, j, be, ns: (i, 0)),
                      pl.BlockSpec((None, d, tf), lambda i, j, be, ns: (be[i], 0, jj(i, j, ns))),
                      pl.BlockSpec((None, d, tf), lambda i, j, be, ns: (be[i], 0, jj(i, j, ns))),
                      pl.BlockSpec((None, tf, d), lambda i, j, be, ns: (be[i], jj(i, j, ns), 0))],
            out_specs=pl.BlockSpec(memory_space=pl.ANY),
            scratch_shapes=[pltpu.VMEM((tm, d), F32), pltpu.SemaphoreType.DMA((2,))]),
        compiler_params=_params(("arbitrary", "arbitrary")),
        name=name,
    )(block_e, block_nsub, x, w1, w3, w2)


def _route_kernel(x_ref, w_ref, tri_ref, meta_ref, gate_ref, cnt_ref, run, *, n_experts):
    @pl.when(pl.program_id(0) == 0)
    def _():
        run[...] = jnp.zeros_like(run)

    x = x_ref[...]
    w = w_ref[...]
    xh = x.astype(BF16)
    wh = w.astype(BF16)
    xl = (x - xh.astype(F32)).astype(BF16)
    wl = (w - wh.astype(F32)).astype(BF16)
    logits = (jnp.dot(xh, wh, preferred_element_type=F32) + jnp.dot(xl, wh, preferred_element_type=F32)
              + jnp.dot(xh, wl, preferred_element_type=F32))
    lane_i = lax.broadcasted_iota(jnp.int32, logits.shape, 1)
    lane = lane_i.astype(F32)
    neg = jnp.float32(-jnp.inf)
    logits = jnp.where(lane_i < n_experts, logits, neg)
    m1 = jnp.max(logits, axis=-1, keepdims=True)
    i1 = jnp.min(jnp.where(logits == m1, lane, float(LANES)), axis=-1, keepdims=True)
    rest = jnp.where(lane == i1, neg, logits)
    m2 = jnp.max(rest, axis=-1, keepdims=True)
    i2 = jnp.min(jnp.where(rest == m2, lane, float(LANES)), axis=-1, keepdims=True)
    e2 = jnp.exp(m2 - m1)
    den = 1.0 + e2
    pick1 = jnp.where(lane == i1, 1.0, 0.0)
    pick2 = jnp.where(lane == i2, 1.0, 0.0)
    picks = pick1 + pick2
    before = run[...] + jnp.dot(tri_ref[...], picks.astype(BF16), preferred_element_type=F32)
    r1 = jnp.sum(before * pick1, axis=-1, keepdims=True)
    r2 = jnp.sum(before * pick2, axis=-1, keepdims=True)
    run[...] += jnp.sum(picks, axis=0, keepdims=True)
    cnt_ref[...] = run[...].astype(jnp.int32)
    meta = jnp.where(lane_i == 0, i1, jnp.where(lane_i == 1, i2, jnp.where(
        lane_i == 2, r1, jnp.where(lane_i == 3, r2, 0.0))))
    meta_ref[...] = meta.T[:META_ROWS].astype(jnp.int32)
    gate_ref[...] = jnp.where(lane_i == 0, 1.0 / den, jnp.where(lane_i == 1, e2 / den, 0.0))


def _route(x, w_router):
    n, d = x.shape
    e = w_router.shape[1]
    tm = min(ROUTE_ROWS, n)
    wpad = jnp.zeros((d, LANES), F32).at[:, :e].set(w_router)
    t = jnp.arange(tm)
    tri = (t[:, None] > t[None, :]).astype(BF16)
    meta, gate, cnt = pl.pallas_call(
        functools.partial(_route_kernel, n_experts=e),
        out_shape=(jax.ShapeDtypeStruct((META_ROWS, n), jnp.int32), jax.ShapeDtypeStruct((n, LANES), F32),
                   jax.ShapeDtypeStruct((1, LANES), jnp.int32)),
        grid=(n // tm,),
        in_specs=[pl.BlockSpec((tm, d), lambda i: (i, 0)), pl.BlockSpec((d, LANES), lambda i: (0, 0)),
                  pl.BlockSpec((tm, tm), lambda i: (0, 0))],
        out_specs=(pl.BlockSpec((META_ROWS, tm), lambda i: (0, i)), pl.BlockSpec((tm, LANES), lambda i: (i, 0)),
                   pl.BlockSpec((1, LANES), lambda i: (0, 0))),
        scratch_shapes=[pltpu.VMEM((1, LANES), F32)],
        compiler_params=_params(("arbitrary",)),
        name="route_top2",
    )(x, wpad, tri)
    return meta, gate, cnt[0, :e]


def _gather_kernel(tok_ref, ns_ref, x_hbm, o_ref, buf, sem):
    s = pl.program_id(0)
    per = FFN_ROWS // GATHER_ROWS

    def nonempty(step):
        return (step % per) * GATHER_ROWS < ns_ref[step // per] * FFN_SUB

    def issue(step):
        slot = step % 2
        base = step * GATHER_ROWS

        def row(r, carry):
            t = tok_ref[base + r]
            pltpu.make_async_copy(x_hbm.at[pl.ds(t, 1), :], buf.at[slot, pl.ds(r, 1), :], sem.at[slot]).start()
            return carry

        lax.fori_loop(0, GATHER_ROWS, row, 0, unroll=8)

    @pl.when(jnp.logical_and(s == 0, nonempty(0)))
    def _():
        issue(0)

    nxt = jnp.minimum(s + 1, pl.num_programs(0) - 1)

    @pl.when(jnp.logical_and(s + 1 < pl.num_programs(0), nonempty(nxt)))
    def _():
        issue(nxt)

    @pl.when(nonempty(s))
    def _():
        slot = s % 2
        pltpu.make_async_copy(x_hbm.at[pl.ds(0, GATHER_ROWS), :], buf.at[slot], sem.at[slot]).wait()
        o_ref[...] = buf[slot].astype(o_ref.dtype)

    @pl.when(jnp.logical_not(nonempty(s)))
    def _():
        o_ref[...] = jnp.zeros_like(o_ref)


def _gather_rows(x, slot_tok, block_nsub, p):
    n, d = x.shape
    return pl.pallas_call(
        _gather_kernel,
        out_shape=jax.ShapeDtypeStruct((p, d), BF16),
        grid_spec=pltpu.PrefetchScalarGridSpec(
            num_scalar_prefetch=2,
            grid=(p // GATHER_ROWS,),
            in_specs=[pl.BlockSpec(memory_space=pl.ANY)],
            out_specs=pl.BlockSpec((GATHER_ROWS, d), lambda s, tok, ns: (s, 0)),
            scratch_shapes=[pltpu.VMEM((2, GATHER_ROWS, d), F32), pltpu.SemaphoreType.DMA((2,))]),
        compiler_params=_params(("arbitrary",)),
        name="moe_gather",
    )(slot_tok, block_nsub, x)


def _combine_kernel(p0_ref, p1_ref, y_hbm, x_ref, ple_ref, gate_ref, lg_ref, lb_ref,
                    o_ref, ob_ref, buf, sem):
    i = pl.program_id(0)
    tm = x_ref.shape[0]

    def issue(tile):
        slot = tile % 2
        base = tile * tm

        def row(r, carry):
            pltpu.make_async_copy(y_hbm.at[pl.ds(p0_ref[base + r], 1), :], buf.at[slot, 0, pl.ds(r, 1), :],
                                  sem.at[slot]).start()
            pltpu.make_async_copy(y_hbm.at[pl.ds(p1_ref[base + r], 1), :], buf.at[slot, 1, pl.ds(r, 1), :],
                                  sem.at[slot]).start()
            return carry

        lax.fori_loop(0, tm, row, 0, unroll=4)

    @pl.when(i == 0)
    def _():
        issue(0)

    @pl.when(i + 1 < pl.num_programs(0))
    def _():
        issue(i + 1)

    slot = i % 2
    pltpu.make_async_copy(y_hbm.at[pl.ds(0, tm), :], buf.at[slot, 0], sem.at[slot]).wait()
    pltpu.make_async_copy(y_hbm.at[pl.ds(0, tm), :], buf.at[slot, 1], sem.at[slot]).wait()
    gate = gate_ref[...]
    f = gate[:, 0:1] * buf[slot, 0] + gate[:, 1:2] * buf[slot, 1]
    out = _ln_body(DEEPNORM_ALPHA * x_ref[...] + f + ple_ref[...].astype(F32), lg_ref[...], lb_ref[...])
    o_ref[...] = out
    ob_ref[...] = out.astype(BF16)


def _combine_ln(y, pos0, pos1, gate, x, ple, ln_g, ln_b):
    n, d = x.shape
    tm = min(LN_ROWS, n)
    tile = lambda: pl.BlockSpec((tm, d), lambda i, a, b: (i, 0))
    row = lambda: pl.BlockSpec((1, d), lambda i, a, b: (0, 0))
    return pl.pallas_call(
        _combine_kernel,
        out_shape=(jax.ShapeDtypeStruct((n, d), F32), jax.ShapeDtypeStruct((n, d), BF16)),
        grid_spec=pltpu.PrefetchScalarGridSpec(
            num_scalar_prefetch=2,
            grid=(n // tm,),
            in_specs=[pl.BlockSpec(memory_space=pl.ANY), tile(), tile(),
                      pl.BlockSpec((tm, LANES), lambda i, a, b: (i, 0)), row(), row()],
            out_specs=(tile(), tile()),
            scratch_shapes=[pltpu.VMEM((2, 2, tm, d), F32), pltpu.SemaphoreType.DMA((2,))]),
        compiler_params=_params(("arbitrary",)),
        name="moe_combine_ln",
    )(pos0, pos1, y, x, ple, gate, ln_g.reshape(1, d), ln_b.reshape(1, d))


def _moe_plan(meta, counts, n_experts):
    n = meta.shape[1]
    nk = n * TOP_K
    nblk = (nk - n_experts) // FFN_ROWS + n_experts
    sub_per_blk = FFN_ROWS // FFN_SUB
    expert = meta[:TOP_K]
    rank = meta[TOP_K:2 * TOP_K]
    nsub_e = (counts + FFN_SUB - 1) // FFN_SUB
    nblk_e = (nsub_e + sub_per_blk - 1) // sub_per_blk
    rows_e = jnp.maximum((nsub_e + nblk_e - 1) // jnp.maximum(nblk_e, 1), 1) * FFN_SUB
    blk_end = jnp.cumsum(nblk_e)
    blk_off = blk_end - nblk_e
    onehot = expert[None] == jnp.arange(n_experts, dtype=jnp.int32)[:, None, None]
    rpb = jnp.sum(jnp.where(onehot, rows_e[:, None, None], 0), axis=0)
    first = jnp.sum(jnp.where(onehot, blk_off[:, None, None], 0), axis=0)
    pos = (first + rank // rpb) * FFN_ROWS + rank % rpb
    blk = jnp.arange(nblk, dtype=jnp.int32)
    used = blk < blk_end[-1]
    block_e = jnp.clip(jnp.searchsorted(blk_end, blk, side="right"), 0, n_experts - 1).astype(jnp.int32)
    block_e = jnp.where(used, block_e, block_e[jnp.maximum(blk_end[-1] - 1, 0)])
    rows = jnp.clip(counts[block_e] - (blk - blk_off[block_e]) * rows_e[block_e], 0, rows_e[block_e])
    rows = jnp.where(used, rows, 0)
    block_nsub = ((rows + FFN_SUB - 1) // FFN_SUB).astype(jnp.int32)
    tok = jnp.tile(jnp.arange(n, dtype=jnp.int32), TOP_K)
    slot_tok = jnp.zeros((nblk * FFN_ROWS,), jnp.int32).at[pos.reshape(nk)].set(
        tok, unique_indices=True)
    return pos.astype(jnp.int32), slot_tok, block_e, block_nsub, nblk * FFN_ROWS


def _token_mixer(xb, layer, w_in, w_br_ret, w_br_sb, w_gate, b_gate, batch, seq):
    d = xb.shape[1]
    in_width = w_in.shape[2]
    rw = RET_HEADS * HEAD_DIM
    sw = SB_HEADS * HEAD_DIM
    col_scale = jnp.ones((in_width,), F32)
    col_scale = col_scale.at[rw:2 * rw].set(HEAD_DIM ** -0.5)
    col_scale = col_scale.at[4 * rw:4 * rw + sw].set(HEAD_DIM ** -0.5 * math.log2(math.e))
    h = _mm([xb[None]], [(0, w_in, 0)], [(col_scale.reshape(1, 1, in_width), 0)], layer, _ep_colscale,
            in_width, BF16, 2048, 1024, "in_proj")
    o_ret = _retention(h, batch, seq)
    o_sb = _stick_breaking(h, batch, seq, (4 * RET_HEADS) // SB_HEADS)
    tn = min(512, d)
    bg = b_gate.reshape(b_gate.shape[0], 1, 2 * d)
    return _mm([o_ret[None], o_sb[None], xb[None]],
               [(0, w_br_ret, 0), (1, w_br_sb, 0), (2, w_gate, 0), (2, w_gate, d // tn)],
               [(bg, 0), (bg, d // tn)], layer, _ep_merge, d, BF16, 1024, tn, "branch_merge",
               single_buffer_b=True)


def kernel(x, p, w_in, w_br_ret, w_br_sb, w_gate, b_gate, w_o, ln1_g, ln1_b,
           ffn_w1, ffn_w3, ffn_w2, moe_router, moe_w1, moe_w3, moe_w2,
           ple_w, ple_gate_w, ln2_g, ln2_b):
    batch, seq, d = x.shape
    n = batch * seq
    depth = w_in.shape[0]
    n_experts = moe_router.shape[2]
    xf = x.reshape(n, d)
    xb = xf.astype(BF16)
    pf = p.reshape(depth, n, p.shape[3])
    ew1 = moe_w1.reshape((-1,) + moe_w1.shape[2:])
    ew3 = moe_w3.reshape((-1,) + moe_w3.shape[2:])
    ew2 = moe_w2.reshape((-1,) + moe_w2.shape[2:])
    for i in range(depth):
        merged = _token_mixer(xb, i, w_in, w_br_ret, w_br_sb, w_gate, b_gate, batch, seq)
        xf, xb = _proj_ln(merged, w_o, i, xf, ln1_g[i], ln1_b[i], "out_proj_ln")
        ple = _mm([xb[None], pf], [(0, ple_gate_w, 0), (1, ple_w, 0)], [], i, _ep_ple, d, BF16, 1024, 1024, "ple")
        if i % 2 == 0:
            nblk = n // min(FFN_ROWS, n)
            f = _ffn(xb, ffn_w1, ffn_w3, ffn_w2, jnp.full((nblk,), i // 2, jnp.int32),
                     jnp.full((nblk,), FFN_ROWS // FFN_SUB, jnp.int32), "dense_swiglu", all_rows=True)
            xf, xb = _ln(xf, [f, ple], ln2_g[i], ln2_b[i], "ln_ffn")
        else:
            meta, gate, counts = _route(xf, moe_router[i // 2])
            pos, slot_tok, block_e, block_nsub, slots = _moe_plan(meta, counts, n_experts)
            xs = _gather_rows(xf, slot_tok, block_nsub, slots)
            y = _ffn(xs, ew1, ew3, ew2, block_e + (i // 2) * n_experts, block_nsub, "expert_swiglu")
            xf, xb = _combine_ln(y, pos[0], pos[1], gate, xf, ple, ln2_g[i], ln2_b[i])
    return xf.reshape(batch, seq, d)
```

```python
import functools
import math

import jax
import jax.numpy as jnp
from jax import lax
from jax.experimental import pallas as pl
from jax.experimental.pallas import tpu as pltpu

F32 = jnp.float32
BF16 = jnp.bfloat16

RET_HEADS = 8
SB_HEADS = 8
HEAD_DIM = 128
RET_CHUNK = 128
ROPE_BASE = 10000.0
TOP_K = 2
DEPTH = 2
DEEPNORM_ALPHA = (2 * DEPTH) ** 0.25
LN_EPS = 1e-5

V7X_VMEM_BYTES = 64 * 1024 * 1024
VMEM_LIMIT = V7X_VMEM_BYTES - 8 * 1024 * 1024
LANES = 128

RET_STEP_CHUNKS = 2
MM_EPILOGUE_COLS = 256
SB_BLOCK = 128
PROJ_LN_ROWS = 512
FFN_ROWS = 2048
FFN_SUB = 128
FFN_CHUNK = 1024
FFN_TF = 256
LN_ROWS = 256
ROUTE_ROWS = 1024
META_ROWS = 8
GATHER_ROWS = 256


def _params(sem):
    return pltpu.CompilerParams(dimension_semantics=sem, vmem_limit_bytes=VMEM_LIMIT)


def _mm_kernel(*refs, a_of, n_a, n_extra, epilogue):
    n_prod = len(a_of)
    a_refs = refs[:n_a]
    b_refs = refs[n_a:n_a + n_prod]
    e_refs = refs[n_a + n_prod:n_a + n_prod + n_extra]
    o_ref = refs[n_a + n_prod + n_extra]
    b_scr = refs[n_a + n_prod + n_extra + 1:]

    @pl.when(pl.program_id(1) == 0)
    def _():
        for b_ref, s in zip(b_refs, b_scr):
            s[...] = b_ref[...].astype(BF16)

    a_vals = [a[...].astype(BF16) for a in a_refs]
    tn = o_ref.shape[1]
    cw = min(MM_EPILOGUE_COLS, tn)
    for c in range(tn // cw):
        cols = slice(c * cw, (c + 1) * cw)
        accs = [jnp.dot(a_vals[ai], s[:, cols], preferred_element_type=F32) for ai, s in zip(a_of, b_scr)]
        o_ref[:, cols] = epilogue(accs, [e[:, cols] for e in e_refs]).astype(o_ref.dtype)


def _mm(a_ops, products, extras, layer, epilogue, n_out, out_dtype, tm, tn, name, single_buffer_b=False):
    b_mode = dict(pipeline_mode=pl.Buffered(1)) if single_buffer_b else {}
    m = a_ops[0].shape[1]
    tm = min(tm, m)
    tn = min(tn, n_out)
    assert m % tm == 0 and n_out % tn == 0
    in_specs, args, scratch = [], [], []
    for a in a_ops:
        la = layer if a.shape[0] > 1 else 0
        in_specs.append(pl.BlockSpec((None, tm, a.shape[2]), lambda j, i, la=la: (la, i, 0)))
        args.append(a)
    for _, b, off in products:
        in_specs.append(pl.BlockSpec((None, b.shape[1], tn), lambda j, i, off=off: (layer, 0, j + off),
                                     **b_mode))
        args.append(b)
        scratch.append(pltpu.VMEM((b.shape[1], tn), BF16))
    for e, off in extras:
        le = layer if e.shape[0] > 1 else 0
        in_specs.append(pl.BlockSpec((None, 1, tn), lambda j, i, off=off, le=le: (le, 0, j + off)))
        args.append(e)
    kern = functools.partial(_mm_kernel, a_of=tuple(ai for ai, _, _ in products), n_a=len(a_ops),
                             n_extra=len(extras), epilogue=epilogue)
    return pl.pallas_call(
        kern,
        out_shape=jax.ShapeDtypeStruct((m, n_out), out_dtype),
        grid=(n_out // tn, m // tm),
        in_specs=in_specs,
        out_specs=pl.BlockSpec((tm, tn), lambda j, i: (i, j)),
        scratch_shapes=scratch,
        compiler_params=_params(("arbitrary", "arbitrary")),
        name=name,
    )(*args)


def _ep_colscale(accs, extras):
    return accs[0] * extras[0]


def _ep_merge(accs, extras):
    o_r, o_s, z_r, z_s = accs
    b_r, b_s = extras
    return jax.nn.sigmoid(z_r + b_r) * o_r + jax.nn.sigmoid(z_s + b_s) * o_s


def _ep_ple(accs, extras):
    return jax.nn.sigmoid(accs[0]) * accs[1]


def _ln_body(y, g, b):
    mu = jnp.mean(y, axis=-1, keepdims=True)
    yc = y - mu
    var = jnp.mean(yc * yc, axis=-1, keepdims=True)
    return yc * lax.rsqrt(var + LN_EPS) * g + b


def _ln_kernel(*refs, n_add):
    x_ref = refs[0]
    add_refs = refs[1:1 + n_add]
    g_ref, b_ref, o_ref, ob_ref = refs[1 + n_add:]
    y = DEEPNORM_ALPHA * x_ref[...]
    for a in add_refs:
        y = y + a[...].astype(F32)
    out = _ln_body(y, g_ref[...], b_ref[...])
    o_ref[...] = out
    ob_ref[...] = out.astype(BF16)


def _ln(x, adds, g, b, name):
    n, d = x.shape
    tm = min(2 * LN_ROWS, n)
    tile = pl.BlockSpec((tm, d), lambda i: (i, 0))
    row = pl.BlockSpec((1, d), lambda i: (0, 0))
    return pl.pallas_call(
        functools.partial(_ln_kernel, n_add=len(adds)),
        out_shape=(jax.ShapeDtypeStruct((n, d), F32), jax.ShapeDtypeStruct((n, d), BF16)),
        grid=(n // tm,),
        in_specs=[tile] * (1 + len(adds)) + [row, row],
        out_specs=(tile, tile),
        compiler_params=_params(("arbitrary",)),
        name=name,
    )(x, *adds, g.reshape(1, d), b.reshape(1, d))


def _proj_ln_kernel(a_ref, w_ref, x_ref, g_ref, b_ref, o_ref, ob_ref, w_scr):
    @pl.when(pl.program_id(0) == 0)
    def _():
        w_scr[...] = w_ref[...].astype(BF16)

    mix = jnp.dot(a_ref[...], w_scr[...], preferred_element_type=F32)
    out = _ln_body(DEEPNORM_ALPHA * x_ref[...] + mix, g_ref[...], b_ref[...])
    o_ref[...] = out
    ob_ref[...] = out.astype(BF16)


def _proj_ln(a, w, layer, x, g, b, name):
    n, d = x.shape
    k = a.shape[1]
    tm = min(PROJ_LN_ROWS, n)
    tile = pl.BlockSpec((tm, d), lambda i: (i, 0))
    row = pl.BlockSpec((1, d), lambda i: (0, 0))
    return pl.pallas_call(
        _proj_ln_kernel,
        out_shape=(jax.ShapeDtypeStruct((n, d), F32), jax.ShapeDtypeStruct((n, d), BF16)),
        grid=(n // tm,),
        in_specs=[pl.BlockSpec((tm, k), lambda i: (i, 0)),
                  pl.BlockSpec((None, k, d), lambda i: (layer, 0, 0), pipeline_mode=pl.Buffered(1)),
                  tile, row, row],
        out_specs=(tile, tile),
        scratch_shapes=[pltpu.VMEM((k, d), BF16)],
        compiler_params=_params(("arbitrary",)),
        name=name,
    )(a, w, x, g.reshape(1, d), b.reshape(1, d))


def _dot_nt(a, b):
    return lax.dot_general(a, b, (((1,), (1,)), ((), ())), preferred_element_type=F32)


def _dot_tn(a, b):
    return lax.dot_general(a, b, (((0,), (0,)), ((), ())), preferred_element_type=F32)


def _ret_kernel(q_ref, k_ref, v_ref, g_ref, cos_ref, sin_ref, decay_ref, xi_ref, zeta_ref,
                o_ref, r_scr, *, chunk_decay):
    d = HEAD_DIM

    @pl.when(pl.program_id(1) == 0)
    def _():
        r_scr[...] = jnp.zeros_like(r_scr)

    c = RET_CHUNK
    pairs = [(slice(ci * c, (ci + 1) * c), h, slice(h * d, (h + 1) * d))
             for ci in range(RET_STEP_CHUNKS) for h in range(RET_HEADS)]
    qbs, kbs, kzs = [], [], []
    for rows, h, sl in pairs:
        cos = cos_ref[rows, :]
        sin = sin_ref[rows, :]

        def rot(t):
            return t * cos + pltpu.roll(t, d // 2, 1) * sin

        kr = rot(k_ref[rows, sl].astype(F32))
        qbs.append(rot(q_ref[rows, sl].astype(F32)).astype(BF16))
        kbs.append(kr.astype(BF16))
        kzs.append((kr * zeta_ref[h]).astype(BF16))
    inners = [_dot_nt(qb, kb) for qb, kb in zip(qbs, kbs)]
    kvs = [_dot_tn(kz, v_ref[rows, sl]) for kz, (rows, h, sl) in zip(kzs, pairs)]
    states = [None] * len(pairs)
    for h in range(RET_HEADS):
        r = r_scr[h]
        for ci in range(RET_STEP_CHUNKS):
            states[ci * RET_HEADS + h] = r
            r = r * chunk_decay[h] + kvs[ci * RET_HEADS + h]
        r_scr[h] = r
    crosses = [jnp.dot(qb, r.astype(BF16), preferred_element_type=F32) for qb, r in zip(qbs, states)]
    pbs = [(inner * decay_ref[h]).astype(BF16) for inner, (rows, h, sl) in zip(inners, pairs)]
    outs = [jnp.dot(pb, v_ref[rows, sl], preferred_element_type=F32) for pb, (rows, h, sl) in zip(pbs, pairs)]
    for out, cross, (rows, h, sl) in zip(outs, crosses, pairs):
        o = out + cross * xi_ref[h]
        mu = jnp.mean(o, axis=-1, keepdims=True)
        oc = o - mu
        var = jnp.mean(oc * oc, axis=-1, keepdims=True)
        g = g_ref[rows, sl].astype(F32)
        o_ref[rows, sl] = (oc * lax.rsqrt(var + LN_EPS) * (g * jax.nn.sigmoid(g))).astype(o_ref.dtype)


def _retention(h, batch, seq):
    n = h.shape[0]
    c = RET_CHUNK
    d = HEAD_DIM
    w = RET_HEADS * d
    half = d // 2
    pos = jnp.arange(seq, dtype=F32)
    inv = ROPE_BASE ** (-jnp.arange(half, dtype=F32) / half)
    ang = pos[:, None] * inv[None, :]
    cos = jnp.concatenate([jnp.cos(ang), jnp.cos(ang)], axis=1)
    sin = jnp.concatenate([-jnp.sin(ang), jnp.sin(ang)], axis=1)
    gamma = 1.0 - jnp.exp2(-5.0 - jnp.arange(RET_HEADS, dtype=F32))
    lg = jnp.log(gamma)
    idx = jnp.arange(c, dtype=F32)
    diff = idx[:, None] - idx[None, :]
    causal = diff >= 0
    decay = jnp.where(causal[None], jnp.exp(jnp.where(causal, diff, 0.0)[None] * lg[:, None, None]), 0.0)
    xi = jnp.broadcast_to(jnp.exp((idx + 1.0)[None, :] * lg[:, None])[:, :, None], (RET_HEADS, c, d))
    zeta = jnp.broadcast_to(jnp.exp((c - 1.0 - idx)[None, :] * lg[:, None])[:, :, None], (RET_HEADS, c, d))
    chunk_decay = tuple(math.exp(c * math.log(1.0 - 2.0 ** (-5.0 - hh))) for hh in range(RET_HEADS))

    rows = RET_STEP_CHUNKS * c
    ns = seq // rows
    assert seq % rows == 0

    def col(j):
        return pl.BlockSpec((rows, w), lambda b, t, j=j: (b * ns + t, j))

    tab = pl.BlockSpec((rows, d), lambda b, t: (t, 0))
    hconst = pl.BlockSpec((RET_HEADS, c, d), lambda b, t: (0, 0, 0))
    return pl.pallas_call(
        functools.partial(_ret_kernel, chunk_decay=chunk_decay),
        out_shape=jax.ShapeDtypeStruct((n, w), BF16),
        grid=(batch, ns),
        in_specs=[col(0), col(1), col(2), col(3), tab, tab,
                  pl.BlockSpec((RET_HEADS, c, c), lambda b, t: (0, 0, 0)), hconst, hconst],
        out_specs=pl.BlockSpec((rows, w), lambda b, t: (b * ns + t, 0)),
        scratch_shapes=[pltpu.VMEM((RET_HEADS, d, d), F32)],
        compiler_params=_params(("arbitrary", "arbitrary")),
        name="retention",
    )(h, h, h, h, cos, sin, decay, xi, zeta)


def _sb_kernel(q_ref, k_ref, v_ref, u_ref, o_ref, acc, carry):
    blk = SB_BLOCK
    d = HEAD_DIM
    qb = pl.program_id(1)
    acc[...] = jnp.zeros_like(acc)
    carry[...] = jnp.zeros_like(carry)
    row = lax.broadcasted_iota(jnp.int32, (blk, blk), 0)
    col = lax.broadcasted_iota(jnp.int32, (blk, blk), 1)
    heads = [slice(hh * d, (hh + 1) * d) for hh in range(SB_HEADS)]
    sign = jnp.uint32(0x80000000)

    def sweep(kb, nblk, mask):
        keys = pl.ds(pl.multiple_of(kb * blk, blk), nblk * blk)
        zs = [_dot_nt(q_ref[:, sl], k_ref[keys, sl]) for sl in heads]
        log_betas, hilos = [], []
        for z in zs:
            neg_abs = lax.bitcast_convert_type(lax.bitcast_convert_type(z, jnp.uint32) | sign, F32)
            log_beta = jnp.minimum(z, 0.0) - jnp.log2(1.0 + jnp.exp2(neg_abs))
            log_fail = log_beta - z
            if mask is not None:
                log_fail = jnp.where(mask, log_fail, 0.0)
            hi = log_fail.astype(BF16)
            lo = (log_fail - hi.astype(F32)).astype(BF16)
            log_betas.append(log_beta)
            for c in range(nblk):
                cols = slice(c * blk, (c + 1) * blk)
                hilos.append(jnp.concatenate([hi[:, cols], lo[:, cols]], axis=1))
        s_all = jnp.dot(jnp.concatenate(hilos, axis=0), u_ref[...], preferred_element_type=F32)
        ws = []
        for hh, sl in enumerate(heads):
            run = carry[:, sl]
            parts = [None] * nblk
            for c in reversed(range(nblk)):
                s = s_all[(hh * nblk + c) * blk:(hh * nblk + c + 1) * blk]
                parts[c] = log_betas[hh][:, c * blk:(c + 1) * blk] + run + s[:, :blk]
                run = run + s[:, blk:]
            carry[:, sl] = run
            w = jnp.exp2(parts[0] if nblk == 1 else jnp.concatenate(parts, axis=1))
            if mask is not None:
                w = jnp.where(mask, w, 0.0)
            ws.append(w.astype(BF16))
        for w, sl in zip(ws, heads):
            acc[:, sl] += jnp.dot(w, v_ref[keys, sl], preferred_element_type=F32)

    @pl.when(qb % 2 == 0)
    def _():
        sweep(qb, 1, col < row)

    @pl.when(qb % 2 == 1)
    def _():
        row2 = lax.broadcasted_iota(jnp.int32, (blk, 2 * blk), 0)
        col2 = lax.broadcasted_iota(jnp.int32, (blk, 2 * blk), 1)
        sweep(qb - 1, 2, col2 < row2 + blk)

    @pl.when((qb // 2) % 2 == 1)
    def _():
        sweep((qb // 4) * 4, 2, None)

    def body(i, c):
        sweep((qb // 4 - 1 - i) * 4, 4, None)
        return c

    lax.fori_loop(0, qb // 4, body, 0)
    o_ref[...] = acc[...].astype(o_ref.dtype)


def _stick_breaking(h, batch, seq, col0):
    n = h.shape[0]
    blk = SB_BLOCK
    w = SB_HEADS * HEAD_DIM
    nq = seq // blk
    idx = jnp.arange(blk)
    tri = (idx[:, None] > idx[None, :]).astype(BF16)
    half = jnp.concatenate([tri, jnp.ones((blk, blk), BF16)], axis=1)
    u = jnp.concatenate([half, half], axis=0)
    return pl.pallas_call(
        _sb_kernel,
        out_shape=jax.ShapeDtypeStruct((n, w), BF16),
        grid=(batch, nq),
        in_specs=[pl.BlockSpec((blk, w), lambda b, t: (b * nq + t, col0)),
                  pl.BlockSpec((seq, w), lambda b, t: (b, col0 + 1)),
                  pl.BlockSpec((seq, w), lambda b, t: (b, col0 + 2)),
                  pl.BlockSpec((2 * blk, 2 * blk), lambda b, t: (0, 0))],
        out_specs=pl.BlockSpec((blk, w), lambda b, t: (b * nq + t, 0)),
        scratch_shapes=[pltpu.VMEM((blk, w), F32), pltpu.VMEM((blk, w), F32)],
        compiler_params=_params(("arbitrary", "arbitrary")),
        name="stick_breaking",
    )(h, h, h, u)


def _ffn_kernel(be_ref, ns_ref, x_ref, w1_ref, w3_ref, w2_ref, y_hbm, acc, sem, *, all_rows):
    i = pl.program_id(0)
    j = pl.program_id(1)
    tm = x_ref.shape[0]
    nsub = ns_ref[i]

    @pl.when(jnp.logical_and(i == 0, j == 0))
    def _():
        acc[...] = jnp.zeros_like(acc)

    def chunk(start, size):
        rows = pl.ds(start, size)
        x = x_ref[rows, :]
        a = jnp.dot(x, w1_ref[...].astype(BF16), preferred_element_type=F32)
        b = jnp.dot(x, w3_ref[...].astype(BF16), preferred_element_type=F32)
        hmid = (a * jax.nn.sigmoid(a) * b).astype(BF16)
        y = jnp.dot(hmid, w2_ref[...].astype(BF16), preferred_element_type=F32)
        acc[rows, :] = jnp.where(j == 0, y, acc[rows, :] + y)

    last = j == pl.num_programs(1) - 1
    half = tm // 2

    def copy_out(h):
        return pltpu.make_async_copy(acc.at[pl.ds(h * half, half), :],
                                     y_hbm.at[pl.ds(pl.multiple_of(i * tm + h * half, half), half), :], sem.at[h])

    if all_rows:
        step = min(FFN_CHUNK, half)
        for s in range(tm // step):
            chunk(s * step, step)
            if (s + 1) * step == half:
                @pl.when(last)
                def _():
                    copy_out(0).start()
    else:
        assert half == FFN_CHUNK and FFN_CHUNK == 8 * FFN_SUB
        nfull = nsub // 8

        def full_body(s, carry):
            chunk(pl.multiple_of(s * FFN_CHUNK, FFN_CHUNK), FFN_CHUNK)

            @pl.when(jnp.logical_and(last, s == 0))
            def _():
                copy_out(0).start()

            return carry

        lax.fori_loop(0, nfull, full_body, 0)
        rem = nsub % 8
        off = nfull * FFN_CHUNK
        for subs in (4, 2, 1):
            size = subs * FFN_SUB

            @pl.when((rem & subs) != 0)
            def _(off=off, size=size):
                chunk(pl.multiple_of(off, size), size)

            off = off + (rem & subs) * FFN_SUB

        @pl.when(jnp.logical_and(last, nfull == 0))
        def _():
            copy_out(0).start()

    @pl.when(last)
    def _():
        copy_out(1).start()
        copy_out(0).wait()
        copy_out(1).wait()


def _ffn(x, w1, w3, w2, block_e, block_nsub, name, all_rows=False):
    p, d = x.shape
    f = w1.shape[2]
    tm = min(FFN_ROWS, p)
    tf = min(FFN_TF, f)
    assert p % tm == 0 and f % tf == 0 and tm % FFN_SUB == 0
    nj = f // tf

    def jj(i, j, ns):
        return jnp.where(ns[i] > 0, j, nj - 1)

    return pl.pallas_call(
        functools.partial(_ffn_kernel, all_rows=all_rows),
        out_shape=jax.ShapeDtypeStruct((p, d), F32),
        grid_spec=pltpu.PrefetchScalarGridSpec(
            num_scalar_prefetch=2,
            grid=(p // tm, nj),
            in_specs=[pl.BlockSpec((tm, d), lambda i, j, be, ns: (i, 0)),
                      pl.BlockSpec((None, d, tf), lambda i, j, be, ns: (be[i], 0, jj(i, j, ns))),
                      pl.BlockSpec((None, d, tf), lambda i, j, be, ns: (be[i], 0, jj(i, j, ns))),
                      pl.BlockSpec((None, tf, d), lambda i, j, be, ns: (be[i], jj(i, j, ns), 0))],
            out_specs=pl.BlockSpec(memory_space=pl.ANY),
            scratch_shapes=[pltpu.VMEM((tm, d), F32), pltpu.SemaphoreType.DMA((2,))]),
        compiler_params=_params(("arbitrary", "arbitrary")),
        name=name,
    )(block_e, block_nsub, x, w1, w3, w2)


def _route_kernel(x_ref, w_ref, tri_ref, meta_ref, gate_ref, cnt_ref, run, *, n_experts):
    @pl.when(pl.program_id(0) == 0)
    def _():
        run[...] = jnp.zeros_like(run)

    x = x_ref[...]
    w = w_ref[...]
    xh = x.astype(BF16)
    wh = w.astype(BF16)
    xl = (x - xh.astype(F32)).astype(BF16)
    wl = (w - wh.astype(F32)).astype(BF16)
    logits = (jnp.dot(xh, wh, preferred_element_type=F32) + jnp.dot(xl, wh, preferred_element_type=F32)
              + jnp.dot(xh, wl, preferred_element_type=F32))
    lane_i = lax.broadcasted_iota(jnp.int32, logits.shape, 1)
    lane = lane_i.astype(F32)
    neg = jnp.float32(-jnp.inf)
    logits = jnp.where(lane_i < n_experts, logits, neg)
    m1 = jnp.max(logits, axis=-1, keepdims=True)
    i1 = jnp.min(jnp.where(logits == m1, lane, float(LANES)), axis=-1, keepdims=True)
    rest = jnp.where(lane == i1, neg, logits)
    m2 = jnp.max(rest, axis=-1, keepdims=True)
    i2 = jnp.min(jnp.where(rest == m2, lane, float(LANES)), axis=-1, keepdims=True)
    e2 = jnp.exp(m2 - m1)
    den = 1.0 + e2
    pick1 = jnp.where(lane == i1, 1.0, 0.0)
    pick2 = jnp.where(lane == i2, 1.0, 0.0)
    picks = pick1 + pick2
    before = run[...] + jnp.dot(tri_ref[...], picks.astype(BF16), preferred_element_type=F32)
    r1 = jnp.sum(before * pick1, axis=-1, keepdims=True)
    r2 = jnp.sum(before * pick2, axis=-1, keepdims=True)
    run[...] += jnp.sum(picks, axis=0, keepdims=True)
    cnt_ref[...] = run[...].astype(jnp.int32)
    meta = jnp.where(lane_i == 0, i1, jnp.where(lane_i == 1, i2, jnp.where(
        lane_i == 2, r1, jnp.where(lane_i == 3, r2, 0.0))))
    meta_ref[...] = meta.T[:META_ROWS].astype(jnp.int32)
    gate_ref[...] = jnp.where(lane_i == 0, 1.0 / den, jnp.where(lane_i == 1, e2 / den, 0.0))


def _route(x, w_router):
    n, d = x.shape
    e = w_router.shape[1]
    tm = min(ROUTE_ROWS, n)
    wpad = jnp.zeros((d, LANES), F32).at[:, :e].set(w_router)
    t = jnp.arange(tm)
    tri = (t[:, None] > t[None, :]).astype(BF16)
    meta, gate, cnt = pl.pallas_call(
        functools.partial(_route_kernel, n_experts=e),
        out_shape=(jax.ShapeDtypeStruct((META_ROWS, n), jnp.int32), jax.ShapeDtypeStruct((n, LANES), F32),
                   jax.ShapeDtypeStruct((1, LANES), jnp.int32)),
        grid=(n // tm,),
        in_specs=[pl.BlockSpec((tm, d), lambda i: (i, 0)), pl.BlockSpec((d, LANES), lambda i: (0, 0)),
                  pl.BlockSpec((tm, tm), lambda i: (0, 0))],
        out_specs=(pl.BlockSpec((META_ROWS, tm), lambda i: (0, i)), pl.BlockSpec((tm, LANES), lambda i: (i, 0)),
                   pl.BlockSpec((1, LANES), lambda i: (0, 0))),
        scratch_shapes=[pltpu.VMEM((1, LANES), F32)],
        compiler_params=_params(("arbitrary",)),
        name="route_top2",
    )(x, wpad, tri)
    return meta, gate, cnt[0, :e]


def _gather_kernel(tok_ref, ns_ref, x_hbm, o_ref, buf, sem):
    s = pl.program_id(0)
    per = FFN_ROWS // GATHER_ROWS

    def nonempty(step):
        return (step % per) * GATHER_ROWS < ns_ref[step // per] * FFN_SUB

    def issue(step):
        slot = step % 2
        base = step * GATHER_ROWS

        def row(r, carry):
            t = tok_ref[base + r]
            pltpu.make_async_copy(x_hbm.at[pl.ds(t, 1), :], buf.at[slot, pl.ds(r, 1), :], sem.at[slot]).start()
            return carry

        lax.fori_loop(0, GATHER_ROWS, row, 0, unroll=8)

    @pl.when(jnp.logical_and(s == 0, nonempty(0)))
    def _():
        issue(0)

    nxt = jnp.minimum(s + 1, pl.num_programs(0) - 1)

    @pl.when(jnp.logical_and(s + 1 < pl.num_programs(0), nonempty(nxt)))
    def _():
        issue(nxt)

    @pl.when(nonempty(s))
    def _():
        slot = s % 2
        pltpu.make_async_copy(x_hbm.at[pl.ds(0, GATHER_ROWS), :], buf.at[slot], sem.at[slot]).wait()
        o_ref[...] = buf[slot].astype(o_ref.dtype)

    @pl.when(jnp.logical_not(nonempty(s)))
    def _():
        o_ref[...] = jnp.zeros_like(o_ref)


def _gather_rows(x, slot_tok, block_nsub, p):
    n, d = x.shape
    return pl.pallas_call(
        _gather_kernel,
        out_shape=jax.ShapeDtypeStruct((p, d), BF16),
        grid_spec=pltpu.PrefetchScalarGridSpec(
            num_scalar_prefetch=2,
            grid=(p // GATHER_ROWS,),
            in_specs=[pl.BlockSpec(memory_space=pl.ANY)],
            out_specs=pl.BlockSpec((GATHER_ROWS, d), lambda s, tok, ns: (s, 0)),
            scratch_shapes=[pltpu.VMEM((2, GATHER_ROWS, d), F32), pltpu.SemaphoreType.DMA((2,))]),
        compiler_params=_params(("arbitrary",)),
        name="moe_gather",
    )(slot_tok, block_nsub, x)


def _combine_kernel(p0_ref, p1_ref, y_hbm, x_ref, ple_ref, gate_ref, lg_ref, lb_ref,
                    o_ref, ob_ref, buf, sem):
    i = pl.program_id(0)
    tm = x_ref.shape[0]

    def issue(tile):
        slot = tile % 2
        base = tile * tm

        def row(r, carry):
            pltpu.make_async_copy(y_hbm.at[pl.ds(p0_ref[base + r], 1), :], buf.at[slot, 0, pl.ds(r, 1), :],
                                  sem.at[slot]).start()
            pltpu.make_async_copy(y_hbm.at[pl.ds(p1_ref[base + r], 1), :], buf.at[slot, 1, pl.ds(r, 1), :],
                                  sem.at[slot]).start()
            return carry

        lax.fori_loop(0, tm, row, 0, unroll=4)

    @pl.when(i == 0)
    def _():
        issue(0)

    @pl.when(i + 1 < pl.num_programs(0))
    def _():
        issue(i + 1)

    slot = i % 2
    pltpu.make_async_copy(y_hbm.at[pl.ds(0, tm), :], buf.at[slot, 0], sem.at[slot]).wait()
    pltpu.make_async_copy(y_hbm.at[pl.ds(0, tm), :], buf.at[slot, 1], sem.at[slot]).wait()
    gate = gate_ref[...]
    f = gate[:, 0:1] * buf[slot, 0] + gate[:, 1:2] * buf[slot, 1]
    out = _ln_body(DEEPNORM_ALPHA * x_ref[...] + f + ple_ref[...].astype(F32), lg_ref[...], lb_ref[...])
    o_ref[...] = out
    ob_ref[...] = out.astype(BF16)


def _combine_ln(y, pos0, pos1, gate, x, ple, ln_g, ln_b):
    n, d = x.shape
    tm = min(2 * LN_ROWS, n)
    tile = lambda: pl.BlockSpec((tm, d), lambda i, a, b: (i, 0))
    row = lambda: pl.BlockSpec((1, d), lambda i, a, b: (0, 0))
    return pl.pallas_call(
        _combine_kernel,
        out_shape=(jax.ShapeDtypeStruct((n, d), F32), jax.ShapeDtypeStruct((n, d), BF16)),
        grid_spec=pltpu.PrefetchScalarGridSpec(
            num_scalar_prefetch=2,
            grid=(n // tm,),
            in_specs=[pl.BlockSpec(memory_space=pl.ANY), tile(), tile(),
                      pl.BlockSpec((tm, LANES), lambda i, a, b: (i, 0)), row(), row()],
            out_specs=(tile(), tile()),
            scratch_shapes=[pltpu.VMEM((2, 2, tm, d), F32), pltpu.SemaphoreType.DMA((2,))]),
        compiler_params=_params(("arbitrary",)),
        name="moe_combine_ln",
    )(pos0, pos1, y, x, ple, gate, ln_g.reshape(1, d), ln_b.reshape(1, d))


def _moe_plan(meta, counts, n_experts):
    n = meta.shape[1]
    nk = n * TOP_K
    nblk = (nk - n_experts) // FFN_ROWS + n_experts
    sub_per_blk = FFN_ROWS // FFN_SUB
    expert = meta[:TOP_K]
    rank = meta[TOP_K:2 * TOP_K]
    nsub_e = (counts + FFN_SUB - 1) // FFN_SUB
    nblk_e = (nsub_e + sub_per_blk - 1) // sub_per_blk
    rows_e = jnp.maximum((nsub_e + nblk_e - 1) // jnp.maximum(nblk_e, 1), 1) * FFN_SUB
    blk_end = jnp.cumsum(nblk_e)
    blk_off = blk_end - nblk_e
    onehot = expert[None] == jnp.arange(n_experts, dtype=jnp.int32)[:, None, None]
    rpb = jnp.sum(jnp.where(onehot, rows_e[:, None, None], 0), axis=0)
    first = jnp.sum(jnp.where(onehot, blk_off[:, None, None], 0), axis=0)
    pos = (first + rank // rpb) * FFN_ROWS + rank % rpb
    blk = jnp.arange(nblk, dtype=jnp.int32)
    used = blk < blk_end[-1]
    block_e = jnp.clip(jnp.searchsorted(blk_end, blk, side="right"), 0, n_experts - 1).astype(jnp.int32)
    block_e = jnp.where(used, block_e, block_e[jnp.maximum(blk_end[-1] - 1, 0)])
    rows = jnp.clip(counts[block_e] - (blk - blk_off[block_e]) * rows_e[block_e], 0, rows_e[block_e])
    rows = jnp.where(used, rows, 0)
    block_nsub = ((rows + FFN_SUB - 1) // FFN_SUB).astype(jnp.int32)
    tok = jnp.tile(jnp.arange(n, dtype=jnp.int32), TOP_K)
    slot_tok = jnp.zeros((nblk * FFN_ROWS,), jnp.int32).at[pos.reshape(nk)].set(
        tok, unique_indices=True)
    return pos.astype(jnp.int32), slot_tok, block_e, block_nsub, nblk * FFN_ROWS


def _token_mixer(xb, layer, w_in, w_br_ret, w_br_sb, w_gate, b_gate, batch, seq):
    d = xb.shape[1]
    in_width = w_in.shape[2]
    rw = RET_HEADS * HEAD_DIM
    sw = SB_HEADS * HEAD_DIM
    col_scale = jnp.ones((in_width,), F32)
    col_scale = col_scale.at[rw:2 * rw].set(HEAD_DIM ** -0.5)
    col_scale = col_scale.at[4 * rw:4 * rw + sw].set(HEAD_DIM ** -0.5 * math.log2(math.e))
    h = _mm([xb[None]], [(0, w_in, 0)], [(col_scale.reshape(1, 1, in_width), 0)], layer, _ep_colscale,
            in_width, BF16, 2048, 1024, "in_proj")
    o_ret = _retention(h, batch, seq)
    o_sb = _stick_breaking(h, batch, seq, (4 * RET_HEADS) // SB_HEADS)
    tn = min(512, d)
    bg = b_gate.reshape(b_gate.shape[0], 1, 2 * d)
    return _mm([o_ret[None], o_sb[None], xb[None]],
               [(0, w_br_ret, 0), (1, w_br_sb, 0), (2, w_gate, 0), (2, w_gate, d // tn)],
               [(bg, 0), (bg, d // tn)], layer, _ep_merge, d, BF16, 1024, tn, "branch_merge",
               single_buffer_b=True)


def kernel(x, p, w_in, w_br_ret, w_br_sb, w_gate, b_gate, w_o, ln1_g, ln1_b,
           ffn_w1, ffn_w3, ffn_w2, moe_router, moe_w1, moe_w3, moe_w2,
           ple_w, ple_gate_w, ln2_g, ln2_b):
    batch, seq, d = x.shape
    n = batch * seq
    depth = w_in.shape[0]
    n_experts = moe_router.shape[2]
    xf = x.reshape(n, d)
    xb = xf.astype(BF16)
    pf = p.reshape(depth, n, p.shape[3])
    ew1 = moe_w1.reshape((-1,) + moe_w1.shape[2:])
    ew3 = moe_w3.reshape((-1,) + moe_w3.shape[2:])
    ew2 = moe_w2.reshape((-1,) + moe_w2.shape[2:])
    for i in range(depth):
        merged = _token_mixer(xb, i, w_in, w_br_ret, w_br_sb, w_gate, b_gate, batch, seq)
        xf, xb = _proj_ln(merged, w_o, i, xf, ln1_g[i], ln1_b[i], "out_proj_ln")
        ple = _mm([xb[None], pf], [(0, ple_gate_w, 0), (1, ple_w, 0)], [], i, _ep_ple, d, BF16, 1024, 1024, "ple")
        if i % 2 == 0:
            nblk = n // min(FFN_ROWS, n)
            f = _ffn(xb, ffn_w1, ffn_w3, ffn_w2, jnp.full((nblk,), i // 2, jnp.int32),
                     jnp.full((nblk,), FFN_ROWS // FFN_SUB, jnp.int32), "dense_swiglu", all_rows=True)
            xf, xb = _ln(xf, [f, ple], ln2_g[i], ln2_b[i], "ln_ffn")
        else:
            meta, gate, counts = _route(xf, moe_router[i // 2])
            pos, slot_tok, block_e, block_nsub, slots = _moe_plan(meta, counts, n_experts)
            xs = _gather_rows(xf, slot_tok, block_nsub, slots)
            y = _ffn(xs, ew1, ew3, ew2, block_e + (i // 2) * n_experts, block_nsub, "expert_swiglu")
            xf, xb = _combine_ln(y, pos[0], pos[1], gate, xf, ple, ln2_g[i], ln2_b[i])
    return xf.reshape(batch, seq, d)
```

```python
import functools
import math

import jax
import jax.numpy as jnp
from jax import lax
from jax.experimental import pallas as pl
from jax.experimental.pallas import tpu as pltpu

F32 = jnp.float32
BF16 = jnp.bfloat16

RET_HEADS = 8
SB_HEADS = 8
HEAD_DIM = 128
RET_CHUNK = 128
ROPE_BASE = 10000.0
TOP_K = 2
DEPTH = 2
DEEPNORM_ALPHA = (2 * DEPTH) ** 0.25
LN_EPS = 1e-5

V7X_VMEM_BYTES = 64 * 1024 * 1024
VMEM_LIMIT = V7X_VMEM_BYTES - 8 * 1024 * 1024
LANES = 128

RET_STEP_CHUNKS = 2
MM_EPILOGUE_COLS = 256
SB_BLOCK = 128
PROJ_LN_ROWS = 512
FFN_ROWS = 2048
FFN_SUB = 128
FFN_CHUNK = 1024
FFN_TF = 256
LN_ROWS = 256
ROUTE_ROWS = 512
META_ROWS = 8
GATHER_ROWS = 256


def _params(sem):
    return pltpu.CompilerParams(dimension_semantics=sem, vmem_limit_bytes=VMEM_LIMIT)


def _mm_kernel(*refs, a_of, n_a, n_extra, epilogue):
    n_prod = len(a_of)
    a_refs = refs[:n_a]
    b_refs = refs[n_a:n_a + n_prod]
    e_refs = refs[n_a + n_prod:n_a + n_prod + n_extra]
    o_ref = refs[n_a + n_prod + n_extra]
    b_scr = refs[n_a + n_prod + n_extra + 1:]

    @pl.when(pl.program_id(1) == 0)
    def _():
        for b_ref, s in zip(b_refs, b_scr):
            s[...] = b_ref[...].astype(BF16)

    a_vals = [a[...].astype(BF16) for a in a_refs]
    tn = o_ref.shape[1]
    cw = min(MM_EPILOGUE_COLS, tn)
    for c in range(tn // cw):
        cols = slice(c * cw, (c + 1) * cw)
        accs = [jnp.dot(a_vals[ai], s[:, cols], preferred_element_type=F32) for ai, s in zip(a_of, b_scr)]
        o_ref[:, cols] = epilogue(accs, [e[:, cols] for e in e_refs]).astype(o_ref.dtype)


def _mm(a_ops, products, extras, layer, epilogue, n_out, out_dtype, tm, tn, name, single_buffer_b=False):
    b_mode = dict(pipeline_mode=pl.Buffered(1)) if single_buffer_b else {}
    m = a_ops[0].shape[1]
    tm = min(tm, m)
    tn = min(tn, n_out)
    assert m % tm == 0 and n_out % tn == 0
    in_specs, args, scratch = [], [], []
    for a in a_ops:
        la = layer if a.shape[0] > 1 else 0
        in_specs.append(pl.BlockSpec((None, tm, a.shape[2]), lambda j, i, la=la: (la, i, 0)))
        args.append(a)
    for _, b, off in products:
        in_specs.append(pl.BlockSpec((None, b.shape[1], tn), lambda j, i, off=off: (layer, 0, j + off),
                                     **b_mode))
        args.append(b)
        scratch.append(pltpu.VMEM((b.shape[1], tn), BF16))
    for e, off in extras:
        le = layer if e.shape[0] > 1 else 0
        in_specs.append(pl.BlockSpec((None, 1, tn), lambda j, i, off=off, le=le: (le, 0, j + off)))
        args.append(e)
    kern = functools.partial(_mm_kernel, a_of=tuple(ai for ai, _, _ in products), n_a=len(a_ops),
                             n_extra=len(extras), epilogue=epilogue)
    return pl.pallas_call(
        kern,
        out_shape=jax.ShapeDtypeStruct((m, n_out), out_dtype),
        grid=(n_out // tn, m // tm),
        in_specs=in_specs,
        out_specs=pl.BlockSpec((tm, tn), lambda j, i: (i, j)),
        scratch_shapes=scratch,
        compiler_params=_params(("arbitrary", "arbitrary")),
        name=name,
    )(*args)


def _ep_colscale(accs, extras):
    return accs[0] * extras[0]


def _ep_merge(accs, extras):
    o_r, o_s, z_r, z_s = accs
    b_r, b_s = extras
    return jax.nn.sigmoid(z_r + b_r) * o_r + jax.nn.sigmoid(z_s + b_s) * o_s


def _ep_ple(accs, extras):
    return jax.nn.sigmoid(accs[0]) * accs[1]


def _ln_body(y, g, b):
    mu = jnp.mean(y, axis=-1, keepdims=True)
    yc = y - mu
    var = jnp.mean(yc * yc, axis=-1, keepdims=True)
    return yc * lax.rsqrt(var + LN_EPS) * g + b


def _ln_kernel(*refs, n_add):
    x_ref = refs[0]
    add_refs = refs[1:1 + n_add]
    g_ref, b_ref, o_ref, ob_ref = refs[1 + n_add:]
    y = DEEPNORM_ALPHA * x_ref[...]
    for a in add_refs:
        y = y + a[...].astype(F32)
    out = _ln_body(y, g_ref[...], b_ref[...])
    o_ref[...] = out
    ob_ref[...] = out.astype(BF16)


def _ln(x, adds, g, b, name):
    n, d = x.shape
    tm = min(2 * LN_ROWS, n)
    tile = pl.BlockSpec((tm, d), lambda i: (i, 0))
    row = pl.BlockSpec((1, d), lambda i: (0, 0))
    return pl.pallas_call(
        functools.partial(_ln_kernel, n_add=len(adds)),
        out_shape=(jax.ShapeDtypeStruct((n, d), F32), jax.ShapeDtypeStruct((n, d), BF16)),
        grid=(n // tm,),
        in_specs=[tile] * (1 + len(adds)) + [row, row],
        out_specs=(tile, tile),
        compiler_params=_params(("arbitrary",)),
        name=name,
    )(x, *adds, g.reshape(1, d), b.reshape(1, d))


def _proj_ln_kernel(a_ref, w_ref, x_ref, g_ref, b_ref, o_ref, ob_ref, w_scr):
    @pl.when(pl.program_id(0) == 0)
    def _():
        w_scr[...] = w_ref[...].astype(BF16)

    mix = jnp.dot(a_ref[...], w_scr[...], preferred_element_type=F32)
    out = _ln_body(DEEPNORM_ALPHA * x_ref[...] + mix, g_ref[...], b_ref[...])
    o_ref[...] = out
    ob_ref[...] = out.astype(BF16)


def _proj_ln(a, w, layer, x, g, b, name):
    n, d = x.shape
    k = a.shape[1]
    tm = min(PROJ_LN_ROWS, n)
    tile = pl.BlockSpec((tm, d), lambda i: (i, 0))
    row = pl.BlockSpec((1, d), lambda i: (0, 0))
    return pl.pallas_call(
        _proj_ln_kernel,
        out_shape=(jax.ShapeDtypeStruct((n, d), F32), jax.ShapeDtypeStruct((n, d), BF16)),
        grid=(n // tm,),
        in_specs=[pl.BlockSpec((tm, k), lambda i: (i, 0)),
                  pl.BlockSpec((None, k, d), lambda i: (layer, 0, 0), pipeline_mode=pl.Buffered(1)),
                  tile, row, row],
        out_specs=(tile, tile),
        scratch_shapes=[pltpu.VMEM((k, d), BF16)],
        compiler_params=_params(("arbitrary",)),
        name=name,
    )(a, w, x, g.reshape(1, d), b.reshape(1, d))


def _dot_nt(a, b):
    return lax.dot_general(a, b, (((1,), (1,)), ((), ())), preferred_element_type=F32)


def _dot_tn(a, b):
    return lax.dot_general(a, b, (((0,), (0,)), ((), ())), preferred_element_type=F32)


def _ret_kernel(q_ref, k_ref, v_ref, g_ref, cos_ref, sin_ref, decay_ref, xi_ref, zeta_ref,
                o_ref, r_scr, *, chunk_decay):
    d = HEAD_DIM

    @pl.when(pl.program_id(1) == 0)
    def _():
        r_scr[...] = jnp.zeros_like(r_scr)

    c = RET_CHUNK
    pairs = [(slice(ci * c, (ci + 1) * c), h, slice(h * d, (h + 1) * d))
             for ci in range(RET_STEP_CHUNKS) for h in range(RET_HEADS)]
    qbs, kbs, kzs = [], [], []
    for rows, h, sl in pairs:
        cos = cos_ref[rows, :]
        sin = sin_ref[rows, :]

        def rot(t):
            return t * cos + pltpu.roll(t, d // 2, 1) * sin

        kr = rot(k_ref[rows, sl].astype(F32))
        qbs.append(rot(q_ref[rows, sl].astype(F32)).astype(BF16))
        kbs.append(kr.astype(BF16))
        kzs.append((kr * zeta_ref[h]).astype(BF16))
    inners = [_dot_nt(qb, kb) for qb, kb in zip(qbs, kbs)]
    kvs = [_dot_tn(kz, v_ref[rows, sl]) for kz, (rows, h, sl) in zip(kzs, pairs)]
    states = [None] * len(pairs)
    for h in range(RET_HEADS):
        r = r_scr[h]
        for ci in range(RET_STEP_CHUNKS):
            states[ci * RET_HEADS + h] = r
            r = r * chunk_decay[h] + kvs[ci * RET_HEADS + h]
        r_scr[h] = r
    crosses = [jnp.dot(qb, r.astype(BF16), preferred_element_type=F32) for qb, r in zip(qbs, states)]
    pbs = [(inner * decay_ref[h]).astype(BF16) for inner, (rows, h, sl) in zip(inners, pairs)]
    outs = [jnp.dot(pb, v_ref[rows, sl], preferred_element_type=F32) for pb, (rows, h, sl) in zip(pbs, pairs)]
    for out, cross, (rows, h, sl) in zip(outs, crosses, pairs):
        o = out + cross * xi_ref[h]
        mu = jnp.mean(o, axis=-1, keepdims=True)
        oc = o - mu
        var = jnp.mean(oc * oc, axis=-1, keepdims=True)
        g = g_ref[rows, sl].astype(F32)
        o_ref[rows, sl] = (oc * lax.rsqrt(var + LN_EPS) * (g * jax.nn.sigmoid(g))).astype(o_ref.dtype)


def _retention(h, batch, seq):
    n = h.shape[0]
    c = RET_CHUNK
    d = HEAD_DIM
    w = RET_HEADS * d
    half = d // 2
    pos = jnp.arange(seq, dtype=F32)
    inv = ROPE_BASE ** (-jnp.arange(half, dtype=F32) / half)
    ang = pos[:, None] * inv[None, :]
    cos = jnp.concatenate([jnp.cos(ang), jnp.cos(ang)], axis=1)
    sin = jnp.concatenate([-jnp.sin(ang), jnp.sin(ang)], axis=1)
    gamma = 1.0 - jnp.exp2(-5.0 - jnp.arange(RET_HEADS, dtype=F32))
    lg = jnp.log(gamma)
    idx = jnp.arange(c, dtype=F32)
    diff = idx[:, None] - idx[None, :]
    causal = diff >= 0
    decay = jnp.where(causal[None], jnp.exp(jnp.where(causal, diff, 0.0)[None] * lg[:, None, None]), 0.0)
    xi = jnp.broadcast_to(jnp.exp((idx + 1.0)[None, :] * lg[:, None])[:, :, None], (RET_HEADS, c, d))
    zeta = jnp.broadcast_to(jnp.exp((c - 1.0 - idx)[None, :] * lg[:, None])[:, :, None], (RET_HEADS, c, d))
    chunk_decay = tuple(math.exp(c * math.log(1.0 - 2.0 ** (-5.0 - hh))) for hh in range(RET_HEADS))

    rows = RET_STEP_CHUNKS * c
    ns = seq // rows
    assert seq % rows == 0

    def col(j):
        return pl.BlockSpec((rows, w), lambda b, t, j=j: (b * ns + t, j))

    tab = pl.BlockSpec((rows, d), lambda b, t: (t, 0))
    hconst = pl.BlockSpec((RET_HEADS, c, d), lambda b, t: (0, 0, 0))
    return pl.pallas_call(
        functools.partial(_ret_kernel, chunk_decay=chunk_decay),
        out_shape=jax.ShapeDtypeStruct((n, w), BF16),
        grid=(batch, ns),
        in_specs=[col(0), col(1), col(2), col(3), tab, tab,
                  pl.BlockSpec((RET_HEADS, c, c), lambda b, t: (0, 0, 0)), hconst, hconst],
        out_specs=pl.BlockSpec((rows, w), lambda b, t: (b * ns + t, 0)),
        scratch_shapes=[pltpu.VMEM((RET_HEADS, d, d), F32)],
        compiler_params=_params(("arbitrary", "arbitrary")),
        name="retention",
    )(h, h, h, h, cos, sin, decay, xi, zeta)


def _sb_kernel(q_ref, k_ref, v_ref, u_ref, o_ref, acc, carry):
    blk = SB_BLOCK
    d = HEAD_DIM
    qb = pl.program_id(1)
    acc[...] = jnp.zeros_like(acc)
    carry[...] = jnp.zeros_like(carry)
    row = lax.broadcasted_iota(jnp.int32, (blk, blk), 0)
    col = lax.broadcasted_iota(jnp.int32, (blk, blk), 1)
    heads = [slice(hh * d, (hh + 1) * d) for hh in range(SB_HEADS)]
    sign = jnp.uint32(0x80000000)

    def sweep(kb, nblk, mask):
        keys = pl.ds(pl.multiple_of(kb * blk, blk), nblk * blk)
        zs = [_dot_nt(q_ref[:, sl], k_ref[keys, sl]) for sl in heads]
        log_betas, hilos = [], []
        for z in zs:
            neg_abs = lax.bitcast_convert_type(lax.bitcast_convert_type(z, jnp.uint32) | sign, F32)
            log_beta = jnp.minimum(z, 0.0) - jnp.log2(1.0 + jnp.exp2(neg_abs))
            log_fail = log_beta - z
            if mask is not None:
                log_fail = jnp.where(mask, log_fail, 0.0)
            hi = log_fail.astype(BF16)
            lo = (log_fail - hi.astype(F32)).astype(BF16)
            log_betas.append(log_beta)
            for c in range(nblk):
                cols = slice(c * blk, (c + 1) * blk)
                hilos.append(jnp.concatenate([hi[:, cols], lo[:, cols]], axis=1))
        s_all = jnp.dot(jnp.concatenate(hilos, axis=0), u_ref[...], preferred_element_type=F32)
        ws = []
        for hh, sl in enumerate(heads):
            run = carry[:, sl]
            parts = [None] * nblk
            for c in reversed(range(nblk)):
                s = s_all[(hh * nblk + c) * blk:(hh * nblk + c + 1) * blk]
                parts[c] = log_betas[hh][:, c * blk:(c + 1) * blk] + run + s[:, :blk]
                run = run + s[:, blk:]
            carry[:, sl] = run
            w = jnp.exp2(parts[0] if nblk == 1 else jnp.concatenate(parts, axis=1))
            if mask is not None:
                w = jnp.where(mask, w, 0.0)
            ws.append(w.astype(BF16))
        for w, sl in zip(ws, heads):
            acc[:, sl] += jnp.dot(w, v_ref[keys, sl], preferred_element_type=F32)

    @pl.when(qb % 2 == 0)
    def _():
        sweep(qb, 1, col < row)

    @pl.when(qb % 2 == 1)
    def _():
        row2 = lax.broadcasted_iota(jnp.int32, (blk, 2 * blk), 0)
        col2 = lax.broadcasted_iota(jnp.int32, (blk, 2 * blk), 1)
        sweep(qb - 1, 2, col2 < row2 + blk)

    @pl.when((qb // 2) % 2 == 1)
    def _():
        sweep((qb // 4) * 4, 2, None)

    def body(i, c):
        sweep((qb // 4 - 1 - i) * 4, 4, None)
        return c

    lax.fori_loop(0, qb // 4, body, 0)
    o_ref[...] = acc[...].astype(o_ref.dtype)


def _stick_breaking(h, batch, seq, col0):
    n = h.shape[0]
    blk = SB_BLOCK
    w = SB_HEADS * HEAD_DIM
    nq = seq // blk
    idx = jnp.arange(blk)
    tri = (idx[:, None] > idx[None, :]).astype(BF16)
    half = jnp.concatenate([tri, jnp.ones((blk, blk), BF16)], axis=1)
    u = jnp.concatenate([half, half], axis=0)
    return pl.pallas_call(
        _sb_kernel,
        out_shape=jax.ShapeDtypeStruct((n, w), BF16),
        grid=(batch, nq),
        in_specs=[pl.BlockSpec((blk, w), lambda b, t: (b * nq + t, col0)),
                  pl.BlockSpec((seq, w), lambda b, t: (b, col0 + 1)),
                  pl.BlockSpec((seq, w), lambda b, t: (b, col0 + 2)),
                  pl.BlockSpec((2 * blk, 2 * blk), lambda b, t: (0, 0))],
        out_specs=pl.BlockSpec((blk, w), lambda b, t: (b * nq + t, 0)),
        scratch_shapes=[pltpu.VMEM((blk, w), F32), pltpu.VMEM((blk, w), F32)],
        compiler_params=_params(("arbitrary", "arbitrary")),
        name="stick_breaking",
    )(h, h, h, u)


def _ffn_kernel(be_ref, ns_ref, x_ref, w1_ref, w3_ref, w2_ref, y_hbm, acc, sem, *, all_rows):
    i = pl.program_id(0)
    j = pl.program_id(1)
    tm = x_ref.shape[0]
    nsub = ns_ref[i]

    @pl.when(jnp.logical_and(i == 0, j == 0))
    def _():
        acc[...] = jnp.zeros_like(acc)

    def chunk(start, size):
        rows = pl.ds(start, size)
        x = x_ref[rows, :]
        a = jnp.dot(x, w1_ref[...].astype(BF16), preferred_element_type=F32)
        b = jnp.dot(x, w3_ref[...].astype(BF16), preferred_element_type=F32)
        hmid = (a * jax.nn.sigmoid(a) * b).astype(BF16)
        y = jnp.dot(hmid, w2_ref[...].astype(BF16), preferred_element_type=F32)
        acc[rows, :] = jnp.where(j == 0, y, acc[rows, :] + y)

    last = j == pl.num_programs(1) - 1
    half = tm // 2

    def copy_out(h):
        return pltpu.make_async_copy(acc.at[pl.ds(h * half, half), :],
                                     y_hbm.at[pl.ds(pl.multiple_of(i * tm + h * half, half), half), :], sem.at[h])

    if all_rows:
        step = min(FFN_CHUNK // 2, half)
        for s in range(tm // step):
            chunk(s * step, step)
            if (s + 1) * step == half:
                @pl.when(last)
                def _():
                    copy_out(0).start()
    else:
        assert half == FFN_CHUNK and FFN_CHUNK == 8 * FFN_SUB
        nfull = nsub // 8

        def full_body(s, carry):
            chunk(pl.multiple_of(s * FFN_CHUNK, FFN_CHUNK), FFN_CHUNK)

            @pl.when(jnp.logical_and(last, s == 0))
            def _():
                copy_out(0).start()

            return carry

        lax.fori_loop(0, nfull, full_body, 0)
        rem = nsub % 8
        off = nfull * FFN_CHUNK
        for subs in (4, 2, 1):
            size = subs * FFN_SUB

            @pl.when((rem & subs) != 0)
            def _(off=off, size=size):
                chunk(pl.multiple_of(off, size), size)

            off = off + (rem & subs) * FFN_SUB

        @pl.when(jnp.logical_and(last, nfull == 0))
        def _():
            copy_out(0).start()

    @pl.when(last)
    def _():
        copy_out(1).start()
        copy_out(0).wait()
        copy_out(1).wait()


def _ffn(x, w1, w3, w2, block_e, block_nsub, name, all_rows=False):
    p, d = x.shape
    f = w1.shape[2]
    tm = min(FFN_ROWS, p)
    tf = min(FFN_TF, f)
    assert p % tm == 0 and f % tf == 0 and tm % FFN_SUB == 0
    nj = f // tf

    def jj(i, j, ns):
        return jnp.where(ns[i] > 0, j, nj - 1)

    return pl.pallas_call(
        functools.partial(_ffn_kernel, all_rows=all_rows),
        out_shape=jax.ShapeDtypeStruct((p, d), F32),
        grid_spec=pltpu.PrefetchScalarGridSpec(
            num_scalar_prefetch=2,
            grid=(p // tm, nj),
            in_specs=[pl.BlockSpec((tm, d), lambda i, j, be, ns: (i, 0)),
                      pl.BlockSpec((None, d, tf), lambda i, j, be, ns: (be[i], 0, jj(i, j, ns))),
                      pl.BlockSpec((None, d, tf), lambda i, j, be, ns: (be[i], 0, jj(i, j, ns))),
                      pl.BlockSpec((None, tf, d), lambda i, j, be, ns: (be[i], jj(i, j, ns), 0))],
            out_specs=pl.BlockSpec(memory_space=pl.ANY),
            scratch_shapes=[pltpu.VMEM((tm, d), F32), pltpu.SemaphoreType.DMA((2,))]),
        compiler_params=_params(("arbitrary", "arbitrary")),
        name=name,
    )(block_e, block_nsub, x, w1, w3, w2)


def _route_kernel(x_ref, w_ref, tri_ref, meta_ref, gate_ref, cnt_ref, run, *, n_experts):
    @pl.when(pl.program_id(0) == 0)
    def _():
        run[...] = jnp.zeros_like(run)

    x = x_ref[...]
    w = w_ref[...]
    xh = x.astype(BF16)
    wh = w.astype(BF16)
    xl = (x - xh.astype(F32)).astype(BF16)
    wl = (w - wh.astype(F32)).astype(BF16)
    logits = (jnp.dot(xh, wh, preferred_element_type=F32) + jnp.dot(xl, wh, preferred_element_type=F32)
              + jnp.dot(xh, wl, preferred_element_type=F32))
    lane_i = lax.broadcasted_iota(jnp.int32, logits.shape, 1)
    lane = lane_i.astype(F32)
    neg = jnp.float32(-jnp.inf)
    logits = jnp.where(lane_i < n_experts, logits, neg)
    m1 = jnp.max(logits, axis=-1, keepdims=True)
    i1 = jnp.min(jnp.where(logits == m1, lane, float(LANES)), axis=-1, keepdims=True)
    rest = jnp.where(lane == i1, neg, logits)
    m2 = jnp.max(rest, axis=-1, keepdims=True)
    i2 = jnp.min(jnp.where(rest == m2, lane, float(LANES)), axis=-1, keepdims=True)
    e2 = jnp.exp(m2 - m1)
    den = 1.0 + e2
    pick1 = jnp.where(lane == i1, 1.0, 0.0)
    pick2 = jnp.where(lane == i2, 1.0, 0.0)
    picks = pick1 + pick2
    before = run[...] + jnp.dot(tri_ref[...], picks.astype(BF16), preferred_element_type=F32)
    r1 = jnp.sum(before * pick1, axis=-1, keepdims=True)
    r2 = jnp.sum(before * pick2, axis=-1, keepdims=True)
    run[...] += jnp.sum(picks, axis=0, keepdims=True)
    cnt_ref[...] = run[...].astype(jnp.int32)
    meta = jnp.where(lane_i == 0, i1, jnp.where(lane_i == 1, i2, jnp.where(
        lane_i == 2, r1, jnp.where(lane_i == 3, r2, 0.0))))
    meta_ref[...] = meta.T[:META_ROWS].astype(jnp.int32)
    gate_ref[...] = jnp.where(lane_i == 0, 1.0 / den, jnp.where(lane_i == 1, e2 / den, 0.0))


def _route(x, w_router):
    n, d = x.shape
    e = w_router.shape[1]
    tm = min(ROUTE_ROWS, n)
    wpad = jnp.zeros((d, LANES), F32).at[:, :e].set(w_router)
    t = jnp.arange(tm)
    tri = (t[:, None] > t[None, :]).astype(BF16)
    meta, gate, cnt = pl.pallas_call(
        functools.partial(_route_kernel, n_experts=e),
        out_shape=(jax.ShapeDtypeStruct((META_ROWS, n), jnp.int32), jax.ShapeDtypeStruct((n, LANES), F32),
                   jax.ShapeDtypeStruct((1, LANES), jnp.int32)),
        grid=(n // tm,),
        in_specs=[pl.BlockSpec((tm, d), lambda i: (i, 0)), pl.BlockSpec((d, LANES), lambda i: (0, 0)),
                  pl.BlockSpec((tm, tm), lambda i: (0, 0))],
        out_specs=(pl.BlockSpec((META_ROWS, tm), lambda i: (0, i)), pl.BlockSpec((tm, LANES), lambda i: (i, 0)),
                   pl.BlockSpec((1, LANES), lambda i: (0, 0))),
        scratch_shapes=[pltpu.VMEM((1, LANES), F32)],
        compiler_params=_params(("arbitrary",)),
        name="route_top2",
    )(x, wpad, tri)
    return meta, gate, cnt[0, :e]


def _gather_kernel(tok_ref, ns_ref, x_hbm, o_ref, buf, sem):
    s = pl.program_id(0)
    per = FFN_ROWS // GATHER_ROWS

    def nonempty(step):
        return (step % per) * GATHER_ROWS < ns_ref[step // per] * FFN_SUB

    def issue(step):
        slot = step % 2
        base = step * GATHER_ROWS

        def row_pair(rp, carry):
            for prio in range(2):
                r = 2 * rp + prio
                t = tok_ref[base + r]
                pltpu.make_async_copy(x_hbm.at[pl.ds(t, 1), :], buf.at[slot, pl.ds(r, 1), :],
                                      sem.at[slot]).start(priority=prio)
            return carry

        lax.fori_loop(0, GATHER_ROWS // 2, row_pair, 0, unroll=4)

    @pl.when(jnp.logical_and(s == 0, nonempty(0)))
    def _():
        issue(0)

    nxt = jnp.minimum(s + 1, pl.num_programs(0) - 1)

    @pl.when(jnp.logical_and(s + 1 < pl.num_programs(0), nonempty(nxt)))
    def _():
        issue(nxt)

    @pl.when(nonempty(s))
    def _():
        slot = s % 2
        pltpu.make_async_copy(x_hbm.at[pl.ds(0, GATHER_ROWS), :], buf.at[slot], sem.at[slot]).wait()
        o_ref[...] = buf[slot].astype(o_ref.dtype)

    @pl.when(jnp.logical_not(nonempty(s)))
    def _():
        o_ref[...] = jnp.zeros_like(o_ref)


def _gather_rows(x, slot_tok, block_nsub, p):
    n, d = x.shape
    return pl.pallas_call(
        _gather_kernel,
        out_shape=jax.ShapeDtypeStruct((p, d), BF16),
        grid_spec=pltpu.PrefetchScalarGridSpec(
            num_scalar_prefetch=2,
            grid=(p // GATHER_ROWS,),
            in_specs=[pl.BlockSpec(memory_space=pl.ANY)],
            out_specs=pl.BlockSpec((GATHER_ROWS, d), lambda s, tok, ns: (s, 0)),
            scratch_shapes=[pltpu.VMEM((2, GATHER_ROWS, d), F32), pltpu.SemaphoreType.DMA((2,))]),
        compiler_params=_params(("arbitrary",)),
        name="moe_gather",
    )(slot_tok, block_nsub, x)


def _combine_kernel(p0_ref, p1_ref, y_hbm, x_ref, ple_ref, gate_ref, lg_ref, lb_ref,
                    o_ref, ob_ref, buf, sem):
    i = pl.program_id(0)
    tm = x_ref.shape[0]

    def issue(tile):
        slot = tile % 2
        base = tile * tm

        def row(r, carry):
            pltpu.make_async_copy(y_hbm.at[pl.ds(p0_ref[base + r], 1), :], buf.at[slot, 0, pl.ds(r, 1), :],
                                  sem.at[slot]).start(priority=0)
            pltpu.make_async_copy(y_hbm.at[pl.ds(p1_ref[base + r], 1), :], buf.at[slot, 1, pl.ds(r, 1), :],
                                  sem.at[slot]).start(priority=1)
            return carry

        lax.fori_loop(0, tm, row, 0, unroll=4)

    @pl.when(i == 0)
    def _():
        issue(0)

    @pl.when(i + 1 < pl.num_programs(0))
    def _():
        issue(i + 1)

    slot = i % 2
    pltpu.make_async_copy(y_hbm.at[pl.ds(0, tm), :], buf.at[slot, 0], sem.at[slot]).wait()
    pltpu.make_async_copy(y_hbm.at[pl.ds(0, tm), :], buf.at[slot, 1], sem.at[slot]).wait()
    gate = gate_ref[...]
    f = gate[:, 0:1] * buf[slot, 0] + gate[:, 1:2] * buf[slot, 1]
    out = _ln_body(DEEPNORM_ALPHA * x_ref[...] + f + ple_ref[...].astype(F32), lg_ref[...], lb_ref[...])
    o_ref[...] = out
    ob_ref[...] = out.astype(BF16)


def _combine_ln(y, pos0, pos1, gate, x, ple, ln_g, ln_b):
    n, d = x.shape
    tm = min(LN_ROWS, n)
    tile = lambda: pl.BlockSpec((tm, d), lambda i, a, b: (i, 0))
    row = lambda: pl.BlockSpec((1, d), lambda i, a, b: (0, 0))
    return pl.pallas_call(
        _combine_kernel,
        out_shape=(jax.ShapeDtypeStruct((n, d), F32), jax.ShapeDtypeStruct((n, d), BF16)),
        grid_spec=pltpu.PrefetchScalarGridSpec(
            num_scalar_prefetch=2,
            grid=(n // tm,),
            in_specs=[pl.BlockSpec(memory_space=pl.ANY), tile(), tile(),
                      pl.BlockSpec((tm, LANES), lambda i, a, b: (i, 0)), row(), row()],
            out_specs=(tile(), tile()),
            scratch_shapes=[pltpu.VMEM((2, 2, tm, d), F32), pltpu.SemaphoreType.DMA((2,))]),
        compiler_params=_params(("arbitrary",)),
        name="moe_combine_ln",
    )(pos0, pos1, y, x, ple, gate, ln_g.reshape(1, d), ln_b.reshape(1, d))


def _moe_plan(meta, counts, n_experts):
    n = meta.shape[1]
    nk = n * TOP_K
    nblk = (nk - n_experts) // FFN_ROWS + n_experts
    sub_per_blk = FFN_ROWS // FFN_SUB
    expert = meta[:TOP_K]
    rank = meta[TOP_K:2 * TOP_K]
    nsub_e = (counts + FFN_SUB - 1) // FFN_SUB
    nblk_e = (nsub_e + sub_per_blk - 1) // sub_per_blk
    rows_e = jnp.maximum((nsub_e + nblk_e - 1) // jnp.maximum(nblk_e, 1), 1) * FFN_SUB
    blk_end = jnp.cumsum(nblk_e)
    blk_off = blk_end - nblk_e
    onehot = expert[None] == jnp.arange(n_experts, dtype=jnp.int32)[:, None, None]
    rpb = jnp.sum(jnp.where(onehot, rows_e[:, None, None], 0), axis=0)
    first = jnp.sum(jnp.where(onehot, blk_off[:, None, None], 0), axis=0)
    pos = (first + rank // rpb) * FFN_ROWS + rank % rpb
    blk = jnp.arange(nblk, dtype=jnp.int32)
    used = blk < blk_end[-1]
    block_e = jnp.clip(jnp.searchsorted(blk_end, blk, side="right"), 0, n_experts - 1).astype(jnp.int32)
    block_e = jnp.where(used, block_e, block_e[jnp.maximum(blk_end[-1] - 1, 0)])
    rows = jnp.clip(counts[block_e] - (blk - blk_off[block_e]) * rows_e[block_e], 0, rows_e[block_e])
    rows = jnp.where(used, rows, 0)
    block_nsub = ((rows + FFN_SUB - 1) // FFN_SUB).astype(jnp.int32)
    tok = jnp.tile(jnp.arange(n, dtype=jnp.int32), TOP_K)
    slot_tok = jnp.zeros((nblk * FFN_ROWS,), jnp.int32).at[pos.reshape(nk)].set(
        tok, unique_indices=True)
    return pos.astype(jnp.int32), slot_tok, block_e, block_nsub, nblk * FFN_ROWS


def _token_mixer(xb, layer, w_in, w_br_ret, w_br_sb, w_gate, b_gate, batch, seq):
    d = xb.shape[1]
    in_width = w_in.shape[2]
    rw = RET_HEADS * HEAD_DIM
    sw = SB_HEADS * HEAD_DIM
    col_scale = jnp.ones((in_width,), F32)
    col_scale = col_scale.at[rw:2 * rw].set(HEAD_DIM ** -0.5)
    col_scale = col_scale.at[4 * rw:4 * rw + sw].set(HEAD_DIM ** -0.5 * math.log2(math.e))
    h = _mm([xb[None]], [(0, w_in, 0)], [(col_scale.reshape(1, 1, in_width), 0)], layer, _ep_colscale,
            in_width, BF16, 2048, 1024, "in_proj")
    o_ret = _retention(h, batch, seq)
    o_sb = _stick_breaking(h, batch, seq, (4 * RET_HEADS) // SB_HEADS)
    tn = min(512, d)
    bg = b_gate.reshape(b_gate.shape[0], 1, 2 * d)
    return _mm([o_ret[None], o_sb[None], xb[None]],
               [(0, w_br_ret, 0), (1, w_br_sb, 0), (2, w_gate, 0), (2, w_gate, d // tn)],
               [(bg, 0), (bg, d // tn)], layer, _ep_merge, d, BF16, 1024, tn, "branch_merge",
               single_buffer_b=True)


def kernel(x, p, w_in, w_br_ret, w_br_sb, w_gate, b_gate, w_o, ln1_g, ln1_b,
           ffn_w1, ffn_w3, ffn_w2, moe_router, moe_w1, moe_w3, moe_w2,
           ple_w, ple_gate_w, ln2_g, ln2_b):
    batch, seq, d = x.shape
    n = batch * seq
    depth = w_in.shape[0]
    n_experts = moe_router.shape[2]
    xf = x.reshape(n, d)
    xb = xf.astype(BF16)
    pf = p.reshape(depth, n, p.shape[3])
    ew1 = moe_w1.reshape((-1,) + moe_w1.shape[2:])
    ew3 = moe_w3.reshape((-1,) + moe_w3.shape[2:])
    ew2 = moe_w2.reshape((-1,) + moe_w2.shape[2:])
    for i in range(depth):
        merged = _token_mixer(xb, i, w_in, w_br_ret, w_br_sb, w_gate, b_gate, batch, seq)
        xf, xb = _proj_ln(merged, w_o, i, xf, ln1_g[i], ln1_b[i], "out_proj_ln")
        ple = _mm([xb[None], pf], [(0, ple_gate_w, 0), (1, ple_w, 0)], [], i, _ep_ple, d, BF16, 1024, 1024, "ple")
        if i % 2 == 0:
            nblk = n // min(FFN_ROWS, n)
            f = _ffn(xb, ffn_w1, ffn_w3, ffn_w2, jnp.full((nblk,), i // 2, jnp.int32),
                     jnp.full((nblk,), FFN_ROWS // FFN_SUB, jnp.int32), "dense_swiglu", all_rows=True)
            xf, xb = _ln(xf, [f, ple], ln2_g[i], ln2_b[i], "ln_ffn")
        else:
            meta, gate, counts = _route(xf, moe_router[i // 2])
            pos, slot_tok, block_e, block_nsub, slots = _moe_plan(meta, counts, n_experts)
            xs = _gather_rows(xf, slot_tok, block_nsub, slots)
            y = _ffn(xs, ew1, ew3, ew2, block_e + (i // 2) * n_experts, block_nsub, "expert_swiglu")
            xf, xb = _combine_ln(y, pos[0], pos[1], gate, xf, ple, ln2_g[i], ln2_b[i])
    return xf.reshape(batch, seq, d)
```
